```python
import math
import jax
import jax.numpy as jnp
from jax import lax
import numpy as np

D_MODEL = 1024
BATCH = 8
SEQ = 2048
DEPTH = 1
DEC_BATCH = 128
DEC_SEQ = 4
PAST_LEN = 8192
PAGE_SIZE = 128

MIX_WIDTH = D_MODEL
ATTN_HEADS = 8
ATTN_KV_HEADS = 2
HEAD_DIM = 64
GQA_GROUP = ATTN_HEADS // ATTN_KV_HEADS
ATTN_WIDTH = ATTN_HEADS * HEAD_DIM
KV_WIDTH = ATTN_KV_HEADS * HEAD_DIM
WINDOW = 128
ATTN_BLOCK = WINDOW
ATTN_SCALE = HEAD_DIM ** -0.5
REL_BUCKETS = 32
REL_MAX_DIST = WINDOW
HG_WIDTH = MIX_WIDTH - ATTN_WIDTH
HG_HEADS = 4
HG_DK = HG_WIDTH // HG_HEADS
HG_DV = HG_WIDTH // HG_HEADS
HG_CHUNK = 64
PEER_HEADS = 8
PEER_NKEYS = 128
PEER_EXPERTS = PEER_NKEYS * PEER_NKEYS
PEER_DKEY = 256
PEER_HALF = PEER_DKEY // 2
PEER_TOPK = 16
PEER_BLOCK = 128
EPS = 1e-6
IN_WIDTH = ATTN_WIDTH + 2 * KV_WIDTH + 4 * HG_WIDTH
SPLITS = [ATTN_WIDTH, ATTN_WIDTH + KV_WIDTH, ATTN_WIDTH + 2 * KV_WIDTH,
          ATTN_WIDTH + 2 * KV_WIDTH + HG_WIDTH, ATTN_WIDTH + 2 * KV_WIDTH + 2 * HG_WIDTH,
          ATTN_WIDTH + 2 * KV_WIDTH + 3 * HG_WIDTH]
F32 = jnp.float32

kernel_name = 'hymba_swa_hgrn2_peer_step'


def rmsnorm(x, w):
    xf = x.astype(F32)
    y = xf * lax.rsqrt(jnp.mean(xf * xf, axis=-1, keepdims=True) + EPS)
    return (y * w.astype(F32)).astype(x.dtype)


def t5_bucket(dist):
    n = jnp.maximum(dist, 0)
    max_exact = REL_BUCKETS // 2
    nf = jnp.maximum(n, 1).astype(F32)
    large = max_exact + (jnp.log(nf / max_exact) / math.log(REL_MAX_DIST / max_exact)
                         * (REL_BUCKETS - max_exact)).astype(jnp.int32)
    large = jnp.minimum(large, REL_BUCKETS - 1)
    return jnp.where(n < max_exact, n, large)


def rel_bias(dist, table):
    q_len, k_len = dist.shape
    b = table.astype(F32)[t5_bucket(dist)]
    return jnp.transpose(b, (2, 0, 1)).reshape(ATTN_KV_HEADS, GQA_GROUP, q_len, k_len)


def sink_softmax(s, mask, sinks):
    sink = sinks.astype(F32).reshape(ATTN_KV_HEADS, GQA_GROUP, 1, 1)
    s = jnp.where(mask, s, -jnp.inf)
    m = jnp.maximum(jnp.max(s, axis=-1, keepdims=True), sink)
    p = jnp.exp(s - m)
    return p / (jnp.sum(p, axis=-1, keepdims=True) + jnp.exp(sink - m))


def swa_prompt(q, k, v, sinks, rel_table):
    bsz, seq = q.shape[:2]
    nb = seq // ATTN_BLOCK
    qb = q.reshape(bsz, nb, ATTN_BLOCK, ATTN_KV_HEADS, GQA_GROUP, HEAD_DIM)
    kb = k.reshape(bsz, nb, ATTN_BLOCK, ATTN_KV_HEADS, HEAD_DIM)
    vb = v.reshape(bsz, nb, ATTN_BLOCK, ATTN_KV_HEADS, HEAD_DIM)
    pad = ((0, 0), (1, 0), (0, 0), (0, 0), (0, 0))
    kk = jnp.concatenate([jnp.pad(kb[:, :-1], pad), kb], axis=2)
    vv = jnp.concatenate([jnp.pad(vb[:, :-1], pad), vb], axis=2)
    s = jnp.einsum('bnqhgd,bnkhd->bnhgqk', qb, kk).astype(F32) * ATTN_SCALE
    dist = (jnp.arange(ATTN_BLOCK)[:, None] + ATTN_BLOCK) - jnp.arange(2 * ATTN_BLOCK)[None, :]
    kpos = jnp.arange(nb)[:, None] * ATTN_BLOCK - ATTN_BLOCK + jnp.arange(2 * ATTN_BLOCK)[None, :]
    mask = ((dist >= 0) & (dist <= WINDOW))[None] & (kpos >= 0)[:, None, :]
    p = sink_softmax(s + rel_bias(dist, rel_table), mask[:, None, None], sinks)
    o = jnp.einsum('bnhgqk,bnkhd->bnqhgd', p, vv.astype(F32))
    return o.reshape(bsz, seq, ATTN_WIDTH)


def swa_sample(q, k, v, k_buf, v_buf, sinks, rel_table):
    bsz, t = q.shape[:2]
    wb = k_buf.shape[1]
    kk = jnp.concatenate([k_buf.astype(k.dtype), k], axis=1)
    vv = jnp.concatenate([v_buf.astype(v.dtype), v], axis=1)
    qg = q.reshape(bsz, t, ATTN_KV_HEADS, GQA_GROUP, HEAD_DIM)
    s = jnp.einsum('bqhgd,bkhd->bhgqk', qg, kk).astype(F32) * ATTN_SCALE
    dist = (wb + jnp.arange(t))[:, None] - jnp.arange(wb + t)[None, :]
    mask = (dist >= 0) & (dist <= WINDOW)
    p = sink_softmax(s + rel_bias(dist, rel_table), mask, sinks)
    o = jnp.einsum('bhgqk,bkhd->bqhgd', p, vv.astype(F32)).reshape(bsz, t, ATTN_WIDTH)
    return o, kk[:, t:], vv[:, t:]


def hgrn2_chunked(q, k, logf, v, s0, chunk):
    bsz, t = q.shape[:2]
    n = t // chunk

    def split(a):
        a = a.reshape(bsz, n, chunk, HG_HEADS, a.shape[-1])
        return jnp.transpose(a, (1, 0, 3, 2, 4))

    idx = jnp.arange(chunk)
    causal = (idx[:, None] >= idx[None, :])[:, :, None]

    def step(state, inp):
        qc, kc, gc, vc = inp
        b = jnp.cumsum(gc, axis=-2)
        dec = jnp.exp(jnp.where(causal, b[:, :, :, None, :] - b[:, :, None, :, :], -jnp.inf))
        a = jnp.einsum('bhtk,bhtsk,bhsk->bhts', qc, dec, kc)
        o = jnp.einsum('bhts,bhsv->bhtv', a, vc) + jnp.einsum('bhtk,bhkv->bhtv', qc * jnp.exp(b), state)
        b_last = b[:, :, -1:, :]
        state = jnp.exp(b_last[:, :, 0, :])[..., None] * state + jnp.einsum(
            'bhsk,bhsv->bhkv', kc * jnp.exp(b_last - b), vc)
        return state, o

    s_new, o = lax.scan(step, s0, (split(q), split(k), split(logf), split(v)))
    o = jnp.transpose(o, (1, 0, 3, 2, 4)).reshape(bsz, t, HG_HEADS, HG_DV)
    return o, s_new


def peer(x2d, w_q, sub_keys, u, v):
    n_tok, d = x2d.shape
    nblk = -(-n_tok // PEER_BLOCK)
    xp = jnp.pad(x2d, ((0, nblk * PEER_BLOCK - n_tok), (0, 0))).reshape(nblk, PEER_BLOCK, d)
    keys = sub_keys.astype(F32)

    def blk(xb):
        qry = (xb @ w_q).astype(F32).reshape(PEER_BLOCK, PEER_HEADS, 2, PEER_HALF)
        sc = jnp.einsum('thcd,hcnd->thcn', qry, keys)
        sv, si = lax.top_k(sc, PEER_TOPK)
        cand = sv[:, :, 0, :, None] + sv[:, :, 1, None, :]
        cidx = si[:, :, 0, :, None] * PEER_NKEYS + si[:, :, 1, None, :]
        fv, fi = lax.top_k(cand.reshape(PEER_BLOCK, PEER_HEADS, PEER_TOPK * PEER_TOPK), PEER_TOPK)
        eidx = jnp.take_along_axis(cidx.reshape(PEER_BLOCK, PEER_HEADS, PEER_TOPK * PEER_TOPK), fi, axis=-1)
        gate = jax.nn.softmax(fv, axis=-1)
        ue = u[eidx]
        ve = v[eidx]
        act = jax.nn.gelu(jnp.einsum('thkd,td->thk', ue, xb).astype(F32), approximate=False)
        return jnp.einsum('thk,thkd->td', (gate * act).astype(xb.dtype), ve)

    y = lax.map(blk, xp).reshape(nblk * PEER_BLOCK, d)
    return y[:n_tok]


def decoder_layer(x, cache, lb, norm_mix_w, w_in, sinks, rel_table, hg_norm_w, w_o,
                  norm_ffn_w, peer_w_q, peer_sub_keys, peer_u, peer_v):
    bsz, t, _ = x.shape
    xn = rmsnorm(x, norm_mix_w)
    proj = xn @ w_in
    q_a, k_a, v_a, q_h, f_h, i_h, g_h = jnp.split(proj, SPLITS, axis=-1)
    q_a = q_a.reshape(bsz, t, ATTN_HEADS, HEAD_DIM)
    k_a = k_a.reshape(bsz, t, ATTN_KV_HEADS, HEAD_DIM)
    v_a = v_a.reshape(bsz, t, ATTN_KV_HEADS, HEAD_DIM)
    if cache is None:
        attn = swa_prompt(q_a, k_a, v_a, sinks, rel_table)
        wb = min(WINDOW, t)
        k_new, v_new = k_a[:, t - wb:], v_a[:, t - wb:]
        s0 = jnp.zeros((bsz, HG_HEADS, HG_DK, HG_DV), F32)
    else:
        k_buf, v_buf, s_prev = cache
        attn, k_new, v_new = swa_sample(q_a, k_a, v_a, k_buf, v_buf, sinks, rel_table)
        s0 = s_prev.astype(F32)
    fgate = lb + (1.0 - lb) * jax.nn.sigmoid(f_h.astype(F32))
    heads = lambda a: a.reshape(bsz, t, HG_HEADS, -1)
    o_h, s_new = hgrn2_chunked(heads(jax.nn.silu(q_h.astype(F32))), heads(1.0 - fgate),
                               heads(jnp.log(fgate)), heads(i_h.astype(F32)), s0,
                               math.gcd(t, HG_CHUNK))
    o_h = o_h * lax.rsqrt(jnp.mean(o_h * o_h, axis=-1, keepdims=True) + EPS)
    o_h = o_h.reshape(bsz, t, HG_WIDTH) * hg_norm_w.astype(F32) * jax.nn.silu(g_h.astype(F32))
    mix = jnp.concatenate([attn.astype(x.dtype), o_h.astype(x.dtype)], axis=-1) @ w_o
    h = x + mix
    hn = rmsnorm(h, norm_ffn_w)
    y = h + peer(hn.reshape(bsz * t, D_MODEL), peer_w_q, peer_sub_keys, peer_u, peer_v).reshape(bsz, t, D_MODEL)
    return y, k_new.astype(x.dtype), v_new.astype(x.dtype), s_new.astype(x.dtype)


def setup_inputs(seed: int = 0) -> dict:
    key = jax.random.key(seed)
    ks = jax.random.split(key, 18)

    def nrm(k, shape, scale):
        return jax.random.normal(k, shape, F32) * scale

    win_buf = min(WINDOW, PAST_LEN)
    return {
        'x_prompt': nrm(ks[0], (BATCH, SEQ, D_MODEL), 1.0),
        'x_sample': nrm(ks[1], (DEC_BATCH, DEC_SEQ, D_MODEL), 1.0),
        'cache_k_win': nrm(ks[2], (DEPTH, DEC_BATCH, win_buf, ATTN_KV_HEADS, HEAD_DIM), 1.0),
        'cache_v_win': nrm(ks[3], (DEPTH, DEC_BATCH, win_buf, ATTN_KV_HEADS, HEAD_DIM), 1.0),
        'state_hgrn': nrm(ks[4], (DEPTH, DEC_BATCH, HG_HEADS, HG_DK, HG_DV), 0.5),
        'norm_mix_w': 1.0 + nrm(ks[5], (DEPTH, D_MODEL), 0.02),
        'w_in': nrm(ks[6], (DEPTH, D_MODEL, IN_WIDTH), D_MODEL ** -0.5),
        'attn_sinks': nrm(ks[7], (DEPTH, ATTN_HEADS), 0.5),
        'rel_bias_table': nrm(ks[8], (REL_BUCKETS, ATTN_HEADS), 0.5),
        'hg_lb': nrm(ks[9], (DEPTH + 1, HG_WIDTH), 0.1),
        'hg_norm_w': 1.0 + nrm(ks[10], (DEPTH, HG_WIDTH), 0.02),
        'w_o': nrm(ks[11], (DEPTH, MIX_WIDTH, D_MODEL), MIX_WIDTH ** -0.5),
        'norm_ffn_w': 1.0 + nrm(ks[12], (DEPTH, D_MODEL), 0.02),
        'peer_w_q': nrm(ks[13], (DEPTH, D_MODEL, PEER_HEADS * PEER_DKEY), D_MODEL ** -0.5),
        'peer_sub_keys': nrm(ks[14], (DEPTH, PEER_HEADS, 2, PEER_NKEYS, PEER_HALF), PEER_HALF ** -0.5),
        'peer_u': nrm(ks[15], (DEPTH, PEER_EXPERTS, D_MODEL), D_MODEL ** -0.5),
        'peer_v': nrm(ks[16], (DEPTH, PEER_EXPERTS, D_MODEL), 0.5),
        'norm_final_w': 1.0 + nrm(ks[17], (D_MODEL,), 0.02),
    }


def reference(x_prompt, x_sample, cache_k_win, cache_v_win, state_hgrn, norm_mix_w, w_in,
              attn_sinks, rel_bias_table, hg_lb, hg_norm_w, w_o, norm_ffn_w, peer_w_q,
              peer_sub_keys, peer_u, peer_v, norm_final_w):
    lower_bounds = jnp.cumsum(jax.nn.softmax(hg_lb.astype(F32), axis=0), axis=0)
    xp, xs = x_prompt, x_sample
    kp_l, vp_l, sp_l, ks_l, vs_l, ss_l = [], [], [], [], [], []
    for l in range(DEPTH):
        weights = (lower_bounds[l], norm_mix_w[l], w_in[l], attn_sinks[l], rel_bias_table, hg_norm_w[l],
                   w_o[l], norm_ffn_w[l], peer_w_q[l], peer_sub_keys[l], peer_u[l], peer_v[l])
        xp, kp, vp, sp = decoder_layer(xp, None, *weights)
        xs, ksm, vsm, ssm = decoder_layer(xs, (cache_k_win[l], cache_v_win[l], state_hgrn[l]), *weights)
        kp_l.append(kp)
        vp_l.append(vp)
        sp_l.append(sp)
        ks_l.append(ksm)
        vs_l.append(vsm)
        ss_l.append(ssm)
    y_prompt = rmsnorm(xp, norm_final_w)
    y_sample = rmsnorm(xs, norm_final_w)
    return (y_prompt, y_sample, jnp.stack(kp_l), jnp.stack(vp_l), jnp.stack(sp_l),
            jnp.stack(ks_l), jnp.stack(vs_l), jnp.stack(ss_l))
```

```python
import functools
import math

import jax
import jax.numpy as jnp
from jax import lax
from jax.experimental import pallas as pl
from jax.experimental.pallas import tpu as pltpu

F32 = jnp.float32
BF16 = jnp.bfloat16

EPS = 1e-6
NEG = -1e30

ATTN_HEADS = 8
ATTN_KV_HEADS = 2
HEAD_DIM = 64
WINDOW = 128
REL_BUCKETS = 32
HG_HEADS = 4
HG_CHUNK = 64
PEER_HEADS = 8
PEER_TOPK = 16

LANES = 128
SUBLANES = 8
VMEM_LIMIT = 56 * 1024 * 1024


def _cparams(*sem):
    return pltpu.CompilerParams(dimension_semantics=sem, vmem_limit_bytes=VMEM_LIMIT)


def _nt(a, b):
    return lax.dot_general(a, b, (((1,), (1,)), ((), ())), preferred_element_type=F32)


def _tn(a, b):
    return lax.dot_general(a, b, (((0,), (0,)), ((), ())), preferred_element_type=F32)


def _dot(a, b):
    return jnp.dot(a, b, preferred_element_type=F32)


def _sigmoid(x):
    return 1.0 / (1.0 + jnp.exp(-x))


def _in_proj_kernel(x_ref, nw_ref, w_ref, o_ref):
    x = x_ref[...]
    xn = x * lax.rsqrt(jnp.mean(x * x, axis=-1, keepdims=True) + EPS) * nw_ref[...]
    o_ref[...] = _dot(xn.astype(BF16), w_ref[...])


def _in_proj(x2d, norm_w, w_bf16, tm):
    t, d = x2d.shape
    n = w_bf16.shape[1]
    return pl.pallas_call(
        _in_proj_kernel,
        grid=(t // tm,),
        in_specs=[pl.BlockSpec((tm, d), lambda i: (i, 0)),
                  pl.BlockSpec((1, d), lambda i: (0, 0)),
                  pl.BlockSpec((d, n), lambda i: (0, 0))],
        out_specs=pl.BlockSpec((tm, n), lambda i: (i, 0)),
        out_shape=jax.ShapeDtypeStruct((t, n), F32),
        compiler_params=_cparams("parallel"),
        name="in_proj",
    )(x2d, norm_w.reshape(1, d), w_bf16)


def _t5_bucket(dist):
    n = jnp.maximum(dist, 0)
    max_exact = REL_BUCKETS // 2
    nf = jnp.maximum(n, 1).astype(F32)
    large = max_exact + (jnp.log(nf / max_exact) / math.log(WINDOW / max_exact)
                         * (REL_BUCKETS - max_exact)).astype(jnp.int32)
    large = jnp.minimum(large, REL_BUCKETS - 1)
    return jnp.where(n < max_exact, n, large)


def _masked_bias(dist, mask, table):
    b = jnp.transpose(table.astype(F32)[_t5_bucket(dist)], (2, 0, 1))
    return jnp.where(mask[None], b, NEG)


def _swa_prompt_kernel(sink_ref, q_ref, kp_ref, kc_ref, vp_ref, vc_ref, bias_ref, o_ref):
    first = pl.program_id(1) == 0
    scale = HEAD_DIM ** -0.5
    group = ATTN_HEADS // ATTN_KV_HEADS
    blk = q_ref.shape[1]
    for h in range(ATTN_HEADS):
        kv = (h // group) * HEAD_DIM
        qh = q_ref[0, :, h * HEAD_DIM:(h + 1) * HEAD_DIM].astype(BF16)
        kp = kp_ref[0, :, kv:kv + HEAD_DIM].astype(BF16)
        kc = kc_ref[0, :, kv:kv + HEAD_DIM].astype(BF16)
        vp = vp_ref[0, :, kv:kv + HEAD_DIM].astype(BF16)
        vc = vc_ref[0, :, kv:kv + HEAD_DIM].astype(BF16)
        sp = _nt(qh, kp) * scale + bias_ref[h, :, :blk]
        sc = _nt(qh, kc) * scale + bias_ref[h, :, blk:]
        sp = jnp.where(first, NEG, sp)
        sink = sink_ref[h]
        m = jnp.maximum(jnp.maximum(jnp.max(sp, axis=-1, keepdims=True),
                                    jnp.max(sc, axis=-1, keepdims=True)), sink)
        pp = jnp.exp(sp - m)
        pc = jnp.exp(sc - m)
        den = (jnp.sum(pp, axis=-1, keepdims=True) + jnp.sum(pc, axis=-1, keepdims=True)
               + jnp.exp(sink - m))
        o = (_dot(pp.astype(BF16), vp) + _dot(pc.astype(BF16), vc)) / den
        o_ref[0, :, h * HEAD_DIM:(h + 1) * HEAD_DIM] = o.astype(o_ref.dtype)


def _swa_prompt(proj3, sinks, bias, col_q, col_k, col_v):
    bsz, seq, _ = proj3.shape
    blk = WINDOW
    aw = ATTN_HEADS * HEAD_DIM
    kw = ATTN_KV_HEADS * HEAD_DIM
    prev = lambda b, n: (b, jnp.maximum(n - 1, 0))
    return pl.pallas_call(
        _swa_prompt_kernel,
        grid=(bsz, seq // blk),
        in_specs=[pl.BlockSpec(memory_space=pltpu.SMEM),
                  pl.BlockSpec((1, blk, aw), lambda b, n: (b, n, col_q)),
                  pl.BlockSpec((1, blk, kw), lambda b, n: prev(b, n) + (col_k,)),
                  pl.BlockSpec((1, blk, kw), lambda b, n: (b, n, col_k)),
                  pl.BlockSpec((1, blk, kw), lambda b, n: prev(b, n) + (col_v,)),
                  pl.BlockSpec((1, blk, kw), lambda b, n: (b, n, col_v)),
                  pl.BlockSpec((ATTN_HEADS, blk, 2 * blk), lambda b, n: (0, 0, 0))],
        out_specs=pl.BlockSpec((1, blk, aw), lambda b, n: (b, n, 0)),
        out_shape=jax.ShapeDtypeStruct((bsz, seq, aw), BF16),
        compiler_params=_cparams("parallel", "arbitrary"),
        name="swa_prompt",
    )(sinks, proj3, proj3, proj3, proj3, proj3, bias)


def _swa_sample_kernel(sink_ref, q_ref, kk_ref, vv_ref, bias_ref, o_ref):
    scale = HEAD_DIM ** -0.5
    group = ATTN_HEADS // ATTN_KV_HEADS
    for h in range(ATTN_HEADS):
        kv = (h // group) * HEAD_DIM
        qh = q_ref[:, :, h * HEAD_DIM:(h + 1) * HEAD_DIM].astype(BF16)
        kh = kk_ref[:, :, kv:kv + HEAD_DIM].astype(BF16)
        vh = vv_ref[:, :, kv:kv + HEAD_DIM].astype(BF16)
        s = jnp.einsum('bqd,bkd->bqk', qh, kh, preferred_element_type=F32) * scale + bias_ref[h][None]
        sink = sink_ref[h]
        m = jnp.maximum(jnp.max(s, axis=-1, keepdims=True), sink)
        p = jnp.exp(s - m)
        den = jnp.sum(p, axis=-1, keepdims=True) + jnp.exp(sink - m)
        o = jnp.einsum('bqk,bkd->bqd', p.astype(BF16), vh, preferred_element_type=F32) / den
        o_ref[:, :, h * HEAD_DIM:(h + 1) * HEAD_DIM] = o.astype(o_ref.dtype)


def _swa_sample(proj3, kk, vv, sinks, bias, col_q, bb):
    bsz, tp, _ = proj3.shape
    kp = kk.shape[1]
    aw = ATTN_HEADS * HEAD_DIM
    kw = ATTN_KV_HEADS * HEAD_DIM
    return pl.pallas_call(
        _swa_sample_kernel,
        grid=(bsz // bb,),
        in_specs=[pl.BlockSpec(memory_space=pltpu.SMEM),
                  pl.BlockSpec((bb, tp, aw), lambda b: (b, 0, col_q)),
                  pl.BlockSpec((bb, kp, kw), lambda b: (b, 0, 0)),
                  pl.BlockSpec((bb, kp, kw), lambda b: (b, 0, 0)),
                  pl.BlockSpec((ATTN_HEADS, tp, kp), lambda b: (0, 0, 0))],
        out_specs=pl.BlockSpec((bb, tp, aw), lambda b: (b, 0, 0)),
        out_shape=jax.ShapeDtypeStruct((bsz, tp, aw), BF16),
        compiler_params=_cparams("parallel"),
        name="swa_sample",
    )(sinks, proj3, kk, vv, bias)


def _split3(x):
    hi = x.astype(BF16)
    r = x - hi.astype(F32)
    mid = r.astype(BF16)
    lo = (r - mid.astype(F32)).astype(BF16)
    return hi, mid, lo


def _hgrn_kernel(*refs, t_valid, has_state):
    if has_state:
        q_ref, f_ref, i_ref, g_ref, lb_ref, nw_ref, s0_ref, o_ref, s_ref, st_scr = refs
    else:
        q_ref, f_ref, i_ref, g_ref, lb_ref, nw_ref, o_ref, s_ref, st_scr = refs
    bb, chunk, width = q_ref.shape
    dk = width // HG_HEADS
    c = pl.program_id(1)

    @pl.when(c == 0)
    def _init():
        if has_state:
            def load(b, carry):
                for h in range(HG_HEADS):
                    st_scr[b, h] = s0_ref[b, h].T
                return carry
            lax.fori_loop(0, bb, load, 0)
        else:
            st_scr[...] = jnp.zeros_like(st_scr)

    row = lax.broadcasted_iota(jnp.int32, (chunk, chunk), 0)
    col = lax.broadcasted_iota(jnp.int32, (chunk, chunk), 1)
    causal = row >= col
    tri = jnp.where(causal, 1.0, 0.0).astype(BF16)
    valid = lax.broadcasted_iota(jnp.int32, (chunk, dk), 0) < t_valid
    mid_row = chunk // 2

    def body(b, carry):
        for h in range(HG_HEADS):
            sl = slice(h * dk, (h + 1) * dk)
            qx = q_ref[b, :, sl]
            q = qx * _sigmoid(qx)
            lb = lb_ref[:, sl]
            f = lb + (1.0 - lb) * _sigmoid(f_ref[b, :, sl])
            k = 1.0 - f
            lg = jnp.log(f)
            if t_valid < chunk:
                k = jnp.where(valid, k, 0.0)
                lg = jnp.where(valid, lg, 0.0)
            v = i_ref[b, :, sl].astype(BF16)
            cum = sum(_dot(tri, part) for part in _split3(lg))
            cum_mid = cum[mid_row:mid_row + 1, :]
            cum_last = cum[chunk - 1:chunk, :]
            qt = (q * jnp.exp(cum - cum_mid)).astype(BF16)
            kt = (k * jnp.exp(cum_mid - cum)).astype(BF16)
            a = jnp.where(causal, _nt(qt, kt), 0.0)
            st = st_scr[b, h]
            o = _dot(a.astype(BF16), v) + _nt((q * jnp.exp(cum)).astype(BF16), st.astype(BF16))
            kd = (k * jnp.exp(cum_last - cum)).astype(BF16)
            st_scr[b, h] = st * jnp.exp(cum_last) + _tn(v, kd)
            o = o * lax.rsqrt(jnp.mean(o * o, axis=-1, keepdims=True) + EPS)
            gx = g_ref[b, :, sl]
            o = o * nw_ref[:, sl] * (gx * _sigmoid(gx))
            o_ref[b, :, sl] = o.astype(o_ref.dtype)
        return carry

    lax.fori_loop(0, bb, body, 0)

    @pl.when(c == pl.num_programs(1) - 1)
    def _final():
        def store(b, carry):
            for h in range(HG_HEADS):
                s_ref[b, h] = st_scr[b, h].T
            return carry
        lax.fori_loop(0, bb, store, 0)


def _hgrn(proj3, lb, norm_w, s0, cols, chunk, t_valid, bb):
    bsz, t, _ = proj3.shape
    width = lb.shape[0]
    dk = width // HG_HEADS
    has_state = s0 is not None
    spec = lambda cb: pl.BlockSpec((bb, chunk, width), lambda b, c: (b, c, cb))
    vec = pl.BlockSpec((1, width), lambda b, c: (0, 0))
    st_spec = pl.BlockSpec((bb, HG_HEADS, dk, dk), lambda b, c: (b, 0, 0, 0))
    in_specs = [spec(cols[0]), spec(cols[1]), spec(cols[2]), spec(cols[3]), vec, vec]
    args = [proj3, proj3, proj3, proj3, lb.reshape(1, width), norm_w.reshape(1, width)]
    if has_state:
        in_specs.append(st_spec)
        args.append(s0)
    return pl.pallas_call(
        functools.partial(_hgrn_kernel, t_valid=t_valid, has_state=has_state),
        grid=(bsz // bb, t // chunk),
        in_specs=in_specs,
        out_specs=[pl.BlockSpec((bb, chunk, width), lambda b, c: (b, c, 0)), st_spec],
        out_shape=[jax.ShapeDtypeStruct((bsz, t, width), BF16),
                   jax.ShapeDtypeStruct((bsz, HG_HEADS, dk, dk), F32)],
        scratch_shapes=[pltpu.VMEM((bb, HG_HEADS, dk, dk), F32)],
        compiler_params=_cparams("parallel", "arbitrary"),
        name="hgrn_state" if has_state else "hgrn_prompt",
    )(*args)


def _out_proj_kernel(x_ref, a_ref, oh_ref, wo_ref, nw_ref, h_ref, hnt_ref):
    aw = a_ref.shape[1]
    mix = _dot(a_ref[...], wo_ref[:aw, :]) + _dot(oh_ref[...], wo_ref[aw:, :])
    h = x_ref[...] + mix
    h_ref[...] = h
    hn = h * lax.rsqrt(jnp.mean(h * h, axis=-1, keepdims=True) + EPS) * nw_ref[...]
    hnt_ref[...] = hn.T.astype(hnt_ref.dtype)


def _out_proj(x2d, attn, oh, wo_bf16, norm_w, tm):
    t, d = x2d.shape
    aw, hw = attn.shape[1], oh.shape[1]
    return pl.pallas_call(
        _out_proj_kernel,
        grid=(t // tm,),
        in_specs=[pl.BlockSpec((tm, d), lambda i: (i, 0)),
                  pl.BlockSpec((tm, aw), lambda i: (i, 0)),
                  pl.BlockSpec((tm, hw), lambda i: (i, 0)),
                  pl.BlockSpec((aw + hw, d), lambda i: (0, 0)),
                  pl.BlockSpec((1, d), lambda i: (0, 0))],
        out_specs=[pl.BlockSpec((tm, d), lambda i: (i, 0)),
                   pl.BlockSpec((d, tm), lambda i: (0, i))],
        out_shape=[jax.ShapeDtypeStruct((t, d), F32),
                   jax.ShapeDtypeStruct((d, t), BF16)],
        compiler_params=_cparams("parallel"),
        name="out_proj",
    )(x2d, attn, oh, wo_bf16, norm_w.reshape(1, d))


def _peer_scores_kernel(hnt_ref, wqt_ref, keys_ref, sc_ref):
    qt = _dot(wqt_ref[...], hnt_ref[...])
    half = keys_ref.shape[2]
    for hc in range(keys_ref.shape[0]):
        sc_ref[hc] = _dot(keys_ref[hc], qt[hc * half:(hc + 1) * half, :].astype(BF16))


def _peer_scores(hnt, wqt_bf16, keys_bf16, tb):
    d, t = hnt.shape
    nhc, nk, half = keys_bf16.shape
    return pl.pallas_call(
        _peer_scores_kernel,
        grid=(t // tb,),
        in_specs=[pl.BlockSpec((d, tb), lambda i: (0, i)),
                  pl.BlockSpec((nhc * half, d), lambda i: (0, 0)),
                  pl.BlockSpec((nhc, nk, half), lambda i: (0, 0, 0))],
        out_specs=pl.BlockSpec((nhc, nk, tb), lambda i: (0, 0, i)),
        out_shape=jax.ShapeDtypeStruct((nhc, nk, t), F32),
        compiler_params=_cparams("parallel"),
        name="peer_scores",
    )(hnt, wqt_bf16, keys_bf16)


def _sort16_pairs():
    def merge(lo, hi, r):
        step = r * 2
        if step < hi - lo:
            yield from merge(lo, hi, step)
            yield from merge(lo + r, hi, step)
            yield from [(i, i + r) for i in range(lo + r, hi - r, step)]
        else:
            yield (lo, lo + r)

    def sort(lo, hi):
        if hi - lo >= 1:
            mid = lo + (hi - lo) // 2
            yield from sort(lo, mid)
            yield from sort(mid + 1, hi)
            yield from merge(lo, hi, 1)

    return tuple(sort(0, PEER_TOPK - 1))


_SORT16 = _sort16_pairs()


def _bitonic_to_sorted(z):
    z = list(z)
    d = PEER_TOPK // 2
    while d >= 1:
        for i in range(PEER_TOPK):
            if i & d == 0:
                hi, lo = jnp.maximum(z[i], z[i + d]), jnp.minimum(z[i], z[i + d])
                z[i], z[i + d] = hi, lo
        d //= 2
    return z


def _merge_bitonic(top, other):
    z = list(top)
    m = len(other)
    for r in range(PEER_TOPK - m, PEER_TOPK):
        z[r] = jnp.maximum(top[r], other[PEER_TOPK - 1 - r])
    return z


def _top16_desc(x):
    n = x.shape[0] // SUBLANES
    xs = [x[g * SUBLANES:(g + 1) * SUBLANES, :] for g in range(n)]
    for i, j in _SORT16:
        xs[i], xs[j] = jnp.maximum(xs[i], xs[j]), jnp.minimum(xs[i], xs[j])
    shift = SUBLANES // 2
    while shift >= 1:
        ys = [pltpu.roll(v, shift, axis=0) for v in xs]
        xs = _bitonic_to_sorted(_merge_bitonic(xs, ys))
        shift //= 2
    return xs


def _peer_select(sc_ref, thr_scr, pw_scr, q_scr, h, l0):
    lanes = pl.ds(l0, LANES)
    s0 = sc_ref[2 * h, :, lanes]
    s1 = sc_ref[2 * h + 1, :, lanes]
    a = _top16_desc(s0)
    b = _top16_desc(s1)
    lists = [[a[r] + b[c] for c in range(PEER_TOPK // (r + 1))] for r in range(SUBLANES)]
    lists.append([a[r] + b[0] for r in range(SUBLANES, PEER_TOPK)])
    top = lists[0]
    for other in lists[1:-1]:
        top = _bitonic_to_sorted(_merge_bitonic(top, other))
    z = _merge_bitonic(top, lists[-1])
    tau = functools.reduce(jnp.minimum, z)
    best = a[0] + b[0]
    zsum = jnp.zeros_like(tau)
    for cand in lists:
        for v in cand:
            zsum = zsum + jnp.where(v >= tau, jnp.exp(v - best), 0.0)
    inv = 1.0 / zsum
    n = s0.shape[0] // SUBLANES
    for g in range(n):
        rows = slice(g * SUBLANES, (g + 1) * SUBLANES)
        x0 = s0[rows, :]
        thr = jnp.full_like(x0, jnp.inf)
        for c in range(PEER_TOPK):
            thr = jnp.where(x0 + b[c] >= tau, b[c], thr)
        thr_scr[h, rows, lanes] = thr
        pw_scr[h, rows, lanes] = jnp.exp(x0 - a[0]) * inv
        q_scr[h, rows, lanes] = jnp.exp(s1[rows, :] - b[0])


def _peer_dense_kernel(sc_ref, hnt_ref, u_ref, vt_ref, yt_ref, thr_scr, pw_scr, q_scr, h_scr, g_scr):
    e = pl.program_id(1)
    tb = hnt_ref.shape[1]
    nk = sc_ref.shape[1]
    n_i = u_ref.shape[0] // nk
    n_lt = tb // LANES

    @pl.when(e == 0)
    def _select():
        def body(it, carry):
            h = it // n_lt
            l0 = pl.multiple_of((it % n_lt) * LANES, LANES)
            _peer_select(sc_ref, thr_scr, pw_scr, q_scr, h, l0)
            return carry
        lax.fori_loop(0, PEER_HEADS * n_lt, body, 0)
        yt_ref[...] = jnp.zeros_like(yt_ref)

    h_scr[...] = _dot(u_ref[...], hnt_ref[...])

    i0 = pl.multiple_of(e * n_i, SUBLANES)

    def body(lt, carry):
        lanes = pl.ds(pl.multiple_of(lt * LANES, LANES), LANES)
        for ii in range(n_i):
            rows = slice(ii * nk, (ii + 1) * nk)
            w = jnp.zeros((nk, LANES), F32)
            for h in range(PEER_HEADS):
                thr = thr_scr[h, pl.ds(i0, n_i), lanes][ii:ii + 1, :]
                pw = pw_scr[h, pl.ds(i0, n_i), lanes][ii:ii + 1, :]
                w = w + jnp.where(sc_ref[2 * h + 1, :, lanes] >= thr, q_scr[h, :, lanes], 0.0) * pw
            x = h_scr[rows, lanes]
            act = 0.5 * x * (1.0 + lax.erf(x * (2.0 ** -0.5)))
            g_scr[rows, lanes] = (w * act).astype(g_scr.dtype)
        return carry

    lax.fori_loop(0, n_lt, body, 0)
    yt_ref[...] += _dot(vt_ref[...], g_scr[...])


def _peer_dense(sc, hnt, u_bf16, vt_bf16, tb, eb):
    nhc, nk, t = sc.shape
    d = hnt.shape[0]
    n_exp = u_bf16.shape[0]
    assert eb == SUBLANES * nk and n_exp == nk * nk and t % tb == 0 and tb % LANES == 0
    return pl.pallas_call(
        _peer_dense_kernel,
        grid=(t // tb, n_exp // eb),
        in_specs=[pl.BlockSpec((nhc, nk, tb), lambda i, e: (0, 0, i)),
                  pl.BlockSpec((d, tb), lambda i, e: (0, i)),
                  pl.BlockSpec((eb, d), lambda i, e: (e, 0)),
                  pl.BlockSpec((d, eb), lambda i, e: (0, e))],
        out_specs=pl.BlockSpec((d, tb), lambda i, e: (0, i)),
        out_shape=jax.ShapeDtypeStruct((d, t), F32),
        scratch_shapes=[pltpu.VMEM((PEER_HEADS, nk, tb), F32),
                        pltpu.VMEM((PEER_HEADS, nk, tb), F32),
                        pltpu.VMEM((PEER_HEADS, nk, tb), F32),
                        pltpu.VMEM((eb, tb), F32),
                        pltpu.VMEM((eb, tb), BF16)],
        compiler_params=_cparams("parallel", "arbitrary"),
        name="peer_dense",
    )(sc, hnt, u_bf16, vt_bf16)


def _final_kernel(h_ref, yt_ref, nw_ref, o_ref):
    y = h_ref[...] + yt_ref[...].T
    o_ref[...] = y * lax.rsqrt(jnp.mean(y * y, axis=-1, keepdims=True) + EPS) * nw_ref[...]


def _final(h2d, yt, norm_w, tm):
    t, d = h2d.shape
    return pl.pallas_call(
        _final_kernel,
        grid=(t // tm,),
        in_specs=[pl.BlockSpec((tm, d), lambda i: (i, 0)),
                  pl.BlockSpec((d, tm), lambda i: (0, i)),
                  pl.BlockSpec((1, d), lambda i: (0, 0))],
        out_specs=pl.BlockSpec((tm, d), lambda i: (i, 0)),
        out_shape=jax.ShapeDtypeStruct((t, d), F32),
        compiler_params=_cparams("parallel"),
        name="final_norm",
    )(h2d, yt, norm_w.reshape(1, d))


TOKEN_BLOCK = 512
EXPERT_BLOCK = 1024
SAMPLE_T_PAD = 16
SAMPLE_K_PAD = 256


def _ffn(x2d, attn, oh, w, tb):
    h, hnt = _out_proj(x2d, attn, oh, w['wo'], w['norm_ffn'], tb)
    sc = _peer_scores(hnt, w['wqt'], w['keys'], tb)
    yt = _peer_dense(sc, hnt, w['u'], w['vt'], tb, EXPERT_BLOCK)
    return _final(h, yt, w['norm_final'], tb)


def kernel(x_prompt, x_sample, cache_k_win, cache_v_win, state_hgrn, norm_mix_w, w_in, attn_sinks,
           rel_bias_table, hg_lb, hg_norm_w, w_o, norm_ffn_w, peer_w_q, peer_sub_keys, peer_u, peer_v,
           norm_final_w):
    bsz, seq, d = x_prompt.shape
    dbsz, dseq, _ = x_sample.shape
    aw = ATTN_HEADS * HEAD_DIM
    kw = ATTN_KV_HEADS * HEAD_DIM
    hw = hg_norm_w.shape[1]
    wb = cache_k_win.shape[2]

    wi = w_in[0]
    w_in_r = jnp.concatenate([wi[:, :aw], wi[:, aw + 2 * kw:], wi[:, aw:aw + 2 * kw]], axis=1).astype(BF16)
    col_k = (aw + 4 * hw) // kw
    col_v = col_k + 1
    hg_cols = (1, 2, 3, 4)
    lb = jax.nn.softmax(hg_lb.astype(F32), axis=0)[0]
    nhc = PEER_HEADS * 2
    w = {
        'wo': w_o[0].astype(BF16),
        'norm_ffn': norm_ffn_w[0],
        'wqt': peer_w_q[0].T.astype(BF16),
        'keys': peer_sub_keys[0].reshape(nhc, peer_sub_keys.shape[3], peer_sub_keys.shape[4]).astype(BF16),
        'u': peer_u[0].astype(BF16),
        'vt': peer_v[0].T.astype(BF16),
        'norm_final': norm_final_w,
    }
    sinks = attn_sinks[0].astype(F32)

    proj_p = _in_proj(x_prompt.reshape(bsz * seq, d), norm_mix_w[0], w_in_r, TOKEN_BLOCK)
    proj_p3 = proj_p.reshape(bsz, seq, -1)
    blk = WINDOW
    dist_p = (jnp.arange(blk)[:, None] + blk) - jnp.arange(2 * blk)[None, :]
    bias_p = _masked_bias(dist_p, (dist_p >= 0) & (dist_p <= WINDOW), rel_bias_table)
    attn_p = _swa_prompt(proj_p3, sinks, bias_p, 0, col_k, col_v)
    oh_p, st_p = _hgrn(proj_p3, lb, hg_norm_w[0], None, hg_cols, HG_CHUNK, HG_CHUNK, bsz)
    y_p = _ffn(x_prompt.reshape(bsz * seq, d), attn_p.reshape(bsz * seq, aw),
               oh_p.reshape(bsz * seq, hw), w, TOKEN_BLOCK)
    k_off = aw + 4 * hw
    wp = min(WINDOW, seq)
    k_win_p = proj_p3[:, seq - wp:, k_off:k_off + kw].reshape(1, bsz, wp, ATTN_KV_HEADS, HEAD_DIM)
    v_win_p = proj_p3[:, seq - wp:, k_off + kw:k_off + 2 * kw].reshape(1, bsz, wp, ATTN_KV_HEADS, HEAD_DIM)

    tp = SAMPLE_T_PAD
    xs_pad = jnp.pad(x_sample, ((0, 0), (0, tp - dseq), (0, 0)))
    proj_s3 = _in_proj(xs_pad.reshape(dbsz * tp, d), norm_mix_w[0], w_in_r, TOKEN_BLOCK).reshape(dbsz, tp, -1)
    k_new = proj_s3[:, :dseq, k_off:k_off + kw]
    v_new = proj_s3[:, :dseq, k_off + kw:k_off + 2 * kw]
    kk = jnp.concatenate([cache_k_win[0].reshape(dbsz, wb, kw), k_new], axis=1)
    vv = jnp.concatenate([cache_v_win[0].reshape(dbsz, wb, kw), v_new], axis=1)
    kpad = SAMPLE_K_PAD - (wb + dseq)
    kk_pad = jnp.pad(kk, ((0, 0), (0, kpad), (0, 0)))
    vv_pad = jnp.pad(vv, ((0, 0), (0, kpad), (0, 0)))
    dist_s = (wb + jnp.arange(tp))[:, None] - jnp.arange(SAMPLE_K_PAD)[None, :]
    mask_s = ((dist_s >= 0) & (dist_s <= WINDOW) & (jnp.arange(tp)[:, None] < dseq)
              & (jnp.arange(SAMPLE_K_PAD)[None, :] < wb + dseq))
    bias_s = _masked_bias(dist_s, mask_s, rel_bias_table)
    attn_s = _swa_sample(proj_s3, kk_pad, vv_pad, sinks, bias_s, 0, 16)
    oh_s, st_s = _hgrn(proj_s3, lb, hg_norm_w[0], state_hgrn[0], hg_cols, tp, dseq, 8)
    y_s = _ffn(x_sample.reshape(dbsz * dseq, d), attn_s[:, :dseq].reshape(dbsz * dseq, aw),
               oh_s[:, :dseq].reshape(dbsz * dseq, hw), w, TOKEN_BLOCK)
    k_win_s = kk[:, dseq:].reshape(1, dbsz, wb, ATTN_KV_HEADS, HEAD_DIM)
    v_win_s = vv[:, dseq:].reshape(1, dbsz, wb, ATTN_KV_HEADS, HEAD_DIM)

    return (y_p.reshape(bsz, seq, d), y_s.reshape(dbsz, dseq, d), k_win_p, v_win_p, st_p[None],
            k_win_s, v_win_s, st_s[None])
```

```python
import functools
import math

import jax
import jax.numpy as jnp
from jax import lax
from jax.experimental import pallas as pl
from jax.experimental.pallas import tpu as pltpu

F32 = jnp.float32
BF16 = jnp.bfloat16

EPS = 1e-6
NEG = -1e30

ATTN_HEADS = 8
ATTN_KV_HEADS = 2
HEAD_DIM = 64
WINDOW = 128
REL_BUCKETS = 32
HG_HEADS = 4
HG_CHUNK = 64
PEER_HEADS = 8
PEER_TOPK = 16

LANES = 128
SUBLANES = 8
VMEM_LIMIT = 56 * 1024 * 1024


def _cparams(*sem):
    return pltpu.CompilerParams(dimension_semantics=sem, vmem_limit_bytes=VMEM_LIMIT)


def _nt(a, b):
    return lax.dot_general(a, b, (((1,), (1,)), ((), ())), preferred_element_type=F32)


def _tn(a, b):
    return lax.dot_general(a, b, (((0,), (0,)), ((), ())), preferred_element_type=F32)


def _dot(a, b):
    return jnp.dot(a, b, preferred_element_type=F32)


def _sigmoid(x):
    return 1.0 / (1.0 + jnp.exp(-x))


def _in_proj_kernel(x_ref, nw_ref, w_ref, o_ref):
    x = x_ref[...]
    xn = x * lax.rsqrt(jnp.mean(x * x, axis=-1, keepdims=True) + EPS) * nw_ref[...]
    o_ref[...] = _dot(xn.astype(BF16), w_ref[...])


def _in_proj(x2d, norm_w, w_bf16, tm):
    t, d = x2d.shape
    n = w_bf16.shape[1]
    return pl.pallas_call(
        _in_proj_kernel,
        grid=(t // tm,),
        in_specs=[pl.BlockSpec((tm, d), lambda i: (i, 0)),
                  pl.BlockSpec((1, d), lambda i: (0, 0)),
                  pl.BlockSpec((d, n), lambda i: (0, 0))],
        out_specs=pl.BlockSpec((tm, n), lambda i: (i, 0)),
        out_shape=jax.ShapeDtypeStruct((t, n), F32),
        compiler_params=_cparams("parallel"),
        name="in_proj",
    )(x2d, norm_w.reshape(1, d), w_bf16)


def _t5_bucket(dist):
    n = jnp.maximum(dist, 0)
    max_exact = REL_BUCKETS // 2
    nf = jnp.maximum(n, 1).astype(F32)
    large = max_exact + (jnp.log(nf / max_exact) / math.log(WINDOW / max_exact)
                         * (REL_BUCKETS - max_exact)).astype(jnp.int32)
    large = jnp.minimum(large, REL_BUCKETS - 1)
    return jnp.where(n < max_exact, n, large)


def _masked_bias(dist, mask, table):
    b = jnp.transpose(table.astype(F32)[_t5_bucket(dist)], (2, 0, 1))
    return jnp.where(mask[None], b, NEG)


def _swa_prompt_kernel(sink_ref, q_ref, kp_ref, kc_ref, vp_ref, vc_ref, bias_ref, o_ref):
    first = pl.program_id(1) == 0
    scale = HEAD_DIM ** -0.5
    group = ATTN_HEADS // ATTN_KV_HEADS
    blk = q_ref.shape[1]
    for h in range(ATTN_HEADS):
        kv = (h // group) * HEAD_DIM
        qh = q_ref[0, :, h * HEAD_DIM:(h + 1) * HEAD_DIM].astype(BF16)
        kp = kp_ref[0, :, kv:kv + HEAD_DIM].astype(BF16)
        kc = kc_ref[0, :, kv:kv + HEAD_DIM].astype(BF16)
        vp = vp_ref[0, :, kv:kv + HEAD_DIM].astype(BF16)
        vc = vc_ref[0, :, kv:kv + HEAD_DIM].astype(BF16)
        sp = _nt(qh, kp) * scale + bias_ref[h, :, :blk]
        sc = _nt(qh, kc) * scale + bias_ref[h, :, blk:]
        sp = jnp.where(first, NEG, sp)
        sink = sink_ref[h]
        m = jnp.maximum(jnp.maximum(jnp.max(sp, axis=-1, keepdims=True),
                                    jnp.max(sc, axis=-1, keepdims=True)), sink)
        pp = jnp.exp(sp - m)
        pc = jnp.exp(sc - m)
        den = (jnp.sum(pp, axis=-1, keepdims=True) + jnp.sum(pc, axis=-1, keepdims=True)
               + jnp.exp(sink - m))
        o = (_dot(pp.astype(BF16), vp) + _dot(pc.astype(BF16), vc)) / den
        o_ref[0, :, h * HEAD_DIM:(h + 1) * HEAD_DIM] = o.astype(o_ref.dtype)


def _swa_prompt(proj3, sinks, bias, col_q, col_k, col_v):
    bsz, seq, _ = proj3.shape
    blk = WINDOW
    aw = ATTN_HEADS * HEAD_DIM
    kw = ATTN_KV_HEADS * HEAD_DIM
    prev = lambda b, n: (b, jnp.maximum(n - 1, 0))
    return pl.pallas_call(
        _swa_prompt_kernel,
        grid=(bsz, seq // blk),
        in_specs=[pl.BlockSpec(memory_space=pltpu.SMEM),
                  pl.BlockSpec((1, blk, aw), lambda b, n: (b, n, col_q)),
                  pl.BlockSpec((1, blk, kw), lambda b, n: prev(b, n) + (col_k,)),
                  pl.BlockSpec((1, blk, kw), lambda b, n: (b, n, col_k)),
                  pl.BlockSpec((1, blk, kw), lambda b, n: prev(b, n) + (col_v,)),
                  pl.BlockSpec((1, blk, kw), lambda b, n: (b, n, col_v)),
                  pl.BlockSpec((ATTN_HEADS, blk, 2 * blk), lambda b, n: (0, 0, 0))],
        out_specs=pl.BlockSpec((1, blk, aw), lambda b, n: (b, n, 0)),
        out_shape=jax.ShapeDtypeStruct((bsz, seq, aw), BF16),
        compiler_params=_cparams("parallel", "arbitrary"),
        name="swa_prompt",
    )(sinks, proj3, proj3, proj3, proj3, proj3, bias)


def _swa_sample_kernel(sink_ref, q_ref, kk_ref, vv_ref, bias_ref, o_ref):
    scale = HEAD_DIM ** -0.5
    group = ATTN_HEADS // ATTN_KV_HEADS
    for h in range(ATTN_HEADS):
        kv = (h // group) * HEAD_DIM
        qh = q_ref[:, :, h * HEAD_DIM:(h + 1) * HEAD_DIM].astype(BF16)
        kh = kk_ref[:, :, kv:kv + HEAD_DIM].astype(BF16)
        vh = vv_ref[:, :, kv:kv + HEAD_DIM].astype(BF16)
        s = jnp.einsum('bqd,bkd->bqk', qh, kh, preferred_element_type=F32) * scale + bias_ref[h][None]
        sink = sink_ref[h]
        m = jnp.maximum(jnp.max(s, axis=-1, keepdims=True), sink)
        p = jnp.exp(s - m)
        den = jnp.sum(p, axis=-1, keepdims=True) + jnp.exp(sink - m)
        o = jnp.einsum('bqk,bkd->bqd', p.astype(BF16), vh, preferred_element_type=F32) / den
        o_ref[:, :, h * HEAD_DIM:(h + 1) * HEAD_DIM] = o.astype(o_ref.dtype)


def _swa_sample(proj3, kk, vv, sinks, bias, col_q, bb):
    bsz, tp, _ = proj3.shape
    kp = kk.shape[1]
    aw = ATTN_HEADS * HEAD_DIM
    kw = ATTN_KV_HEADS * HEAD_DIM
    return pl.pallas_call(
        _swa_sample_kernel,
        grid=(bsz // bb,),
        in_specs=[pl.BlockSpec(memory_space=pltpu.SMEM),
                  pl.BlockSpec((bb, tp, aw), lambda b: (b, 0, col_q)),
                  pl.BlockSpec((bb, kp, kw), lambda b: (b, 0, 0)),
                  pl.BlockSpec((bb, kp, kw), lambda b: (b, 0, 0)),
                  pl.BlockSpec((ATTN_HEADS, tp, kp), lambda b: (0, 0, 0))],
        out_specs=pl.BlockSpec((bb, tp, aw), lambda b: (b, 0, 0)),
        out_shape=jax.ShapeDtypeStruct((bsz, tp, aw), BF16),
        compiler_params=_cparams("parallel"),
        name="swa_sample",
    )(sinks, proj3, kk, vv, bias)


def _split3(x):
    hi = x.astype(BF16)
    r = x - hi.astype(F32)
    mid = r.astype(BF16)
    lo = (r - mid.astype(F32)).astype(BF16)
    return hi, mid, lo


def _hgrn_kernel(*refs, t_valid, has_state):
    if has_state:
        q_ref, f_ref, i_ref, g_ref, lb_ref, nw_ref, s0_ref, o_ref, s_ref, st_scr = refs
    else:
        q_ref, f_ref, i_ref, g_ref, lb_ref, nw_ref, o_ref, s_ref, st_scr = refs
    bb, chunk, width = q_ref.shape
    dk = width // HG_HEADS
    c = pl.program_id(1)

    @pl.when(c == 0)
    def _init():
        if has_state:
            def load(b, carry):
                for h in range(HG_HEADS):
                    st_scr[b, h] = s0_ref[b, h].T
                return carry
            lax.fori_loop(0, bb, load, 0)
        else:
            st_scr[...] = jnp.zeros_like(st_scr)

    row = lax.broadcasted_iota(jnp.int32, (chunk, chunk), 0)
    col = lax.broadcasted_iota(jnp.int32, (chunk, chunk), 1)
    causal = row >= col
    tri = jnp.where(causal, 1.0, 0.0).astype(BF16)
    valid = lax.broadcasted_iota(jnp.int32, (chunk, dk), 0) < t_valid
    mid_row = chunk // 2

    def body(b, carry):
        for h in range(HG_HEADS):
            sl = slice(h * dk, (h + 1) * dk)
            qx = q_ref[b, :, sl]
            q = qx * _sigmoid(qx)
            lb = lb_ref[:, sl]
            f = lb + (1.0 - lb) * _sigmoid(f_ref[b, :, sl])
            k = 1.0 - f
            lg = jnp.log(f)
            if t_valid < chunk:
                k = jnp.where(valid, k, 0.0)
                lg = jnp.where(valid, lg, 0.0)
            v = i_ref[b, :, sl].astype(BF16)
            cum = sum(_dot(tri, part) for part in _split3(lg))
            cum_mid = cum[mid_row:mid_row + 1, :]
            cum_last = cum[chunk - 1:chunk, :]
            qt = (q * jnp.exp(cum - cum_mid)).astype(BF16)
            kt = (k * jnp.exp(cum_mid - cum)).astype(BF16)
            a = jnp.where(causal, _nt(qt, kt), 0.0)
            st = st_scr[b, h]
            o = _dot(a.astype(BF16), v) + _nt((q * jnp.exp(cum)).astype(BF16), st.astype(BF16))
            kd = (k * jnp.exp(cum_last - cum)).astype(BF16)
            st_scr[b, h] = st * jnp.exp(cum_last) + _tn(v, kd)
            o = o * lax.rsqrt(jnp.mean(o * o, axis=-1, keepdims=True) + EPS)
            gx = g_ref[b, :, sl]
            o = o * nw_ref[:, sl] * (gx * _sigmoid(gx))
            o_ref[b, :, sl] = o.astype(o_ref.dtype)
        return carry

    lax.fori_loop(0, bb, body, 0)

    @pl.when(c == pl.num_programs(1) - 1)
    def _final():
        def store(b, carry):
            for h in range(HG_HEADS):
                s_ref[b, h] = st_scr[b, h].T
            return carry
        lax.fori_loop(0, bb, store, 0)


def _hgrn(proj3, lb, norm_w, s0, cols, chunk, t_valid, bb):
    bsz, t, _ = proj3.shape
    width = lb.shape[0]
    dk = width // HG_HEADS
    has_state = s0 is not None
    spec = lambda cb: pl.BlockSpec((bb, chunk, width), lambda b, c: (b, c, cb))
    vec = pl.BlockSpec((1, width), lambda b, c: (0, 0))
    st_spec = pl.BlockSpec((bb, HG_HEADS, dk, dk), lambda b, c: (b, 0, 0, 0))
    in_specs = [spec(cols[0]), spec(cols[1]), spec(cols[2]), spec(cols[3]), vec, vec]
    args = [proj3, proj3, proj3, proj3, lb.reshape(1, width), norm_w.reshape(1, width)]
    if has_state:
        in_specs.append(st_spec)
        args.append(s0)
    return pl.pallas_call(
        functools.partial(_hgrn_kernel, t_valid=t_valid, has_state=has_state),
        grid=(bsz // bb, t // chunk),
        in_specs=in_specs,
        out_specs=[pl.BlockSpec((bb, chunk, width), lambda b, c: (b, c, 0)), st_spec],
        out_shape=[jax.ShapeDtypeStruct((bsz, t, width), BF16),
                   jax.ShapeDtypeStruct((bsz, HG_HEADS, dk, dk), F32)],
        scratch_shapes=[pltpu.VMEM((bb, HG_HEADS, dk, dk), F32)],
        compiler_params=_cparams("parallel", "arbitrary"),
        name="hgrn_state" if has_state else "hgrn_prompt",
    )(*args)


def _out_proj_kernel(x_ref, a_ref, oh_ref, wo_ref, nw_ref, h_ref, hnt_ref):
    aw = a_ref.shape[1]
    mix = _dot(a_ref[...], wo_ref[:aw, :]) + _dot(oh_ref[...], wo_ref[aw:, :])
    h = x_ref[...] + mix
    h_ref[...] = h
    hn = h * lax.rsqrt(jnp.mean(h * h, axis=-1, keepdims=True) + EPS) * nw_ref[...]
    hnt_ref[...] = hn.T.astype(hnt_ref.dtype)


def _out_proj(x2d, attn, oh, wo_bf16, norm_w, tm):
    t, d = x2d.shape
    aw, hw = attn.shape[1], oh.shape[1]
    return pl.pallas_call(
        _out_proj_kernel,
        grid=(t // tm,),
        in_specs=[pl.BlockSpec((tm, d), lambda i: (i, 0)),
                  pl.BlockSpec((tm, aw), lambda i: (i, 0)),
                  pl.BlockSpec((tm, hw), lambda i: (i, 0)),
                  pl.BlockSpec((aw + hw, d), lambda i: (0, 0)),
                  pl.BlockSpec((1, d), lambda i: (0, 0))],
        out_specs=[pl.BlockSpec((tm, d), lambda i: (i, 0)),
                   pl.BlockSpec((d, tm), lambda i: (0, i))],
        out_shape=[jax.ShapeDtypeStruct((t, d), F32),
                   jax.ShapeDtypeStruct((d, t), BF16)],
        compiler_params=_cparams("parallel"),
        name="out_proj",
    )(x2d, attn, oh, wo_bf16, norm_w.reshape(1, d))


def _peer_scores_kernel(hnt_ref, wqt_ref, keys_ref, sc_ref):
    qt = _dot(wqt_ref[...], hnt_ref[...])
    half = keys_ref.shape[2]
    for hc in range(keys_ref.shape[0]):
        s = _dot(keys_ref[hc], qt[hc * half:(hc + 1) * half, :].astype(BF16))
        for lt in range(sc_ref.shape[0]):
            sc_ref[lt, hc] = s[:, lt * LANES:(lt + 1) * LANES]


def _peer_scores(hnt, wqt_bf16, keys_bf16, tb):
    d, t = hnt.shape
    nhc, nk, half = keys_bf16.shape
    return pl.pallas_call(
        _peer_scores_kernel,
        grid=(t // tb,),
        in_specs=[pl.BlockSpec((d, tb), lambda i: (0, i)),
                  pl.BlockSpec((nhc * half, d), lambda i: (0, 0)),
                  pl.BlockSpec((nhc, nk, half), lambda i: (0, 0, 0))],
        out_specs=pl.BlockSpec((tb // LANES, nhc, nk, LANES), lambda i: (i, 0, 0, 0)),
        out_shape=jax.ShapeDtypeStruct((t // LANES, nhc, nk, LANES), F32),
        compiler_params=_cparams("parallel"),
        name="peer_scores",
    )(hnt, wqt_bf16, keys_bf16)


def _sort16_pairs():
    def merge(lo, hi, r):
        step = r * 2
        if step < hi - lo:
            yield from merge(lo, hi, step)
            yield from merge(lo + r, hi, step)
            yield from [(i, i + r) for i in range(lo + r, hi - r, step)]
        else:
            yield (lo, lo + r)

    def sort(lo, hi):
        if hi - lo >= 1:
            mid = lo + (hi - lo) // 2
            yield from sort(lo, mid)
            yield from sort(mid + 1, hi)
            yield from merge(lo, hi, 1)

    return tuple(sort(0, PEER_TOPK - 1))


_SORT16 = _sort16_pairs()


def _bitonic_to_sorted(z):
    z = list(z)
    d = PEER_TOPK // 2
    while d >= 1:
        for i in range(PEER_TOPK):
            if i & d == 0:
                hi, lo = jnp.maximum(z[i], z[i + d]), jnp.minimum(z[i], z[i + d])
                z[i], z[i + d] = hi, lo
        d //= 2
    return z


def _merge_bitonic(top, other):
    z = list(top)
    m = len(other)
    for r in range(PEER_TOPK - m, PEER_TOPK):
        z[r] = jnp.maximum(top[r], other[PEER_TOPK - 1 - r])
    return z


def _top16_desc(x):
    n = x.shape[0] // SUBLANES
    xs = [x[g * SUBLANES:(g + 1) * SUBLANES, :] for g in range(n)]
    for i, j in _SORT16:
        xs[i], xs[j] = jnp.maximum(xs[i], xs[j]), jnp.minimum(xs[i], xs[j])
    shift = SUBLANES // 2
    while shift >= 1:
        ys = [pltpu.roll(v, shift, axis=0) for v in xs]
        xs = _bitonic_to_sorted(_merge_bitonic(xs, ys))
        shift //= 2
    return xs


def _peer_select(sc_ref, thr_scr, pw_scr, q_scr, h, lt):
    s0 = sc_ref[lt, 2 * h]
    s1 = sc_ref[lt, 2 * h + 1]
    a = _top16_desc(s0)
    b = _top16_desc(s1)
    lists = [[a[r] + b[c] for c in range(PEER_TOPK // (r + 1))] for r in range(SUBLANES)]
    lists.append([a[r] + b[0] for r in range(SUBLANES, PEER_TOPK)])
    top = lists[0]
    for other in lists[1:-1]:
        top = _bitonic_to_sorted(_merge_bitonic(top, other))
    z = _merge_bitonic(top, lists[-1])
    tau = functools.reduce(jnp.minimum, z)
    best = a[0] + b[0]
    zsum = jnp.zeros_like(tau)
    for cand in lists:
        for v in cand:
            zsum = zsum + jnp.where(v >= tau, jnp.exp(v - best), 0.0)
    inv = 1.0 / zsum
    n = s0.shape[0] // SUBLANES
    for g in range(n):
        rows = slice(g * SUBLANES, (g + 1) * SUBLANES)
        x0 = s0[rows, :]
        thr = jnp.full_like(x0, jnp.inf)
        for c in range(PEER_TOPK):
            thr = jnp.where(x0 + b[c] >= tau, b[c], thr)
        thr_scr[lt, h, rows, :] = thr
        pw_scr[lt, h, rows, :] = jnp.exp(x0 - a[0]) * inv
        q_scr[lt, h, rows, :] = jnp.exp(s1[rows, :] - b[0])


def _peer_dense_kernel(sc_ref, hnt_ref, u_ref, vt_ref, yt_ref, thr_scr, pw_scr, q_scr, h_scr, g_scr):
    e = pl.program_id(1)
    tb = hnt_ref.shape[1]
    n_lt, _, nk, _ = sc_ref.shape
    n_i = u_ref.shape[0] // nk

    @pl.when(e == 0)
    def _select():
        def body(it, carry):
            _peer_select(sc_ref, thr_scr, pw_scr, q_scr, it // n_lt, it % n_lt)
            return carry
        lax.fori_loop(0, PEER_HEADS * n_lt, body, 0)
        yt_ref[...] = jnp.zeros_like(yt_ref)

    h_scr[:, :tb] = _dot(u_ref[...], hnt_ref[...])

    i0 = pl.multiple_of(e * n_i, SUBLANES)

    def body(lt, carry):
        lanes = pl.ds(pl.multiple_of(lt * LANES, LANES), LANES)
        for ii in range(n_i):
            rows = slice(ii * nk, (ii + 1) * nk)
            w = jnp.zeros((nk, LANES), F32)
            for h in range(PEER_HEADS):
                thr = thr_scr[lt, h, pl.ds(i0, n_i), :][ii:ii + 1, :]
                pw = pw_scr[lt, h, pl.ds(i0, n_i), :][ii:ii + 1, :]
                w = w + jnp.where(sc_ref[lt, 2 * h + 1] >= thr, q_scr[lt, h], 0.0) * pw
            x = h_scr[rows, lanes]
            act = 0.5 * x * (1.0 + lax.erf(x * (2.0 ** -0.5)))
            g_scr[rows, lanes] = (w * act).astype(g_scr.dtype)
        return carry

    lax.fori_loop(0, n_lt, body, 0)
    yt_ref[...] += _dot(vt_ref[...], g_scr[...])


def _peer_dense(sc, hnt, u_bf16, vt_bf16, tb, eb):
    _, nhc, nk, _ = sc.shape
    d, t = hnt.shape
    n_exp = u_bf16.shape[0]
    n_lt = tb // LANES
    assert eb == SUBLANES * nk and n_exp == nk * nk and t % tb == 0 and tb % LANES == 0
    sel = pltpu.VMEM((n_lt, PEER_HEADS, nk, LANES), F32)
    return pl.pallas_call(
        _peer_dense_kernel,
        grid=(t // tb, n_exp // eb),
        in_specs=[pl.BlockSpec((n_lt, nhc, nk, LANES), lambda i, e: (i, 0, 0, 0)),
                  pl.BlockSpec((d, tb), lambda i, e: (0, i)),
                  pl.BlockSpec((eb, d), lambda i, e: (e, 0)),
                  pl.BlockSpec((d, eb), lambda i, e: (0, e))],
        out_specs=pl.BlockSpec((d, tb), lambda i, e: (0, i)),
        out_shape=jax.ShapeDtypeStruct((d, t), F32),
        scratch_shapes=[sel, sel, sel,
                        pltpu.VMEM((eb, tb + LANES), F32),
                        pltpu.VMEM((eb, tb), BF16)],
        compiler_params=_cparams("parallel", "arbitrary"),
        name="peer_dense",
    )(sc, hnt, u_bf16, vt_bf16)


def _final_kernel(h_ref, yt_ref, nw_ref, o_ref):
    y = h_ref[...] + yt_ref[...].T
    o_ref[...] = y * lax.rsqrt(jnp.mean(y * y, axis=-1, keepdims=True) + EPS) * nw_ref[...]


def _final(h2d, yt, norm_w, tm):
    t, d = h2d.shape
    return pl.pallas_call(
        _final_kernel,
        grid=(t // tm,),
        in_specs=[pl.BlockSpec((tm, d), lambda i: (i, 0)),
                  pl.BlockSpec((d, tm), lambda i: (0, i)),
                  pl.BlockSpec((1, d), lambda i: (0, 0))],
        out_specs=pl.BlockSpec((tm, d), lambda i: (i, 0)),
        out_shape=jax.ShapeDtypeStruct((t, d), F32),
        compiler_params=_cparams("parallel"),
        name="final_norm",
    )(h2d, yt, norm_w.reshape(1, d))


TOKEN_BLOCK = 512
EXPERT_BLOCK = 1024
SAMPLE_T_PAD = 16
SAMPLE_K_PAD = 256


def _ffn(x2d, attn, oh, w, tb):
    h, hnt = _out_proj(x2d, attn, oh, w['wo'], w['norm_ffn'], tb)
    sc = _peer_scores(hnt, w['wqt'], w['keys'], tb)
    yt = _peer_dense(sc, hnt, w['u'], w['vt'], tb, EXPERT_BLOCK)
    return _final(h, yt, w['norm_final'], tb)


def kernel(x_prompt, x_sample, cache_k_win, cache_v_win, state_hgrn, norm_mix_w, w_in, attn_sinks,
           rel_bias_table, hg_lb, hg_norm_w, w_o, norm_ffn_w, peer_w_q, peer_sub_keys, peer_u, peer_v,
           norm_final_w):
    bsz, seq, d = x_prompt.shape
    dbsz, dseq, _ = x_sample.shape
    aw = ATTN_HEADS * HEAD_DIM
    kw = ATTN_KV_HEADS * HEAD_DIM
    hw = hg_norm_w.shape[1]
    wb = cache_k_win.shape[2]

    wi = w_in[0]
    w_in_r = jnp.concatenate([wi[:, :aw], wi[:, aw + 2 * kw:], wi[:, aw:aw + 2 * kw]], axis=1).astype(BF16)
    col_k = (aw + 4 * hw) // kw
    col_v = col_k + 1
    hg_cols = (1, 2, 3, 4)
    lb = jax.nn.softmax(hg_lb.astype(F32), axis=0)[0]
    nhc = PEER_HEADS * 2
    w = {
        'wo': w_o[0].astype(BF16),
        'norm_ffn': norm_ffn_w[0],
        'wqt': peer_w_q[0].T.astype(BF16),
        'keys': peer_sub_keys[0].reshape(nhc, peer_sub_keys.shape[3], peer_sub_keys.shape[4]).astype(BF16),
        'u': peer_u[0].astype(BF16),
        'vt': peer_v[0].T.astype(BF16),
        'norm_final': norm_final_w,
    }
    sinks = attn_sinks[0].astype(F32)

    proj_p = _in_proj(x_prompt.reshape(bsz * seq, d), norm_mix_w[0], w_in_r, TOKEN_BLOCK)
    proj_p3 = proj_p.reshape(bsz, seq, -1)
    blk = WINDOW
    dist_p = (jnp.arange(blk)[:, None] + blk) - jnp.arange(2 * blk)[None, :]
    bias_p = _masked_bias(dist_p, (dist_p >= 0) & (dist_p <= WINDOW), rel_bias_table)
    attn_p = _swa_prompt(proj_p3, sinks, bias_p, 0, col_k, col_v)
    oh_p, st_p = _hgrn(proj_p3, lb, hg_norm_w[0], None, hg_cols, HG_CHUNK, HG_CHUNK, bsz)
    y_p = _ffn(x_prompt.reshape(bsz * seq, d), attn_p.reshape(bsz * seq, aw),
               oh_p.reshape(bsz * seq, hw), w, TOKEN_BLOCK)
    k_off = aw + 4 * hw
    wp = min(WINDOW, seq)
    k_win_p = proj_p3[:, seq - wp:, k_off:k_off + kw].reshape(1, bsz, wp, ATTN_KV_HEADS, HEAD_DIM)
    v_win_p = proj_p3[:, seq - wp:, k_off + kw:k_off + 2 * kw].reshape(1, bsz, wp, ATTN_KV_HEADS, HEAD_DIM)

    tp = SAMPLE_T_PAD
    xs_pad = jnp.pad(x_sample, ((0, 0), (0, tp - dseq), (0, 0)))
    proj_s3 = _in_proj(xs_pad.reshape(dbsz * tp, d), norm_mix_w[0], w_in_r, TOKEN_BLOCK).reshape(dbsz, tp, -1)
    k_new = proj_s3[:, :dseq, k_off:k_off + kw]
    v_new = proj_s3[:, :dseq, k_off + kw:k_off + 2 * kw]
    kk = jnp.concatenate([cache_k_win[0].reshape(dbsz, wb, kw), k_new], axis=1)
    vv = jnp.concatenate([cache_v_win[0].reshape(dbsz, wb, kw), v_new], axis=1)
    kpad = SAMPLE_K_PAD - (wb + dseq)
    kk_pad = jnp.pad(kk, ((0, 0), (0, kpad), (0, 0)))
    vv_pad = jnp.pad(vv, ((0, 0), (0, kpad), (0, 0)))
    dist_s = (wb + jnp.arange(tp))[:, None] - jnp.arange(SAMPLE_K_PAD)[None, :]
    mask_s = ((dist_s >= 0) & (dist_s <= WINDOW) & (jnp.arange(tp)[:, None] < dseq)
              & (jnp.arange(SAMPLE_K_PAD)[None, :] < wb + dseq))
    bias_s = _masked_bias(dist_s, mask_s, rel_bias_table)
    attn_s = _swa_sample(proj_s3, kk_pad, vv_pad, sinks, bias_s, 0, 16)
    oh_s, st_s = _hgrn(proj_s3, lb, hg_norm_w[0], state_hgrn[0], hg_cols, tp, dseq, 8)
    y_s = _ffn(x_sample.reshape(dbsz * dseq, d), attn_s[:, :dseq].reshape(dbsz * dseq, aw),
               oh_s[:, :dseq].reshape(dbsz * dseq, hw), w, TOKEN_BLOCK)
    k_win_s = kk[:, dseq:].reshape(1, dbsz, wb, ATTN_KV_HEADS, HEAD_DIM)
    v_win_s = vv[:, dseq:].reshape(1, dbsz, wb, ATTN_KV_HEADS, HEAD_DIM)

    return (y_p.reshape(bsz, seq, d), y_s.reshape(dbsz, dseq, d), k_win_p, v_win_p, st_p[None],
            k_win_s, v_win_s, st_s[None])
```

```python
import functools
import math

import jax
import jax.numpy as jnp
from jax import lax
from jax.experimental import pallas as pl
from jax.experimental.pallas import tpu as pltpu

F32 = jnp.float32
BF16 = jnp.bfloat16

EPS = 1e-6
NEG = -1e30

ATTN_HEADS = 8
ATTN_KV_HEADS = 2
HEAD_DIM = 64
WINDOW = 128
REL_BUCKETS = 32
HG_HEADS = 4
HG_CHUNK = 64
PEER_HEADS = 8
PEER_TOPK = 16

LANES = 128
SUBLANES = 8
VMEM_LIMIT = 56 * 1024 * 1024


def _cparams(*sem):
    return pltpu.CompilerParams(dimension_semantics=sem, vmem_limit_bytes=VMEM_LIMIT)


def _nt(a, b):
    return lax.dot_general(a, b, (((1,), (1,)), ((), ())), preferred_element_type=F32)


def _tn(a, b):
    return lax.dot_general(a, b, (((0,), (0,)), ((), ())), preferred_element_type=F32)


def _dot(a, b):
    return jnp.dot(a, b, preferred_element_type=F32)


def _sigmoid(x):
    return 1.0 / (1.0 + jnp.exp(-x))


def _in_proj_kernel(x_ref, nw_ref, w_ref, o_ref):
    x = x_ref[...]
    xn = x * lax.rsqrt(jnp.mean(x * x, axis=-1, keepdims=True) + EPS) * nw_ref[...]
    o_ref[...] = _dot(xn.astype(BF16), w_ref[...])


def _in_proj(x2d, norm_w, w_bf16, tm):
    t, d = x2d.shape
    n = w_bf16.shape[1]
    return pl.pallas_call(
        _in_proj_kernel,
        grid=(t // tm,),
        in_specs=[pl.BlockSpec((tm, d), lambda i: (i, 0)),
                  pl.BlockSpec((1, d), lambda i: (0, 0)),
                  pl.BlockSpec((d, n), lambda i: (0, 0))],
        out_specs=pl.BlockSpec((tm, n), lambda i: (i, 0)),
        out_shape=jax.ShapeDtypeStruct((t, n), F32),
        compiler_params=_cparams("parallel"),
        name="in_proj",
    )(x2d, norm_w.reshape(1, d), w_bf16)


def _t5_bucket(dist):
    n = jnp.maximum(dist, 0)
    max_exact = REL_BUCKETS // 2
    nf = jnp.maximum(n, 1).astype(F32)
    large = max_exact + (jnp.log(nf / max_exact) / math.log(WINDOW / max_exact)
                         * (REL_BUCKETS - max_exact)).astype(jnp.int32)
    large = jnp.minimum(large, REL_BUCKETS - 1)
    return jnp.where(n < max_exact, n, large)


def _masked_bias(table, n_q, n_k, offset, mask):
    h = table.shape[1]
    diag = jnp.arange(n_q + n_k - 1) - (n_k - 1) + offset
    per_diag = table.astype(F32)[_t5_bucket(diag)].T
    w = jnp.pad(per_diag[:, ::-1], ((0, 0), (0, 1)))
    p = n_q + n_k
    skew = jnp.tile(w, (1, n_q))[:, :n_q * (p - 1)].reshape(h, n_q, p - 1)
    return jnp.where(mask[None], skew[:, :, n_q - 1:n_q - 1 + n_k], NEG)


def _swa_prompt_kernel(sink_ref, q_ref, kp_ref, kc_ref, vp_ref, vc_ref, bias_ref, o_ref):
    first = pl.program_id(1) == 0
    scale = HEAD_DIM ** -0.5
    group = ATTN_HEADS // ATTN_KV_HEADS
    blk = q_ref.shape[1]
    assert 2 * HEAD_DIM == LANES and kp_ref.shape[2] == LANES and ATTN_KV_HEADS == 2 and group % 2 == 0
    kk = jnp.concatenate([kp_ref[0], kc_ref[0]], axis=0)
    vv = jnp.concatenate([vp_ref[0], vc_ref[0]], axis=0)
    low = lax.broadcasted_iota(jnp.int32, kk.shape, 1) < HEAD_DIM
    col = lax.broadcasted_iota(jnp.int32, (blk, 2 * blk), 1)
    no_prev = (col < blk) & first
    def halves(x, kvh):
        own = jnp.where(low if kvh == 0 else ~low, x, 0.0)
        other = pltpu.roll(own, HEAD_DIM, axis=1)
        lo, hi = (own, other) if kvh == 0 else (other, own)
        return lo.astype(BF16), hi.astype(BF16)

    k_half = [halves(kk, kvh) for kvh in range(ATTN_KV_HEADS)]
    v_half = [halves(vv, kvh) for kvh in range(ATTN_KV_HEADS)]
    heads = range(ATTN_HEADS)
    kv_of = lambda h: h // group
    s = [_nt(q_ref[0, :, (h // 2) * LANES:(h // 2 + 1) * LANES].astype(BF16), k_half[kv_of(h)][h % 2])
         for h in heads]
    s = [jnp.where(no_prev, NEG, s[h] * scale + bias_ref[h]) for h in heads]
    m = [jnp.maximum(jnp.max(s[h], axis=-1, keepdims=True), sink_ref[h]) for h in heads]
    p = [jnp.exp(s[h] - m[h]) for h in heads]
    den = [jnp.sum(p[h], axis=-1, keepdims=True) + jnp.exp(sink_ref[h] - m[h]) for h in heads]
    p = [(p[h] * (1.0 / den[h])).astype(BF16) for h in heads]
    for tile in range(ATTN_HEADS // 2):
        kvh = kv_of(2 * tile)
        o = _dot(p[2 * tile], v_half[kvh][0]) + _dot(p[2 * tile + 1], v_half[kvh][1])
        o_ref[0, :, tile * LANES:(tile + 1) * LANES] = o.astype(o_ref.dtype)


def _swa_prompt(proj3, sinks, bias, col_q, col_k, col_v):
    bsz, seq, _ = proj3.shape
    blk = WINDOW
    aw = ATTN_HEADS * HEAD_DIM
    kw = ATTN_KV_HEADS * HEAD_DIM
    prev = lambda b, n: (b, jnp.maximum(n - 1, 0))
    return pl.pallas_call(
        _swa_prompt_kernel,
        grid=(bsz, seq // blk),
        in_specs=[pl.BlockSpec(memory_space=pltpu.SMEM),
                  pl.BlockSpec((1, blk, aw), lambda b, n: (b, n, col_q)),
                  pl.BlockSpec((1, blk, kw), lambda b, n: prev(b, n) + (col_k,)),
                  pl.BlockSpec((1, blk, kw), lambda b, n: (b, n, col_k)),
                  pl.BlockSpec((1, blk, kw), lambda b, n: prev(b, n) + (col_v,)),
                  pl.BlockSpec((1, blk, kw), lambda b, n: (b, n, col_v)),
                  pl.BlockSpec((ATTN_HEADS, blk, 2 * blk), lambda b, n: (0, 0, 0))],
        out_specs=pl.BlockSpec((1, blk, aw), lambda b, n: (b, n, 0)),
        out_shape=jax.ShapeDtypeStruct((bsz, seq, aw), BF16),
        compiler_params=_cparams("parallel", "arbitrary"),
        name="swa_prompt",
    )(sinks, proj3, proj3, proj3, proj3, proj3, bias)


def _swa_sample_kernel(sink_ref, q_ref, kk_ref, vv_ref, bias_ref, o_ref):
    scale = HEAD_DIM ** -0.5
    group = ATTN_HEADS // ATTN_KV_HEADS
    for h in range(ATTN_HEADS):
        kv = (h // group) * HEAD_DIM
        qh = q_ref[:, :, h * HEAD_DIM:(h + 1) * HEAD_DIM].astype(BF16)
        kh = kk_ref[:, :, kv:kv + HEAD_DIM].astype(BF16)
        vh = vv_ref[:, :, kv:kv + HEAD_DIM].astype(BF16)
        s = jnp.einsum('bqd,bkd->bqk', qh, kh, preferred_element_type=F32) * scale + bias_ref[h][None]
        sink = sink_ref[h]
        m = jnp.maximum(jnp.max(s, axis=-1, keepdims=True), sink)
        p = jnp.exp(s - m)
        den = jnp.sum(p, axis=-1, keepdims=True) + jnp.exp(sink - m)
        o = jnp.einsum('bqk,bkd->bqd', p.astype(BF16), vh, preferred_element_type=F32) / den
        o_ref[:, :, h * HEAD_DIM:(h + 1) * HEAD_DIM] = o.astype(o_ref.dtype)


def _swa_sample(proj3, kk, vv, sinks, bias, col_q, bb):
    bsz, tp, _ = proj3.shape
    kp = kk.shape[1]
    aw = ATTN_HEADS * HEAD_DIM
    kw = ATTN_KV_HEADS * HEAD_DIM
    return pl.pallas_call(
        _swa_sample_kernel,
        grid=(bsz // bb,),
        in_specs=[pl.BlockSpec(memory_space=pltpu.SMEM),
                  pl.BlockSpec((bb, tp, aw), lambda b: (b, 0, col_q)),
                  pl.BlockSpec((bb, kp, kw), lambda b: (b, 0, 0)),
                  pl.BlockSpec((bb, kp, kw), lambda b: (b, 0, 0)),
                  pl.BlockSpec((ATTN_HEADS, tp, kp), lambda b: (0, 0, 0))],
        out_specs=pl.BlockSpec((bb, tp, aw), lambda b: (b, 0, 0)),
        out_shape=jax.ShapeDtypeStruct((bsz, tp, aw), BF16),
        compiler_params=_cparams("parallel"),
        name="swa_sample",
    )(sinks, proj3, kk, vv, bias)


def _split3(x):
    hi = x.astype(BF16)
    r = x - hi.astype(F32)
    mid = r.astype(BF16)
    lo = (r - mid.astype(F32)).astype(BF16)
    return hi, mid, lo


def _hgrn_kernel(*refs, t_valid, has_state):
    if has_state:
        q_ref, f_ref, i_ref, g_ref, lb_ref, nw_ref, s0_ref, o_ref, s_ref, st_scr = refs
    else:
        q_ref, f_ref, i_ref, g_ref, lb_ref, nw_ref, o_ref, s_ref, st_scr = refs
    bb, chunk, width = q_ref.shape
    dk = width // HG_HEADS
    c = pl.program_id(1)

    @pl.when(c == 0)
    def _init():
        if has_state:
            def load(b, carry):
                for h in range(HG_HEADS):
                    st_scr[b, h] = s0_ref[b, h].T
                return carry
            lax.fori_loop(0, bb, load, 0)
        else:
            st_scr[...] = jnp.zeros_like(st_scr)

    row = lax.broadcasted_iota(jnp.int32, (chunk, chunk), 0)
    col = lax.broadcasted_iota(jnp.int32, (chunk, chunk), 1)
    causal = row >= col
    tri = jnp.where(causal, 1.0, 0.0).astype(BF16)
    valid = lax.broadcasted_iota(jnp.int32, (chunk, width), 0) < t_valid
    mid_row = chunk // 2

    def body(b, carry):
        qx = q_ref[b]
        q = qx * _sigmoid(qx)
        lb = lb_ref[...]
        f = lb + (1.0 - lb) * _sigmoid(f_ref[b])
        k = 1.0 - f
        lg = jnp.log(f)
        if t_valid < chunk:
            k = jnp.where(valid, k, 0.0)
            lg = jnp.where(valid, lg, 0.0)
        v = i_ref[b].astype(BF16)
        cum = sum(_dot(tri, part) for part in _split3(lg))
        cum_mid = cum[mid_row:mid_row + 1, :]
        cum_last = cum[chunk - 1:chunk, :]
        qt = (q * jnp.exp(cum - cum_mid)).astype(BF16)
        kt = (k * jnp.exp(cum_mid - cum)).astype(BF16)
        qe = (q * jnp.exp(cum)).astype(BF16)
        kd = (k * jnp.exp(cum_last - cum)).astype(BF16)
        decay = jnp.exp(cum_last)
        heads = range(HG_HEADS)
        sl = [slice(h * dk, (h + 1) * dk) for h in heads]
        st = [st_scr[b, h] for h in heads]
        a = [_nt(qt[:, sl[h]], kt[:, sl[h]]) for h in heads]
        inter = [_nt(qe[:, sl[h]], st[h].astype(BF16)) for h in heads]
        upd = [_tn(v[:, sl[h]], kd[:, sl[h]]) for h in heads]
        for h in heads:
            st_scr[b, h] = st[h] * decay[:, sl[h]] + upd[h]
        a = [jnp.where(causal, a[h], 0.0).astype(BF16) for h in heads]
        o = [_dot(a[h], v[:, sl[h]]) + inter[h] for h in heads]
        outs = [o[h] * lax.rsqrt(jnp.mean(o[h] * o[h], axis=-1, keepdims=True) + EPS) for h in heads]
        gx = g_ref[b]
        o = jnp.concatenate(outs, axis=1) * nw_ref[...] * (gx * _sigmoid(gx))
        o_ref[b] = o.astype(o_ref.dtype)
        return carry

    lax.fori_loop(0, bb, body, 0, unroll=2)

    @pl.when(c == pl.num_programs(1) - 1)
    def _final():
        def store(b, carry):
            for h in range(HG_HEADS):
                s_ref[b, h] = st_scr[b, h].T
            return carry
        lax.fori_loop(0, bb, store, 0)


def _hgrn(proj3, lb, norm_w, s0, cols, chunk, t_valid, bb):
    bsz, t, _ = proj3.shape
    width = lb.shape[0]
    dk = width // HG_HEADS
    has_state = s0 is not None
    spec = lambda cb: pl.BlockSpec((bb, chunk, width), lambda b, c: (b, c, cb))
    vec = pl.BlockSpec((1, width), lambda b, c: (0, 0))
    st_spec = pl.BlockSpec((bb, HG_HEADS, dk, dk), lambda b, c: (b, 0, 0, 0))
    in_specs = [spec(cols[0]), spec(cols[1]), spec(cols[2]), spec(cols[3]), vec, vec]
    args = [proj3, proj3, proj3, proj3, lb.reshape(1, width), norm_w.reshape(1, width)]
    if has_state:
        in_specs.append(st_spec)
        args.append(s0)
    return pl.pallas_call(
        functools.partial(_hgrn_kernel, t_valid=t_valid, has_state=has_state),
        grid=(bsz // bb, t // chunk),
        in_specs=in_specs,
        out_specs=[pl.BlockSpec((bb, chunk, width), lambda b, c: (b, c, 0)), st_spec],
        out_shape=[jax.ShapeDtypeStruct((bsz, t, width), BF16),
                   jax.ShapeDtypeStruct((bsz, HG_HEADS, dk, dk), F32)],
        scratch_shapes=[pltpu.VMEM((bb, HG_HEADS, dk, dk), F32)],
        compiler_params=_cparams("parallel", "arbitrary"),
        name="hgrn_state" if has_state else "hgrn_prompt",
    )(*args)


def _out_proj_kernel(x_ref, a_ref, oh_ref, wo_ref, nw_ref, h_ref, hnt_ref):
    aw = a_ref.shape[1]
    mix = _dot(a_ref[...], wo_ref[:aw, :]) + _dot(oh_ref[...], wo_ref[aw:, :])
    h = x_ref[...] + mix
    h_ref[...] = h
    hn = h * lax.rsqrt(jnp.mean(h * h, axis=-1, keepdims=True) + EPS) * nw_ref[...]
    hnt_ref[...] = hn.T.astype(hnt_ref.dtype)


def _out_proj(x2d, attn, oh, wo_bf16, norm_w, tm):
    t, d = x2d.shape
    aw, hw = attn.shape[1], oh.shape[1]
    return pl.pallas_call(
        _out_proj_kernel,
        grid=(t // tm,),
        in_specs=[pl.BlockSpec((tm, d), lambda i: (i, 0)),
                  pl.BlockSpec((tm, aw), lambda i: (i, 0)),
                  pl.BlockSpec((tm, hw), lambda i: (i, 0)),
                  pl.BlockSpec((aw + hw, d), lambda i: (0, 0)),
                  pl.BlockSpec((1, d), lambda i: (0, 0))],
        out_specs=[pl.BlockSpec((tm, d), lambda i: (i, 0)),
                   pl.BlockSpec((d, tm), lambda i: (0, i))],
        out_shape=[jax.ShapeDtypeStruct((t, d), F32),
                   jax.ShapeDtypeStruct((d, t), BF16)],
        compiler_params=_cparams("parallel"),
        name="out_proj",
    )(x2d, attn, oh, wo_bf16, norm_w.reshape(1, d))


def _peer_scores_kernel(hnt_ref, wqt_ref, keys_ref, sc_ref):
    qt = _dot(wqt_ref[...], hnt_ref[...])
    half = keys_ref.shape[2]
    for hc in range(keys_ref.shape[0]):
        s = _dot(keys_ref[hc], qt[hc * half:(hc + 1) * half, :].astype(BF16))
        for lt in range(sc_ref.shape[0]):
            sc_ref[lt, hc] = s[:, lt * LANES:(lt + 1) * LANES]


def _peer_scores(hnt, wqt_bf16, keys_bf16, tb):
    d, t = hnt.shape
    nhc, nk, half = keys_bf16.shape
    return pl.pallas_call(
        _peer_scores_kernel,
        grid=(t // tb,),
        in_specs=[pl.BlockSpec((d, tb), lambda i: (0, i)),
                  pl.BlockSpec((nhc * half, d), lambda i: (0, 0)),
                  pl.BlockSpec((nhc, nk, half), lambda i: (0, 0, 0))],
        out_specs=pl.BlockSpec((tb // LANES, nhc, nk, LANES), lambda i: (i, 0, 0, 0)),
        out_shape=jax.ShapeDtypeStruct((t // LANES, nhc, nk, LANES), F32),
        compiler_params=_cparams("parallel"),
        name="peer_scores",
    )(hnt, wqt_bf16, keys_bf16)


def _sort16_pairs():
    def merge(lo, hi, r):
        step = r * 2
        if step < hi - lo:
            yield from merge(lo, hi, step)
            yield from merge(lo + r, hi, step)
            yield from [(i, i + r) for i in range(lo + r, hi - r, step)]
        else:
            yield (lo, lo + r)

    def sort(lo, hi):
        if hi - lo >= 1:
            mid = lo + (hi - lo) // 2
            yield from sort(lo, mid)
            yield from sort(mid + 1, hi)
            yield from merge(lo, hi, 1)

    return tuple(sort(0, PEER_TOPK - 1))


_SORT16 = _sort16_pairs()


def _bitonic_to_sorted(z):
    z = list(z)
    d = PEER_TOPK // 2
    while d >= 1:
        for i in range(PEER_TOPK):
            if i & d == 0:
                hi, lo = jnp.maximum(z[i], z[i + d]), jnp.minimum(z[i], z[i + d])
                z[i], z[i + d] = hi, lo
        d //= 2
    return z


def _merge_bitonic(top, other):
    z = list(top)
    m = len(other)
    for r in range(PEER_TOPK - m, PEER_TOPK):
        z[r] = jnp.maximum(top[r], other[PEER_TOPK - 1 - r])
    return z


def _top16_desc(x):
    n = x.shape[0] // SUBLANES
    xs = [x[g * SUBLANES:(g + 1) * SUBLANES, :] for g in range(n)]
    for i, j in _SORT16:
        xs[i], xs[j] = jnp.maximum(xs[i], xs[j]), jnp.minimum(xs[i], xs[j])
    shift = SUBLANES // 2
    while shift >= 1:
        ys = [pltpu.roll(v, shift, axis=0) for v in xs]
        xs = _bitonic_to_sorted(_merge_bitonic(xs, ys))
        shift //= 2
    return xs


def _peer_select(sc_ref, thr_scr, pw_scr, q_scr, h, lt):
    s0 = sc_ref[lt, 2 * h]
    s1 = sc_ref[lt, 2 * h + 1]
    a = _top16_desc(s0)
    b = _top16_desc(s1)
    lists = [[a[r] + b[c] for c in range(PEER_TOPK // (r + 1))] for r in range(SUBLANES)]
    lists.append([a[r] + b[0] for r in range(SUBLANES, PEER_TOPK)])
    top = lists[0]
    for other in lists[1:-1]:
        top = _bitonic_to_sorted(_merge_bitonic(top, other))
    z = _merge_bitonic(top, lists[-1])
    tau = functools.reduce(jnp.minimum, z)
    best = a[0] + b[0]
    zsum = jnp.zeros_like(tau)
    inf = jnp.full_like(tau, jnp.inf)
    thr_rank = []
    for r, cand in enumerate(lists):
        hits = [v >= tau for v in cand]
        for v, hit in zip(cand, hits):
            zsum = zsum + jnp.where(hit, jnp.exp(v - best), 0.0)
        if r < SUBLANES:
            t = inf
            for c, hit in enumerate(hits):
                t = jnp.where(hit, b[c], t)
            thr_rank.append(t)
        else:
            thr_rank.extend(jnp.where(hit, b[0], inf) for hit in hits)
    inv = 1.0 / zsum
    for g in range(s0.shape[0] // SUBLANES):
        rows = slice(g * SUBLANES, (g + 1) * SUBLANES)
        x0 = s0[rows, :]
        thr = jnp.full_like(x0, jnp.inf)
        for r in range(PEER_TOPK):
            thr = jnp.where(x0 == a[r], thr_rank[r], thr)
        thr_scr[lt, h, rows, :] = thr
        pw_scr[lt, h, rows, :] = jnp.exp(x0 - a[0]) * inv
        q_scr[lt, h, rows, :] = jnp.exp(s1[rows, :] - b[0])


def _peer_dense_kernel(sc_ref, hnt_ref, u_ref, vt_ref, yt_ref, thr_scr, pw_scr, q_scr, h_scr, g_scr):
    e = pl.program_id(1)
    tb = hnt_ref.shape[1]
    n_lt, _, nk, _ = sc_ref.shape
    n_i = u_ref.shape[0] // nk

    @pl.when(e == 0)
    def _select():
        def body(it, carry):
            _peer_select(sc_ref, thr_scr, pw_scr, q_scr, it // n_lt, it % n_lt)
            return carry
        lax.fori_loop(0, PEER_HEADS * n_lt, body, 0)
        yt_ref[...] = jnp.zeros_like(yt_ref)

    h_scr[:, :tb] = _dot(u_ref[...], hnt_ref[...])

    i0 = pl.multiple_of(e * n_i, SUBLANES)

    def body(lt, carry):
        lanes = pl.ds(pl.multiple_of(lt * LANES, LANES), LANES)
        for ii in range(n_i):
            rows = slice(ii * nk, (ii + 1) * nk)
            w = jnp.zeros((nk, LANES), F32)
            for h in range(PEER_HEADS):
                thr = thr_scr[lt, h, pl.ds(i0, n_i), :][ii:ii + 1, :]
                pw = pw_scr[lt, h, pl.ds(i0, n_i), :][ii:ii + 1, :]
                w = w + jnp.where(sc_ref[lt, 2 * h + 1] >= thr, q_scr[lt, h], 0.0) * pw
            x = h_scr[rows, lanes]
            act = 0.5 * x * (1.0 + lax.erf(x * (2.0 ** -0.5)))
            g_scr[rows, lanes] = (w * act).astype(g_scr.dtype)
        return carry

    lax.fori_loop(0, n_lt, body, 0)
    yt_ref[...] += _dot(vt_ref[...], g_scr[...])


def _peer_dense(sc, hnt, u_bf16, vt_bf16, tb, eb):
    _, nhc, nk, _ = sc.shape
    d, t = hnt.shape
    n_exp = u_bf16.shape[0]
    n_lt = tb // LANES
    assert eb == SUBLANES * nk and n_exp == nk * nk and t % tb == 0 and tb % LANES == 0
    sel = pltpu.VMEM((n_lt, PEER_HEADS, nk, LANES), F32)
    return pl.pallas_call(
        _peer_dense_kernel,
        grid=(t // tb, n_exp // eb),
        in_specs=[pl.BlockSpec((n_lt, nhc, nk, LANES), lambda i, e: (i, 0, 0, 0)),
                  pl.BlockSpec((d, tb), lambda i, e: (0, i)),
                  pl.BlockSpec((eb, d), lambda i, e: (e, 0)),
                  pl.BlockSpec((d, eb), lambda i, e: (0, e))],
        out_specs=pl.BlockSpec((d, tb), lambda i, e: (0, i)),
        out_shape=jax.ShapeDtypeStruct((d, t), F32),
        scratch_shapes=[sel, sel, sel,
                        pltpu.VMEM((eb, tb + LANES), F32),
                        pltpu.VMEM((eb, tb), BF16)],
        compiler_params=_cparams("parallel", "arbitrary"),
        name="peer_dense",
    )(sc, hnt, u_bf16, vt_bf16)


def _final_kernel(h_ref, yt_ref, nw_ref, o_ref):
    y = h_ref[...] + yt_ref[...].T
    o_ref[...] = y * lax.rsqrt(jnp.mean(y * y, axis=-1, keepdims=True) + EPS) * nw_ref[...]


def _final(h2d, yt, norm_w, tm):
    t, d = h2d.shape
    return pl.pallas_call(
        _final_kernel,
        grid=(t // tm,),
        in_specs=[pl.BlockSpec((tm, d), lambda i: (i, 0)),
                  pl.BlockSpec((d, tm), lambda i: (0, i)),
                  pl.BlockSpec((1, d), lambda i: (0, 0))],
        out_specs=pl.BlockSpec((tm, d), lambda i: (i, 0)),
        out_shape=jax.ShapeDtypeStruct((t, d), F32),
        compiler_params=_cparams("parallel"),
        name="final_norm",
    )(h2d, yt, norm_w.reshape(1, d))


TOKEN_BLOCK = 512
EXPERT_BLOCK = 1024
SAMPLE_T_PAD = 16
SAMPLE_K_PAD = 256


def _ffn(x2d, attn, oh, w, tb):
    h, hnt = _out_proj(x2d, attn, oh, w['wo'], w['norm_ffn'], tb)
    sc = _peer_scores(hnt, w['wqt'], w['keys'], tb)
    yt = _peer_dense(sc, hnt, w['u'], w['vt'], tb, EXPERT_BLOCK)
    return _final(h, yt, w['norm_final'], tb)


def kernel(x_prompt, x_sample, cache_k_win, cache_v_win, state_hgrn, norm_mix_w, w_in, attn_sinks,
           rel_bias_table, hg_lb, hg_norm_w, w_o, norm_ffn_w, peer_w_q, peer_sub_keys, peer_u, peer_v,
           norm_final_w):
    bsz, seq, d = x_prompt.shape
    dbsz, dseq, _ = x_sample.shape
    aw = ATTN_HEADS * HEAD_DIM
    kw = ATTN_KV_HEADS * HEAD_DIM
    hw = hg_norm_w.shape[1]
    wb = cache_k_win.shape[2]

    wi = w_in[0]
    w_in_r = jnp.concatenate([wi[:, :aw], wi[:, aw + 2 * kw:], wi[:, aw:aw + 2 * kw]], axis=1).astype(BF16)
    col_k = (aw + 4 * hw) // kw
    col_v = col_k + 1
    hg_cols = (1, 2, 3, 4)
    lb = jax.nn.softmax(hg_lb.astype(F32), axis=0)[0]
    nhc = PEER_HEADS * 2
    w = {
        'wo': w_o[0].astype(BF16),
        'norm_ffn': norm_ffn_w[0],
        'wqt': peer_w_q[0].T.astype(BF16),
        'keys': peer_sub_keys[0].reshape(nhc, peer_sub_keys.shape[3], peer_sub_keys.shape[4]).astype(BF16),
        'u': peer_u[0].astype(BF16),
        'vt': peer_v[0].T.astype(BF16),
        'norm_final': norm_final_w,
    }
    sinks = attn_sinks[0].astype(F32)

    proj_p = _in_proj(x_prompt.reshape(bsz * seq, d), norm_mix_w[0], w_in_r, TOKEN_BLOCK)
    proj_p3 = proj_p.reshape(bsz, seq, -1)
    blk = WINDOW
    dist_p = (jnp.arange(blk)[:, None] + blk) - jnp.arange(2 * blk)[None, :]
    bias_p = _masked_bias(rel_bias_table, blk, 2 * blk, blk, (dist_p >= 0) & (dist_p <= WINDOW))
    attn_p = _swa_prompt(proj_p3, sinks, bias_p, 0, col_k, col_v)
    oh_p, st_p = _hgrn(proj_p3, lb, hg_norm_w[0], None, hg_cols, HG_CHUNK, HG_CHUNK, bsz)
    y_p = _ffn(x_prompt.reshape(bsz * seq, d), attn_p.reshape(bsz * seq, aw),
               oh_p.reshape(bsz * seq, hw), w, TOKEN_BLOCK)
    k_off = aw + 4 * hw
    wp = min(WINDOW, seq)
    k_win_p = proj_p3[:, seq - wp:, k_off:k_off + kw].reshape(1, bsz, wp, ATTN_KV_HEADS, HEAD_DIM)
    v_win_p = proj_p3[:, seq - wp:, k_off + kw:k_off + 2 * kw].reshape(1, bsz, wp, ATTN_KV_HEADS, HEAD_DIM)

    tp = SAMPLE_T_PAD
    xs_pad = jnp.pad(x_sample, ((0, 0), (0, tp - dseq), (0, 0)))
    proj_s3 = _in_proj(xs_pad.reshape(dbsz * tp, d), norm_mix_w[0], w_in_r, TOKEN_BLOCK).reshape(dbsz, tp, -1)
    k_new = proj_s3[:, :dseq, k_off:k_off + kw]
    v_new = proj_s3[:, :dseq, k_off + kw:k_off + 2 * kw]
    kk = jnp.concatenate([cache_k_win[0].reshape(dbsz, wb, kw), k_new], axis=1)
    vv = jnp.concatenate([cache_v_win[0].reshape(dbsz, wb, kw), v_new], axis=1)
    kpad = SAMPLE_K_PAD - (wb + dseq)
    kk_pad = jnp.pad(kk, ((0, 0), (0, kpad), (0, 0)))
    vv_pad = jnp.pad(vv, ((0, 0), (0, kpad), (0, 0)))
    dist_s = (wb + jnp.arange(tp))[:, None] - jnp.arange(SAMPLE_K_PAD)[None, :]
    mask_s = ((dist_s >= 0) & (dist_s <= WINDOW) & (jnp.arange(tp)[:, None] < dseq)
              & (jnp.arange(SAMPLE_K_PAD)[None, :] < wb + dseq))
    bias_s = _masked_bias(rel_bias_table, tp, SAMPLE_K_PAD, wb, mask_s)
    attn_s = _swa_sample(proj_s3, kk_pad, vv_pad, sinks, bias_s, 0, 16)
    oh_s, st_s = _hgrn(proj_s3, lb, hg_norm_w[0], state_hgrn[0], hg_cols, tp, dseq, 8)
    y_s = _ffn(x_sample.reshape(dbsz * dseq, d), attn_s[:, :dseq].reshape(dbsz * dseq, aw),
               oh_s[:, :dseq].reshape(dbsz * dseq, hw), w, TOKEN_BLOCK)
    k_win_s = kk[:, dseq:].reshape(1, dbsz, wb, ATTN_KV_HEADS, HEAD_DIM)
    v_win_s = vv[:, dseq:].reshape(1, dbsz, wb, ATTN_KV_HEADS, HEAD_DIM)

    return (y_p.reshape(bsz, seq, d), y_s.reshape(dbsz, dseq, d), k_win_p, v_win_p, st_p[None],
            k_win_s, v_win_s, st_s[None])
```

```python
import functools
import math

import jax
import jax.numpy as jnp
from jax import lax
from jax.experimental import pallas as pl
from jax.experimental.pallas import tpu as pltpu

F32 = jnp.float32
BF16 = jnp.bfloat16

EPS = 1e-6
NEG = -1e30

ATTN_HEADS = 8
ATTN_KV_HEADS = 2
HEAD_DIM = 64
WINDOW = 128
REL_BUCKETS = 32
HG_HEADS = 4
HG_CHUNK = 64
PEER_HEADS = 8
PEER_TOPK = 16

LANES = 128
SUBLANES = 8
VMEM_LIMIT = 56 * 1024 * 1024


def _cparams(*sem):
    return pltpu.CompilerParams(dimension_semantics=sem, vmem_limit_bytes=VMEM_LIMIT)


def _nt(a, b):
    return lax.dot_general(a, b, (((1,), (1,)), ((), ())), preferred_element_type=F32)


def _tn(a, b):
    return lax.dot_general(a, b, (((0,), (0,)), ((), ())), preferred_element_type=F32)


def _dot(a, b):
    return jnp.dot(a, b, preferred_element_type=F32)


def _sigmoid(x):
    return 1.0 / (1.0 + jnp.exp(-x))


def _in_proj_kernel(x_ref, nw_ref, w_ref, o_ref):
    x = x_ref[...]
    xn = x * lax.rsqrt(jnp.mean(x * x, axis=-1, keepdims=True) + EPS) * nw_ref[...]
    o_ref[...] = _dot(xn.astype(BF16), w_ref[...])


def _in_proj(x2d, norm_w, w_bf16, tm):
    t, d = x2d.shape
    n = w_bf16.shape[1]
    return pl.pallas_call(
        _in_proj_kernel,
        grid=(t // tm,),
        in_specs=[pl.BlockSpec((tm, d), lambda i: (i, 0)),
                  pl.BlockSpec((1, d), lambda i: (0, 0)),
                  pl.BlockSpec((d, n), lambda i: (0, 0))],
        out_specs=pl.BlockSpec((tm, n), lambda i: (i, 0)),
        out_shape=jax.ShapeDtypeStruct((t, n), F32),
        compiler_params=_cparams("parallel"),
        name="in_proj",
    )(x2d, norm_w.reshape(1, d), w_bf16)


def _t5_bucket(dist):
    n = jnp.maximum(dist, 0)
    max_exact = REL_BUCKETS // 2
    nf = jnp.maximum(n, 1).astype(F32)
    large = max_exact + (jnp.log(nf / max_exact) / math.log(WINDOW / max_exact)
                         * (REL_BUCKETS - max_exact)).astype(jnp.int32)
    large = jnp.minimum(large, REL_BUCKETS - 1)
    return jnp.where(n < max_exact, n, large)


def _masked_bias(table, n_q, n_k, offset, mask):
    h = table.shape[1]
    diag = jnp.arange(n_q + n_k - 1) - (n_k - 1) + offset
    per_diag = table.astype(F32)[_t5_bucket(diag)].T
    w = jnp.pad(per_diag[:, ::-1], ((0, 0), (0, 1)))
    p = n_q + n_k
    skew = jnp.tile(w, (1, n_q))[:, :n_q * (p - 1)].reshape(h, n_q, p - 1)
    return jnp.where(mask[None], skew[:, :, n_q - 1:n_q - 1 + n_k], NEG)


def _swa_prompt_kernel(sink_ref, q_ref, kp_ref, kc_ref, vp_ref, vc_ref, bias_ref, o_ref):
    first = pl.program_id(1) == 0
    scale = HEAD_DIM ** -0.5
    group = ATTN_HEADS // ATTN_KV_HEADS
    blk = q_ref.shape[1]
    assert 2 * HEAD_DIM == LANES and kp_ref.shape[2] == LANES and ATTN_KV_HEADS == 2 and group % 2 == 0
    kk = jnp.concatenate([kp_ref[0], kc_ref[0]], axis=0)
    vv = jnp.concatenate([vp_ref[0], vc_ref[0]], axis=0)
    low = lax.broadcasted_iota(jnp.int32, kk.shape, 1) < HEAD_DIM
    col = lax.broadcasted_iota(jnp.int32, (blk, 2 * blk), 1)
    no_prev = (col < blk) & first
    def halves(x, kvh):
        own = jnp.where(low if kvh == 0 else ~low, x, 0.0)
        other = pltpu.roll(own, HEAD_DIM, axis=1)
        lo, hi = (own, other) if kvh == 0 else (other, own)
        return lo.astype(BF16), hi.astype(BF16)

    k_half = [halves(kk, kvh) for kvh in range(ATTN_KV_HEADS)]
    v_half = [halves(vv, kvh) for kvh in range(ATTN_KV_HEADS)]
    heads = range(ATTN_HEADS)
    kv_of = lambda h: h // group
    s = [_nt(q_ref[0, :, (h // 2) * LANES:(h // 2 + 1) * LANES].astype(BF16), k_half[kv_of(h)][h % 2])
         for h in heads]
    s = [jnp.where(no_prev, NEG, s[h] * scale + bias_ref[h]) for h in heads]
    m = [jnp.maximum(jnp.max(s[h], axis=-1, keepdims=True), sink_ref[h]) for h in heads]
    p = [jnp.exp(s[h] - m[h]) for h in heads]
    den = [jnp.sum(p[h], axis=-1, keepdims=True) + jnp.exp(sink_ref[h] - m[h]) for h in heads]
    p = [(p[h] * (1.0 / den[h])).astype(BF16) for h in heads]
    for tile in range(ATTN_HEADS // 2):
        kvh = kv_of(2 * tile)
        o = _dot(p[2 * tile], v_half[kvh][0]) + _dot(p[2 * tile + 1], v_half[kvh][1])
        o_ref[0, :, tile * LANES:(tile + 1) * LANES] = o.astype(o_ref.dtype)


def _swa_prompt(proj3, sinks, bias, col_q, col_k, col_v):
    bsz, seq, _ = proj3.shape
    blk = WINDOW
    aw = ATTN_HEADS * HEAD_DIM
    kw = ATTN_KV_HEADS * HEAD_DIM
    prev = lambda b, n: (b, jnp.maximum(n - 1, 0))
    return pl.pallas_call(
        _swa_prompt_kernel,
        grid=(bsz, seq // blk),
        in_specs=[pl.BlockSpec(memory_space=pltpu.SMEM),
                  pl.BlockSpec((1, blk, aw), lambda b, n: (b, n, col_q)),
                  pl.BlockSpec((1, blk, kw), lambda b, n: prev(b, n) + (col_k,)),
                  pl.BlockSpec((1, blk, kw), lambda b, n: (b, n, col_k)),
                  pl.BlockSpec((1, blk, kw), lambda b, n: prev(b, n) + (col_v,)),
                  pl.BlockSpec((1, blk, kw), lambda b, n: (b, n, col_v)),
                  pl.BlockSpec((ATTN_HEADS, blk, 2 * blk), lambda b, n: (0, 0, 0))],
        out_specs=pl.BlockSpec((1, blk, aw), lambda b, n: (b, n, 0)),
        out_shape=jax.ShapeDtypeStruct((bsz, seq, aw), BF16),
        compiler_params=_cparams("parallel", "arbitrary"),
        name="swa_prompt",
    )(sinks, proj3, proj3, proj3, proj3, proj3, bias)


def _swa_sample_kernel(sink_ref, q_ref, kk_ref, vv_ref, bias_ref, o_ref):
    scale = HEAD_DIM ** -0.5
    group = ATTN_HEADS // ATTN_KV_HEADS
    for h in range(ATTN_HEADS):
        kv = (h // group) * HEAD_DIM
        qh = q_ref[:, :, h * HEAD_DIM:(h + 1) * HEAD_DIM].astype(BF16)
        kh = kk_ref[:, :, kv:kv + HEAD_DIM].astype(BF16)
        vh = vv_ref[:, :, kv:kv + HEAD_DIM].astype(BF16)
        s = jnp.einsum('bqd,bkd->bqk', qh, kh, preferred_element_type=F32) * scale + bias_ref[h][None]
        sink = sink_ref[h]
        m = jnp.maximum(jnp.max(s, axis=-1, keepdims=True), sink)
        p = jnp.exp(s - m)
        den = jnp.sum(p, axis=-1, keepdims=True) + jnp.exp(sink - m)
        o = jnp.einsum('bqk,bkd->bqd', p.astype(BF16), vh, preferred_element_type=F32) / den
        o_ref[:, :, h * HEAD_DIM:(h + 1) * HEAD_DIM] = o.astype(o_ref.dtype)


def _swa_sample(proj3, kk, vv, sinks, bias, col_q, bb):
    bsz, tp, _ = proj3.shape
    kp = kk.shape[1]
    aw = ATTN_HEADS * HEAD_DIM
    kw = ATTN_KV_HEADS * HEAD_DIM
    return pl.pallas_call(
        _swa_sample_kernel,
        grid=(bsz // bb,),
        in_specs=[pl.BlockSpec(memory_space=pltpu.SMEM),
                  pl.BlockSpec((bb, tp, aw), lambda b: (b, 0, col_q)),
                  pl.BlockSpec((bb, kp, kw), lambda b: (b, 0, 0)),
                  pl.BlockSpec((bb, kp, kw), lambda b: (b, 0, 0)),
                  pl.BlockSpec((ATTN_HEADS, tp, kp), lambda b: (0, 0, 0))],
        out_specs=pl.BlockSpec((bb, tp, aw), lambda b: (b, 0, 0)),
        out_shape=jax.ShapeDtypeStruct((bsz, tp, aw), BF16),
        compiler_params=_cparams("parallel"),
        name="swa_sample",
    )(sinks, proj3, kk, vv, bias)


def _split3(x):
    hi = x.astype(BF16)
    r = x - hi.astype(F32)
    mid = r.astype(BF16)
    lo = (r - mid.astype(F32)).astype(BF16)
    return hi, mid, lo


def _hgrn_kernel(*refs, t_valid, has_state):
    if has_state:
        q_ref, f_ref, i_ref, g_ref, lb_ref, nw_ref, s0_ref, o_ref, s_ref, st_scr = refs
    else:
        q_ref, f_ref, i_ref, g_ref, lb_ref, nw_ref, o_ref, s_ref, st_scr = refs
    bb, chunk, width = q_ref.shape
    dk = width // HG_HEADS
    c = pl.program_id(1)

    @pl.when(c == 0)
    def _init():
        if has_state:
            def load(b, carry):
                for h in range(HG_HEADS):
                    st_scr[b, h] = s0_ref[b, h].T
                return carry
            lax.fori_loop(0, bb, load, 0)
        else:
            st_scr[...] = jnp.zeros_like(st_scr)

    row = lax.broadcasted_iota(jnp.int32, (chunk, chunk), 0)
    col = lax.broadcasted_iota(jnp.int32, (chunk, chunk), 1)
    causal = row >= col
    tri = jnp.where(causal, 1.0, 0.0).astype(BF16)
    valid = lax.broadcasted_iota(jnp.int32, (chunk, width), 0) < t_valid
    mid_row = chunk // 2

    def body(b, carry):
        qx = q_ref[b]
        q = qx * _sigmoid(qx)
        lb = lb_ref[...]
        f = lb + (1.0 - lb) * _sigmoid(f_ref[b])
        k = 1.0 - f
        lg = jnp.log(f)
        if t_valid < chunk:
            k = jnp.where(valid, k, 0.0)
            lg = jnp.where(valid, lg, 0.0)
        v = i_ref[b].astype(BF16)
        cum = sum(_dot(tri, part) for part in _split3(lg))
        cum_mid = cum[mid_row:mid_row + 1, :]
        cum_last = cum[chunk - 1:chunk, :]
        qt = (q * jnp.exp(cum - cum_mid)).astype(BF16)
        kt = (k * jnp.exp(cum_mid - cum)).astype(BF16)
        qe = (q * jnp.exp(cum)).astype(BF16)
        kd = (k * jnp.exp(cum_last - cum)).astype(BF16)
        decay = jnp.exp(cum_last)
        heads = range(HG_HEADS)
        sl = [slice(h * dk, (h + 1) * dk) for h in heads]
        st = [st_scr[b, h] for h in heads]
        a = [_nt(qt[:, sl[h]], kt[:, sl[h]]) for h in heads]
        inter = [_nt(qe[:, sl[h]], st[h].astype(BF16)) for h in heads]
        upd = [_tn(v[:, sl[h]], kd[:, sl[h]]) for h in heads]
        for h in heads:
            st_scr[b, h] = st[h] * decay[:, sl[h]] + upd[h]
        a = [jnp.where(causal, a[h], 0.0).astype(BF16) for h in heads]
        o = [_dot(a[h], v[:, sl[h]]) + inter[h] for h in heads]
        outs = [o[h] * lax.rsqrt(jnp.mean(o[h] * o[h], axis=-1, keepdims=True) + EPS) for h in heads]
        gx = g_ref[b]
        o = jnp.concatenate(outs, axis=1) * nw_ref[...] * (gx * _sigmoid(gx))
        o_ref[b] = o.astype(o_ref.dtype)
        return carry

    lax.fori_loop(0, bb, body, 0, unroll=2)

    @pl.when(c == pl.num_programs(1) - 1)
    def _final():
        def store(b, carry):
            for h in range(HG_HEADS):
                s_ref[b, h] = st_scr[b, h].T
            return carry
        lax.fori_loop(0, bb, store, 0)


def _hgrn(proj3, lb, norm_w, s0, cols, chunk, t_valid, bb):
    bsz, t, _ = proj3.shape
    width = lb.shape[0]
    dk = width // HG_HEADS
    has_state = s0 is not None
    spec = lambda cb: pl.BlockSpec((bb, chunk, width), lambda b, c: (b, c, cb))
    vec = pl.BlockSpec((1, width), lambda b, c: (0, 0))
    st_spec = pl.BlockSpec((bb, HG_HEADS, dk, dk), lambda b, c: (b, 0, 0, 0))
    in_specs = [spec(cols[0]), spec(cols[1]), spec(cols[2]), spec(cols[3]), vec, vec]
    args = [proj3, proj3, proj3, proj3, lb.reshape(1, width), norm_w.reshape(1, width)]
    if has_state:
        in_specs.append(st_spec)
        args.append(s0)
    return pl.pallas_call(
        functools.partial(_hgrn_kernel, t_valid=t_valid, has_state=has_state),
        grid=(bsz // bb, t // chunk),
        in_specs=in_specs,
        out_specs=[pl.BlockSpec((bb, chunk, width), lambda b, c: (b, c, 0)), st_spec],
        out_shape=[jax.ShapeDtypeStruct((bsz, t, width), BF16),
                   jax.ShapeDtypeStruct((bsz, HG_HEADS, dk, dk), F32)],
        scratch_shapes=[pltpu.VMEM((bb, HG_HEADS, dk, dk), F32)],
        compiler_params=_cparams("parallel", "arbitrary"),
        name="hgrn_state" if has_state else "hgrn_prompt",
    )(*args)


def _out_proj_kernel(x_ref, a_ref, oh_ref, wo_ref, nw_ref, h_ref, hnt_ref):
    aw = a_ref.shape[1]
    mix = _dot(a_ref[...], wo_ref[:aw, :]) + _dot(oh_ref[...], wo_ref[aw:, :])
    h = x_ref[...] + mix
    h_ref[...] = h
    hn = h * lax.rsqrt(jnp.mean(h * h, axis=-1, keepdims=True) + EPS) * nw_ref[...]
    hnt_ref[...] = hn.T.astype(hnt_ref.dtype)


def _out_proj(x2d, attn, oh, wo_bf16, norm_w, tm):
    t, d = x2d.shape
    aw, hw = attn.shape[1], oh.shape[1]
    return pl.pallas_call(
        _out_proj_kernel,
        grid=(t // tm,),
        in_specs=[pl.BlockSpec((tm, d), lambda i: (i, 0)),
                  pl.BlockSpec((tm, aw), lambda i: (i, 0)),
                  pl.BlockSpec((tm, hw), lambda i: (i, 0)),
                  pl.BlockSpec((aw + hw, d), lambda i: (0, 0)),
                  pl.BlockSpec((1, d), lambda i: (0, 0))],
        out_specs=[pl.BlockSpec((tm, d), lambda i: (i, 0)),
                   pl.BlockSpec((d, tm), lambda i: (0, i))],
        out_shape=[jax.ShapeDtypeStruct((t, d), F32),
                   jax.ShapeDtypeStruct((d, t), BF16)],
        compiler_params=_cparams("parallel"),
        name="out_proj",
    )(x2d, attn, oh, wo_bf16, norm_w.reshape(1, d))


def _peer_scores_kernel(hnt_ref, wqt_ref, keys_ref, sc_ref):
    qt = _dot(wqt_ref[...], hnt_ref[...])
    half = keys_ref.shape[2]
    for hc in range(keys_ref.shape[0]):
        s = _dot(keys_ref[hc], qt[hc * half:(hc + 1) * half, :].astype(BF16))
        for lt in range(sc_ref.shape[0]):
            sc_ref[lt, hc] = s[:, lt * LANES:(lt + 1) * LANES]


def _peer_scores(hnt, wqt_bf16, keys_bf16, tb):
    d, t = hnt.shape
    nhc, nk, half = keys_bf16.shape
    return pl.pallas_call(
        _peer_scores_kernel,
        grid=(t // tb,),
        in_specs=[pl.BlockSpec((d, tb), lambda i: (0, i)),
                  pl.BlockSpec((nhc * half, d), lambda i: (0, 0)),
                  pl.BlockSpec((nhc, nk, half), lambda i: (0, 0, 0))],
        out_specs=pl.BlockSpec((tb // LANES, nhc, nk, LANES), lambda i: (i, 0, 0, 0)),
        out_shape=jax.ShapeDtypeStruct((t // LANES, nhc, nk, LANES), F32),
        compiler_params=_cparams("parallel"),
        name="peer_scores",
    )(hnt, wqt_bf16, keys_bf16)


def _sort16_pairs():
    def merge(lo, hi, r):
        step = r * 2
        if step < hi - lo:
            yield from merge(lo, hi, step)
            yield from merge(lo + r, hi, step)
            yield from [(i, i + r) for i in range(lo + r, hi - r, step)]
        else:
            yield (lo, lo + r)

    def sort(lo, hi):
        if hi - lo >= 1:
            mid = lo + (hi - lo) // 2
            yield from sort(lo, mid)
            yield from sort(mid + 1, hi)
            yield from merge(lo, hi, 1)

    return tuple(sort(0, PEER_TOPK - 1))


_SORT16 = _sort16_pairs()


def _bitonic_to_sorted(z):
    z = list(z)
    d = PEER_TOPK // 2
    while d >= 1:
        for i in range(PEER_TOPK):
            if i & d == 0:
                hi, lo = jnp.maximum(z[i], z[i + d]), jnp.minimum(z[i], z[i + d])
                z[i], z[i + d] = hi, lo
        d //= 2
    return z


def _merge_bitonic(top, other):
    z = list(top)
    m = len(other)
    for r in range(PEER_TOPK - m, PEER_TOPK):
        z[r] = jnp.maximum(top[r], other[PEER_TOPK - 1 - r])
    return z


def _top16_desc(x):
    n = x.shape[0] // SUBLANES
    xs = [x[g * SUBLANES:(g + 1) * SUBLANES, :] for g in range(n)]
    for i, j in _SORT16:
        xs[i], xs[j] = jnp.maximum(xs[i], xs[j]), jnp.minimum(xs[i], xs[j])
    shift = SUBLANES // 2
    while shift >= 1:
        ys = [pltpu.roll(v, shift, axis=0) for v in xs]
        xs = _bitonic_to_sorted(_merge_bitonic(xs, ys))
        shift //= 2
    return xs


def _peer_select(sc_ref, thr_scr, pw_scr, q_scr, h, lt):
    s0 = sc_ref[lt, 2 * h]
    s1 = sc_ref[lt, 2 * h + 1]
    a = _top16_desc(s0)
    b = _top16_desc(s1)
    lists = [[a[r] + b[c] for c in range(PEER_TOPK // (r + 1))] for r in range(SUBLANES)]
    lists.append([a[r] + b[0] for r in range(SUBLANES, PEER_TOPK)])
    top = lists[0]
    for other in lists[1:-1]:
        top = _bitonic_to_sorted(_merge_bitonic(top, other))
    z = _merge_bitonic(top, lists[-1])
    tau = functools.reduce(jnp.minimum, z)
    best = a[0] + b[0]
    zsum = jnp.zeros_like(tau)
    inf = jnp.full_like(tau, jnp.inf)
    thr_rank = []
    for r, cand in enumerate(lists):
        hits = [v >= tau for v in cand]
        for v, hit in zip(cand, hits):
            zsum = zsum + jnp.where(hit, jnp.exp(v - best), 0.0)
        if r < SUBLANES:
            t = inf
            for c, hit in enumerate(hits):
                t = jnp.where(hit, b[c], t)
            thr_rank.append(t)
        else:
            thr_rank.extend(jnp.where(hit, b[0], inf) for hit in hits)
    inv = 1.0 / zsum
    for g in range(s0.shape[0] // SUBLANES):
        rows = slice(g * SUBLANES, (g + 1) * SUBLANES)
        x0 = s0[rows, :]
        thr = jnp.full_like(x0, jnp.inf)
        for r in range(PEER_TOPK):
            thr = jnp.where(x0 == a[r], thr_rank[r], thr)
        thr_scr[lt, h, rows, :] = thr
        pw_scr[lt, h, rows, :] = jnp.exp(x0 - a[0]) * inv
        q_scr[lt, h, rows, :] = jnp.exp(s1[rows, :] - b[0])


MXU_TILE = 256
MXU_COUNT = 2
ACC_ROWS = 512
ACC_PRE = 0
ACC_OUT = ACC_ROWS // 4


def _peer_dense_kernel(sc_ref, hnt_ref, u_ref, vt_ref, yt_ref, thr_scr, pw_scr, q_scr,
                       h0_scr, h1_scr, g0_scr, g1_scr):
    s = pl.program_id(1)
    n_e = pl.num_programs(1) - 2
    d, tb = hnt_ref.shape
    n_lt, _, nk, _ = sc_ref.shape
    eb = u_ref.shape[0]
    n_i = eb // nk
    assert tb == MXU_COUNT * MXU_TILE and eb % ACC_ROWS == 0 and d % ACC_ROWS == 0

    @pl.when(s == 0)
    def _select():
        def body(it, carry):
            _peer_select(sc_ref, thr_scr, pw_scr, q_scr, it // n_lt, it % n_lt)
            return carry
        lax.fori_loop(0, PEER_HEADS * n_lt, body, 0)
        yt_ref[...] = jnp.zeros_like(yt_ref)
        g0_scr[...] = jnp.zeros_like(g0_scr)
        g1_scr[...] = jnp.zeros_like(g1_scr)

    i0 = pl.multiple_of(jnp.clip(s - 1, 0, n_e - 1) * n_i, SUBLANES)

    def gate(lt, ii, h_r, g_w):
        lanes = slice(lt * LANES, (lt + 1) * LANES)
        rows = slice(ii * nk, (ii + 1) * nk)
        w = jnp.zeros((nk, LANES), F32)
        for h in range(PEER_HEADS):
            thr = thr_scr[lt, h, pl.ds(i0, n_i), :][ii:ii + 1, :]
            pw = pw_scr[lt, h, pl.ds(i0, n_i), :][ii:ii + 1, :]
            w = w + jnp.where(sc_ref[lt, 2 * h + 1] >= thr, q_scr[lt, h], 0.0) * pw
        x = h_r[rows, lanes]
        act = 0.5 * x * (1.0 + lax.erf(x * (2.0 ** -0.5)))
        g_w[rows, lanes] = (w * act).astype(g_w.dtype)

    def mxu_group(lhs_ref, rhs_ref, acc, c, k, reg):
        rows = slice(c * ACC_ROWS, (c + 1) * ACC_ROWS)
        kc = slice(k * MXU_TILE, (k + 1) * MXU_TILE)
        lhs = lhs_ref[rows, kc]
        for q in range(MXU_COUNT):
            pltpu.matmul_push_rhs(rhs_ref[kc, q * MXU_TILE:(q + 1) * MXU_TILE], staging_register=reg, mxu_index=q)
        for q in range(MXU_COUNT):
            pltpu.matmul_acc_lhs(acc, lhs, q, load_staged_rhs=reg)

    def stage(h_w, h_r, g_w, g_r, pre, gating, out):
        pre_groups = [("pre", c, k) for c in range(eb // ACC_ROWS) for k in range(d // MXU_TILE)] if pre else []
        out_groups = [("out", c, k) for c in range(d // ACC_ROWS) for k in range(eb // MXU_TILE)] if out else []
        if pre and out:
            order = [g for pair in zip(pre_groups, out_groups) for g in pair]
        else:
            order = pre_groups + out_groups
        units = [(lt, ii) for lt in range(n_lt) for ii in range(n_i)] if gating else []
        per_group = -(-len(units) // len(order))
        for gi, (kind, c, k) in enumerate(order):
            rows = slice(c * ACC_ROWS, (c + 1) * ACC_ROWS)
            if kind == "pre":
                mxu_group(u_ref, hnt_ref, ACC_PRE, c, k, gi % 2)
                if k == d // MXU_TILE - 1:
                    for q in range(MXU_COUNT):
                        h_w[rows, q * MXU_TILE:(q + 1) * MXU_TILE] = pltpu.matmul_pop(
                            ACC_PRE, (ACC_ROWS, MXU_TILE), F32, q)
            else:
                mxu_group(vt_ref, g_r, ACC_OUT, c, k, gi % 2)
                if k == eb // MXU_TILE - 1:
                    for q in range(MXU_COUNT):
                        yt_ref[rows, q * MXU_TILE:(q + 1) * MXU_TILE] += pltpu.matmul_pop(
                            ACC_OUT, (ACC_ROWS, MXU_TILE), F32, q)
            for lt, ii in units[gi * per_group:(gi + 1) * per_group]:
                gate(lt, ii, h_r, g_w)

    last = n_e + 1

    @pl.when(s == 0)
    def _first():
        stage(h0_scr, None, None, None, True, False, False)

    @pl.when((s > 0) & (s < last) & (s % 2 == 0))
    def _even():
        stage(h0_scr, h1_scr, g1_scr, g0_scr, True, True, True)

    @pl.when((s > 0) & (s < last) & (s % 2 == 1))
    def _odd():
        stage(h1_scr, h0_scr, g0_scr, g1_scr, True, True, True)

    @pl.when((s == last) & (s % 2 == 0))
    def _last_even():
        stage(None, None, None, g0_scr, False, False, True)

    @pl.when((s == last) & (s % 2 == 1))
    def _last_odd():
        stage(None, None, None, g1_scr, False, False, True)


def _peer_dense(sc, hnt, u_bf16, vt_bf16, tb, eb):
    _, nhc, nk, _ = sc.shape
    d, t = hnt.shape
    n_exp = u_bf16.shape[0]
    n_lt = tb // LANES
    assert eb == SUBLANES * nk and n_exp == nk * nk and t % tb == 0 and tb % LANES == 0
    sel = pltpu.VMEM((n_lt, PEER_HEADS, nk, LANES), F32)
    pre = pltpu.VMEM((eb, tb + LANES), F32)
    gated = pltpu.VMEM((eb, tb), BF16)
    n_e = n_exp // eb
    return pl.pallas_call(
        _peer_dense_kernel,
        grid=(t // tb, n_e + 2),
        in_specs=[pl.BlockSpec((n_lt, nhc, nk, LANES), lambda i, s: (i, 0, 0, 0)),
                  pl.BlockSpec((d, tb), lambda i, s: (0, i)),
                  pl.BlockSpec((eb, d), lambda i, s: (jnp.minimum(s, n_e - 1), 0)),
                  pl.BlockSpec((d, eb), lambda i, s: (0, jnp.clip(s - 2, 0, n_e - 1)))],
        out_specs=pl.BlockSpec((d, tb), lambda i, s: (0, i)),
        out_shape=jax.ShapeDtypeStruct((d, t), F32),
        scratch_shapes=[sel, sel, sel, pre, pre, gated, gated],
        compiler_params=_cparams("parallel", "arbitrary"),
        name="peer_dense",
    )(sc, hnt, u_bf16, vt_bf16)


def _final_kernel(h_ref, yt_ref, nw_ref, o_ref):
    y = h_ref[...] + yt_ref[...].T
    o_ref[...] = y * lax.rsqrt(jnp.mean(y * y, axis=-1, keepdims=True) + EPS) * nw_ref[...]


def _final(h2d, yt, norm_w, tm):
    t, d = h2d.shape
    return pl.pallas_call(
        _final_kernel,
        grid=(t // tm,),
        in_specs=[pl.BlockSpec((tm, d), lambda i: (i, 0)),
                  pl.BlockSpec((d, tm), lambda i: (0, i)),
                  pl.BlockSpec((1, d), lambda i: (0, 0))],
        out_specs=pl.BlockSpec((tm, d), lambda i: (i, 0)),
        out_shape=jax.ShapeDtypeStruct((t, d), F32),
        compiler_params=_cparams("parallel"),
        name="final_norm",
    )(h2d, yt, norm_w.reshape(1, d))


TOKEN_BLOCK = 512
EXPERT_BLOCK = 1024
SAMPLE_T_PAD = 16
SAMPLE_K_PAD = 256


def _ffn(x2d, attn, oh, w, tb):
    h, hnt = _out_proj(x2d, attn, oh, w['wo'], w['norm_ffn'], tb)
    sc = _peer_scores(hnt, w['wqt'], w['keys'], tb)
    yt = _peer_dense(sc, hnt, w['u'], w['vt'], tb, EXPERT_BLOCK)
    return _final(h, yt, w['norm_final'], tb)


def kernel(x_prompt, x_sample, cache_k_win, cache_v_win, state_hgrn, norm_mix_w, w_in, attn_sinks,
           rel_bias_table, hg_lb, hg_norm_w, w_o, norm_ffn_w, peer_w_q, peer_sub_keys, peer_u, peer_v,
           norm_final_w):
    bsz, seq, d = x_prompt.shape
    dbsz, dseq, _ = x_sample.shape
    aw = ATTN_HEADS * HEAD_DIM
    kw = ATTN_KV_HEADS * HEAD_DIM
    hw = hg_norm_w.shape[1]
    wb = cache_k_win.shape[2]

    wi = w_in[0]
    w_in_r = jnp.concatenate([wi[:, :aw], wi[:, aw + 2 * kw:], wi[:, aw:aw + 2 * kw]], axis=1).astype(BF16)
    col_k = (aw + 4 * hw) // kw
    col_v = col_k + 1
    hg_cols = (1, 2, 3, 4)
    lb = jax.nn.softmax(hg_lb.astype(F32), axis=0)[0]
    nhc = PEER_HEADS * 2
    w = {
        'wo': w_o[0].astype(BF16),
        'norm_ffn': norm_ffn_w[0],
        'wqt': peer_w_q[0].T.astype(BF16),
        'keys': peer_sub_keys[0].reshape(nhc, peer_sub_keys.shape[3], peer_sub_keys.shape[4]).astype(BF16),
        'u': peer_u[0].astype(BF16),
        'vt': peer_v[0].T.astype(BF16),
        'norm_final': norm_final_w,
    }
    sinks = attn_sinks[0].astype(F32)

    proj_p = _in_proj(x_prompt.reshape(bsz * seq, d), norm_mix_w[0], w_in_r, TOKEN_BLOCK)
    proj_p3 = proj_p.reshape(bsz, seq, -1)
    blk = WINDOW
    dist_p = (jnp.arange(blk)[:, None] + blk) - jnp.arange(2 * blk)[None, :]
    bias_p = _masked_bias(rel_bias_table, blk, 2 * blk, blk, (dist_p >= 0) & (dist_p <= WINDOW))
    attn_p = _swa_prompt(proj_p3, sinks, bias_p, 0, col_k, col_v)
    oh_p, st_p = _hgrn(proj_p3, lb, hg_norm_w[0], None, hg_cols, HG_CHUNK, HG_CHUNK, bsz)
    y_p = _ffn(x_prompt.reshape(bsz * seq, d), attn_p.reshape(bsz * seq, aw),
               oh_p.reshape(bsz * seq, hw), w, TOKEN_BLOCK)
    k_off = aw + 4 * hw
    wp = min(WINDOW, seq)
    k_win_p = proj_p3[:, seq - wp:, k_off:k_off + kw].reshape(1, bsz, wp, ATTN_KV_HEADS, HEAD_DIM)
    v_win_p = proj_p3[:, seq - wp:, k_off + kw:k_off + 2 * kw].reshape(1, bsz, wp, ATTN_KV_HEADS, HEAD_DIM)

    tp = SAMPLE_T_PAD
    xs_pad = jnp.pad(x_sample, ((0, 0), (0, tp - dseq), (0, 0)))
    proj_s3 = _in_proj(xs_pad.reshape(dbsz * tp, d), norm_mix_w[0], w_in_r, TOKEN_BLOCK).reshape(dbsz, tp, -1)
    k_new = proj_s3[:, :dseq, k_off:k_off + kw]
    v_new = proj_s3[:, :dseq, k_off + kw:k_off + 2 * kw]
    kk = jnp.concatenate([cache_k_win[0].reshape(dbsz, wb, kw), k_new], axis=1)
    vv = jnp.concatenate([cache_v_win[0].reshape(dbsz, wb, kw), v_new], axis=1)
    kpad = SAMPLE_K_PAD - (wb + dseq)
    kk_pad = jnp.pad(kk, ((0, 0), (0, kpad), (0, 0)))
    vv_pad = jnp.pad(vv, ((0, 0), (0, kpad), (0, 0)))
    dist_s = (wb + jnp.arange(tp))[:, None] - jnp.arange(SAMPLE_K_PAD)[None, :]
    mask_s = ((dist_s >= 0) & (dist_s <= WINDOW) & (jnp.arange(tp)[:, None] < dseq)
              & (jnp.arange(SAMPLE_K_PAD)[None, :] < wb + dseq))
    bias_s = _masked_bias(rel_bias_table, tp, SAMPLE_K_PAD, wb, mask_s)
    attn_s = _swa_sample(proj_s3, kk_pad, vv_pad, sinks, bias_s, 0, 16)
    oh_s, st_s = _hgrn(proj_s3, lb, hg_norm_w[0], state_hgrn[0], hg_cols, tp, dseq, 8)
    y_s = _ffn(x_sample.reshape(dbsz * dseq, d), attn_s[:, :dseq].reshape(dbsz * dseq, aw),
               oh_s[:, :dseq].reshape(dbsz * dseq, hw), w, TOKEN_BLOCK)
    k_win_s = kk[:, dseq:].reshape(1, dbsz, wb, ATTN_KV_HEADS, HEAD_DIM)
    v_win_s = vv[:, dseq:].reshape(1, dbsz, wb, ATTN_KV_HEADS, HEAD_DIM)

    return (y_p.reshape(bsz, seq, d), y_s.reshape(dbsz, dseq, d), k_win_p, v_win_p, st_p[None],
            k_win_s, v_win_s, st_s[None])
```

```python
import functools
import math

import jax
import jax.numpy as jnp
from jax import lax
from jax.experimental import pallas as pl
from jax.experimental.pallas import tpu as pltpu

F32 = jnp.float32
BF16 = jnp.bfloat16

EPS = 1e-6
NEG = -1e30

ATTN_HEADS = 8
ATTN_KV_HEADS = 2
HEAD_DIM = 64
WINDOW = 128
REL_BUCKETS = 32
HG_HEADS = 4
HG_CHUNK = 64
PEER_HEADS = 8
PEER_TOPK = 16

LANES = 128
SUBLANES = 8
VMEM_LIMIT = 56 * 1024 * 1024


def _cparams(*sem):
    return pltpu.CompilerParams(dimension_semantics=sem, vmem_limit_bytes=VMEM_LIMIT)


def _nt(a, b):
    return lax.dot_general(a, b, (((1,), (1,)), ((), ())), preferred_element_type=F32)


def _tn(a, b):
    return lax.dot_general(a, b, (((0,), (0,)), ((), ())), preferred_element_type=F32)


def _dot(a, b):
    return jnp.dot(a, b, preferred_element_type=F32)


def _sigmoid(x):
    return 1.0 / (1.0 + jnp.exp(-x))


def _in_proj_kernel(x_ref, nw_ref, w_ref, o_ref):
    x = x_ref[...]
    xn = x * lax.rsqrt(jnp.mean(x * x, axis=-1, keepdims=True) + EPS) * nw_ref[...]
    o_ref[...] = _dot(xn.astype(BF16), w_ref[...])


def _in_proj(x2d, norm_w, w_bf16, tm):
    t, d = x2d.shape
    n = w_bf16.shape[1]
    return pl.pallas_call(
        _in_proj_kernel,
        grid=(t // tm,),
        in_specs=[pl.BlockSpec((tm, d), lambda i: (i, 0)),
                  pl.BlockSpec((1, d), lambda i: (0, 0)),
                  pl.BlockSpec((d, n), lambda i: (0, 0))],
        out_specs=pl.BlockSpec((tm, n), lambda i: (i, 0)),
        out_shape=jax.ShapeDtypeStruct((t, n), F32),
        compiler_params=_cparams("parallel"),
        name="in_proj",
    )(x2d, norm_w.reshape(1, d), w_bf16)


def _t5_bucket(dist):
    n = jnp.maximum(dist, 0)
    max_exact = REL_BUCKETS // 2
    nf = jnp.maximum(n, 1).astype(F32)
    large = max_exact + (jnp.log(nf / max_exact) / math.log(WINDOW / max_exact)
                         * (REL_BUCKETS - max_exact)).astype(jnp.int32)
    large = jnp.minimum(large, REL_BUCKETS - 1)
    return jnp.where(n < max_exact, n, large)


def _masked_bias(table, n_q, n_k, offset, mask):
    h = table.shape[1]
    diag = jnp.arange(n_q + n_k - 1) - (n_k - 1) + offset
    per_diag = table.astype(F32)[_t5_bucket(diag)].T
    w = jnp.pad(per_diag[:, ::-1], ((0, 0), (0, 1)))
    p = n_q + n_k
    skew = jnp.tile(w, (1, n_q))[:, :n_q * (p - 1)].reshape(h, n_q, p - 1)
    return jnp.where(mask[None], skew[:, :, n_q - 1:n_q - 1 + n_k], NEG)


def _swa_prompt_kernel(sink_ref, q_ref, kp_ref, kc_ref, vp_ref, vc_ref, bias_ref, o_ref):
    first = pl.program_id(1) == 0
    scale = HEAD_DIM ** -0.5
    group = ATTN_HEADS // ATTN_KV_HEADS
    blk = q_ref.shape[1]
    assert 2 * HEAD_DIM == LANES and kp_ref.shape[2] == LANES and ATTN_KV_HEADS == 2 and group % 2 == 0
    kk = jnp.concatenate([kp_ref[0], kc_ref[0]], axis=0)
    vv = jnp.concatenate([vp_ref[0], vc_ref[0]], axis=0)
    low = lax.broadcasted_iota(jnp.int32, kk.shape, 1) < HEAD_DIM
    col = lax.broadcasted_iota(jnp.int32, (blk, 2 * blk), 1)
    no_prev = (col < blk) & first
    def halves(x, kvh):
        own = jnp.where(low if kvh == 0 else ~low, x, 0.0)
        other = pltpu.roll(own, HEAD_DIM, axis=1)
        lo, hi = (own, other) if kvh == 0 else (other, own)
        return lo.astype(BF16), hi.astype(BF16)

    k_half = [halves(kk, kvh) for kvh in range(ATTN_KV_HEADS)]
    v_half = [halves(vv, kvh) for kvh in range(ATTN_KV_HEADS)]
    heads = range(ATTN_HEADS)
    kv_of = lambda h: h // group
    s = [_nt(q_ref[0, :, (h // 2) * LANES:(h // 2 + 1) * LANES].astype(BF16), k_half[kv_of(h)][h % 2])
         for h in heads]
    s = [jnp.where(no_prev, NEG, s[h] * scale + bias_ref[h]) for h in heads]
    m = [jnp.maximum(jnp.max(s[h], axis=-1, keepdims=True), sink_ref[h]) for h in heads]
    p = [jnp.exp(s[h] - m[h]) for h in heads]
    den = [jnp.sum(p[h], axis=-1, keepdims=True) + jnp.exp(sink_ref[h] - m[h]) for h in heads]
    p = [(p[h] * (1.0 / den[h])).astype(BF16) for h in heads]
    for tile in range(ATTN_HEADS // 2):
        kvh = kv_of(2 * tile)
        o = _dot(p[2 * tile], v_half[kvh][0]) + _dot(p[2 * tile + 1], v_half[kvh][1])
        o_ref[0, :, tile * LANES:(tile + 1) * LANES] = o.astype(o_ref.dtype)


def _swa_prompt(proj3, sinks, bias, col_q, col_k, col_v):
    bsz, seq, _ = proj3.shape
    blk = WINDOW
    aw = ATTN_HEADS * HEAD_DIM
    kw = ATTN_KV_HEADS * HEAD_DIM
    prev = lambda b, n: (b, jnp.maximum(n - 1, 0))
    return pl.pallas_call(
        _swa_prompt_kernel,
        grid=(bsz, seq // blk),
        in_specs=[pl.BlockSpec(memory_space=pltpu.SMEM),
                  pl.BlockSpec((1, blk, aw), lambda b, n: (b, n, col_q)),
                  pl.BlockSpec((1, blk, kw), lambda b, n: prev(b, n) + (col_k,)),
                  pl.BlockSpec((1, blk, kw), lambda b, n: (b, n, col_k)),
                  pl.BlockSpec((1, blk, kw), lambda b, n: prev(b, n) + (col_v,)),
                  pl.BlockSpec((1, blk, kw), lambda b, n: (b, n, col_v)),
                  pl.BlockSpec((ATTN_HEADS, blk, 2 * blk), lambda b, n: (0, 0, 0))],
        out_specs=pl.BlockSpec((1, blk, aw), lambda b, n: (b, n, 0)),
        out_shape=jax.ShapeDtypeStruct((bsz, seq, aw), BF16),
        compiler_params=_cparams("parallel", "arbitrary"),
        name="swa_prompt",
    )(sinks, proj3, proj3, proj3, proj3, proj3, bias)


def _swa_sample_kernel(sink_ref, q_ref, kk_ref, vv_ref, bias_ref, o_ref):
    scale = HEAD_DIM ** -0.5
    group = ATTN_HEADS // ATTN_KV_HEADS
    for h in range(ATTN_HEADS):
        kv = (h // group) * HEAD_DIM
        qh = q_ref[:, :, h * HEAD_DIM:(h + 1) * HEAD_DIM].astype(BF16)
        kh = kk_ref[:, :, kv:kv + HEAD_DIM].astype(BF16)
        vh = vv_ref[:, :, kv:kv + HEAD_DIM].astype(BF16)
        s = jnp.einsum('bqd,bkd->bqk', qh, kh, preferred_element_type=F32) * scale + bias_ref[h][None]
        sink = sink_ref[h]
        m = jnp.maximum(jnp.max(s, axis=-1, keepdims=True), sink)
        p = jnp.exp(s - m)
        den = jnp.sum(p, axis=-1, keepdims=True) + jnp.exp(sink - m)
        o = jnp.einsum('bqk,bkd->bqd', p.astype(BF16), vh, preferred_element_type=F32) / den
        o_ref[:, :, h * HEAD_DIM:(h + 1) * HEAD_DIM] = o.astype(o_ref.dtype)


def _swa_sample(proj3, kk, vv, sinks, bias, col_q, bb):
    bsz, tp, _ = proj3.shape
    kp = kk.shape[1]
    aw = ATTN_HEADS * HEAD_DIM
    kw = ATTN_KV_HEADS * HEAD_DIM
    return pl.pallas_call(
        _swa_sample_kernel,
        grid=(bsz // bb,),
        in_specs=[pl.BlockSpec(memory_space=pltpu.SMEM),
                  pl.BlockSpec((bb, tp, aw), lambda b: (b, 0, col_q)),
                  pl.BlockSpec((bb, kp, kw), lambda b: (b, 0, 0)),
                  pl.BlockSpec((bb, kp, kw), lambda b: (b, 0, 0)),
                  pl.BlockSpec((ATTN_HEADS, tp, kp), lambda b: (0, 0, 0))],
        out_specs=pl.BlockSpec((bb, tp, aw), lambda b: (b, 0, 0)),
        out_shape=jax.ShapeDtypeStruct((bsz, tp, aw), BF16),
        compiler_params=_cparams("parallel"),
        name="swa_sample",
    )(sinks, proj3, kk, vv, bias)


def _split3(x):
    hi = x.astype(BF16)
    r = x - hi.astype(F32)
    mid = r.astype(BF16)
    lo = (r - mid.astype(F32)).astype(BF16)
    return hi, mid, lo


def _hgrn_kernel(*refs, t_valid, has_state):
    if has_state:
        q_ref, f_ref, i_ref, g_ref, lb_ref, nw_ref, s0_ref, o_ref, s_ref, st_scr = refs
    else:
        q_ref, f_ref, i_ref, g_ref, lb_ref, nw_ref, o_ref, s_ref, st_scr = refs
    bb, chunk, width = q_ref.shape
    dk = width // HG_HEADS
    c = pl.program_id(1)

    @pl.when(c == 0)
    def _init():
        if has_state:
            def load(b, carry):
                for h in range(HG_HEADS):
                    st_scr[b, h] = s0_ref[b, h].T
                return carry
            lax.fori_loop(0, bb, load, 0)
        else:
            st_scr[...] = jnp.zeros_like(st_scr)

    row = lax.broadcasted_iota(jnp.int32, (chunk, chunk), 0)
    col = lax.broadcasted_iota(jnp.int32, (chunk, chunk), 1)
    causal = row >= col
    tri = jnp.where(causal, 1.0, 0.0).astype(BF16)
    valid = lax.broadcasted_iota(jnp.int32, (chunk, width), 0) < t_valid
    mid_row = chunk // 2

    def body(b, carry):
        qx = q_ref[b]
        q = qx * _sigmoid(qx)
        lb = lb_ref[...]
        f = lb + (1.0 - lb) * _sigmoid(f_ref[b])
        k = 1.0 - f
        lg = jnp.log(f)
        if t_valid < chunk:
            k = jnp.where(valid, k, 0.0)
            lg = jnp.where(valid, lg, 0.0)
        v = i_ref[b].astype(BF16)
        cum = sum(_dot(tri, part) for part in _split3(lg))
        cum_mid = cum[mid_row:mid_row + 1, :]
        cum_last = cum[chunk - 1:chunk, :]
        qt = (q * jnp.exp(cum - cum_mid)).astype(BF16)
        kt = (k * jnp.exp(cum_mid - cum)).astype(BF16)
        qe = (q * jnp.exp(cum)).astype(BF16)
        kd = (k * jnp.exp(cum_last - cum)).astype(BF16)
        decay = jnp.exp(cum_last)
        heads = range(HG_HEADS)
        sl = [slice(h * dk, (h + 1) * dk) for h in heads]
        st = [st_scr[b, h] for h in heads]
        a = [_nt(qt[:, sl[h]], kt[:, sl[h]]) for h in heads]
        inter = [_nt(qe[:, sl[h]], st[h].astype(BF16)) for h in heads]
        upd = [_tn(v[:, sl[h]], kd[:, sl[h]]) for h in heads]
        for h in heads:
            st_scr[b, h] = st[h] * decay[:, sl[h]] + upd[h]
        a = [jnp.where(causal, a[h], 0.0).astype(BF16) for h in heads]
        o = [_dot(a[h], v[:, sl[h]]) + inter[h] for h in heads]
        outs = [o[h] * lax.rsqrt(jnp.mean(o[h] * o[h], axis=-1, keepdims=True) + EPS) for h in heads]
        gx = g_ref[b]
        o = jnp.concatenate(outs, axis=1) * nw_ref[...] * (gx * _sigmoid(gx))
        o_ref[b] = o.astype(o_ref.dtype)
        return carry

    lax.fori_loop(0, bb, body, 0, unroll=2)

    @pl.when(c == pl.num_programs(1) - 1)
    def _final():
        def store(b, carry):
            for h in range(HG_HEADS):
                s_ref[b, h] = st_scr[b, h].T
            return carry
        lax.fori_loop(0, bb, store, 0)


def _hgrn(proj3, lb, norm_w, s0, cols, chunk, t_valid, bb):
    bsz, t, _ = proj3.shape
    width = lb.shape[0]
    dk = width // HG_HEADS
    has_state = s0 is not None
    spec = lambda cb: pl.BlockSpec((bb, chunk, width), lambda b, c: (b, c, cb))
    vec = pl.BlockSpec((1, width), lambda b, c: (0, 0))
    st_spec = pl.BlockSpec((bb, HG_HEADS, dk, dk), lambda b, c: (b, 0, 0, 0))
    in_specs = [spec(cols[0]), spec(cols[1]), spec(cols[2]), spec(cols[3]), vec, vec]
    args = [proj3, proj3, proj3, proj3, lb.reshape(1, width), norm_w.reshape(1, width)]
    if has_state:
        in_specs.append(st_spec)
        args.append(s0)
    return pl.pallas_call(
        functools.partial(_hgrn_kernel, t_valid=t_valid, has_state=has_state),
        grid=(bsz // bb, t // chunk),
        in_specs=in_specs,
        out_specs=[pl.BlockSpec((bb, chunk, width), lambda b, c: (b, c, 0)), st_spec],
        out_shape=[jax.ShapeDtypeStruct((bsz, t, width), BF16),
                   jax.ShapeDtypeStruct((bsz, HG_HEADS, dk, dk), F32)],
        scratch_shapes=[pltpu.VMEM((bb, HG_HEADS, dk, dk), F32)],
        compiler_params=_cparams("parallel", "arbitrary"),
        name="hgrn_state" if has_state else "hgrn_prompt",
    )(*args)


def _out_proj_kernel(x_ref, a_ref, oh_ref, wo_ref, nw_ref, h_ref, hnt_ref):
    aw = a_ref.shape[1]
    mix = _dot(a_ref[...], wo_ref[:aw, :]) + _dot(oh_ref[...], wo_ref[aw:, :])
    h = x_ref[...] + mix
    h_ref[...] = h
    hn = h * lax.rsqrt(jnp.mean(h * h, axis=-1, keepdims=True) + EPS) * nw_ref[...]
    hnt_ref[...] = hn.T.astype(hnt_ref.dtype)


def _out_proj(x2d, attn, oh, wo_bf16, norm_w, tm):
    t, d = x2d.shape
    aw, hw = attn.shape[1], oh.shape[1]
    return pl.pallas_call(
        _out_proj_kernel,
        grid=(t // tm,),
        in_specs=[pl.BlockSpec((tm, d), lambda i: (i, 0)),
                  pl.BlockSpec((tm, aw), lambda i: (i, 0)),
                  pl.BlockSpec((tm, hw), lambda i: (i, 0)),
                  pl.BlockSpec((aw + hw, d), lambda i: (0, 0)),
                  pl.BlockSpec((1, d), lambda i: (0, 0))],
        out_specs=[pl.BlockSpec((tm, d), lambda i: (i, 0)),
                   pl.BlockSpec((d, tm), lambda i: (0, i))],
        out_shape=[jax.ShapeDtypeStruct((t, d), F32),
                   jax.ShapeDtypeStruct((d, t), BF16)],
        compiler_params=_cparams("parallel"),
        name="out_proj",
    )(x2d, attn, oh, wo_bf16, norm_w.reshape(1, d))


def _peer_scores_kernel(hnt_ref, wqt_ref, keys_ref, sc_ref):
    qt = _dot(wqt_ref[...], hnt_ref[...])
    half = keys_ref.shape[2]
    for hc in range(keys_ref.shape[0]):
        s = _dot(keys_ref[hc], qt[hc * half:(hc + 1) * half, :].astype(BF16))
        for lt in range(sc_ref.shape[0]):
            sc_ref[lt, hc] = s[:, lt * LANES:(lt + 1) * LANES]


def _peer_scores(hnt, wqt_bf16, keys_bf16, tb):
    d, t = hnt.shape
    nhc, nk, half = keys_bf16.shape
    return pl.pallas_call(
        _peer_scores_kernel,
        grid=(t // tb,),
        in_specs=[pl.BlockSpec((d, tb), lambda i: (0, i)),
                  pl.BlockSpec((nhc * half, d), lambda i: (0, 0)),
                  pl.BlockSpec((nhc, nk, half), lambda i: (0, 0, 0))],
        out_specs=pl.BlockSpec((tb // LANES, nhc, nk, LANES), lambda i: (i, 0, 0, 0)),
        out_shape=jax.ShapeDtypeStruct((t // LANES, nhc, nk, LANES), F32),
        compiler_params=_cparams("parallel"),
        name="peer_scores",
    )(hnt, wqt_bf16, keys_bf16)


def _sort16_pairs():
    def merge(lo, hi, r):
        step = r * 2
        if step < hi - lo:
            yield from merge(lo, hi, step)
            yield from merge(lo + r, hi, step)
            yield from [(i, i + r) for i in range(lo + r, hi - r, step)]
        else:
            yield (lo, lo + r)

    def sort(lo, hi):
        if hi - lo >= 1:
            mid = lo + (hi - lo) // 2
            yield from sort(lo, mid)
            yield from sort(mid + 1, hi)
            yield from merge(lo, hi, 1)

    return tuple(sort(0, PEER_TOPK - 1))


_SORT16 = _sort16_pairs()


def _bitonic_to_sorted(z):
    z = list(z)
    d = PEER_TOPK // 2
    while d >= 1:
        for i in range(PEER_TOPK):
            if i & d == 0:
                hi, lo = jnp.maximum(z[i], z[i + d]), jnp.minimum(z[i], z[i + d])
                z[i], z[i + d] = hi, lo
        d //= 2
    return z


def _merge_bitonic(top, other):
    z = list(top)
    m = len(other)
    for r in range(PEER_TOPK - m, PEER_TOPK):
        z[r] = jnp.maximum(top[r], other[PEER_TOPK - 1 - r])
    return z


def _top16_desc(x):
    n = x.shape[0] // SUBLANES
    xs = [x[g * SUBLANES:(g + 1) * SUBLANES, :] for g in range(n)]
    for i, j in _SORT16:
        xs[i], xs[j] = jnp.maximum(xs[i], xs[j]), jnp.minimum(xs[i], xs[j])
    shift = SUBLANES // 2
    while shift >= 1:
        ys = [pltpu.roll(v, shift, axis=0) for v in xs]
        xs = _bitonic_to_sorted(_merge_bitonic(xs, ys))
        shift //= 2
    return xs


def _peer_select(sc_ref, thr_scr, pw_scr, q_scr, h, lt):
    s0 = sc_ref[lt, 2 * h]
    s1 = sc_ref[lt, 2 * h + 1]
    a = _top16_desc(s0)
    b = _top16_desc(s1)
    lists = [[a[r] + b[c] for c in range(PEER_TOPK // (r + 1))] for r in range(SUBLANES)]
    lists.append([a[r] + b[0] for r in range(SUBLANES, PEER_TOPK)])
    top = lists[0]
    for other in lists[1:-1]:
        top = _bitonic_to_sorted(_merge_bitonic(top, other))
    z = _merge_bitonic(top, lists[-1])
    tau = functools.reduce(jnp.minimum, z)
    best = a[0] + b[0]
    zsum = jnp.zeros_like(tau)
    inf = jnp.full_like(tau, jnp.inf)
    thr_rank = []
    for r, cand in enumerate(lists):
        hits = [v >= tau for v in cand]
        for v, hit in zip(cand, hits):
            zsum = zsum + jnp.where(hit, jnp.exp(v - best), 0.0)
        if r < SUBLANES:
            t = inf
            for c, hit in enumerate(hits):
                t = jnp.where(hit, b[c], t)
            thr_rank.append(t)
        else:
            thr_rank.extend(jnp.where(hit, b[0], inf) for hit in hits)
    inv = 1.0 / zsum
    for g in range(s0.shape[0] // SUBLANES):
        rows = slice(g * SUBLANES, (g + 1) * SUBLANES)
        x0 = s0[rows, :]
        thr = jnp.full_like(x0, jnp.inf)
        for r in range(PEER_TOPK):
            thr = jnp.where(x0 == a[r], thr_rank[r], thr)
        thr_scr[lt, h, rows, :] = thr
        pw_scr[lt, h, rows, :] = jnp.exp(x0 - a[0]) * inv
        q_scr[lt, h, rows, :] = jnp.exp(s1[rows, :] - b[0])


MXU_TILE = 256
MXU_COUNT = 2
ACC_ROWS = 512
ACC_PIECE = 32
ACC_PRE = 0
ACC_OUT = ACC_ROWS // 4


def _peer_dense_kernel(sc_ref, hnt_ref, u_ref, vt_ref, yt_ref, thr_scr, pw_scr, q_scr,
                       h0_scr, h1_scr, g0_scr, g1_scr):
    s = pl.program_id(1)
    n_e = pl.num_programs(1) - 2
    d, tb = hnt_ref.shape
    n_lt, _, nk, _ = sc_ref.shape
    eb = u_ref.shape[0]
    n_i = eb // nk
    assert tb == MXU_COUNT * MXU_TILE and eb % ACC_ROWS == 0 and d % ACC_ROWS == 0

    @pl.when(s == 0)
    def _select():
        def body(it, carry):
            _peer_select(sc_ref, thr_scr, pw_scr, q_scr, it // n_lt, it % n_lt)
            return carry
        lax.fori_loop(0, PEER_HEADS * n_lt, body, 0)
        yt_ref[...] = jnp.zeros_like(yt_ref)
        g0_scr[...] = jnp.zeros_like(g0_scr)
        g1_scr[...] = jnp.zeros_like(g1_scr)

    i0 = pl.multiple_of(jnp.clip(s - 1, 0, n_e - 1) * n_i, SUBLANES)

    def gate_steps(lt, ii, h_r, g_w):
        lanes = slice(lt * LANES, (lt + 1) * LANES)
        rows = slice(ii * nk, (ii + 1) * nk)
        state = {"w": jnp.zeros((nk, LANES), F32)}

        def head(h):
            thr = thr_scr[lt, h, pl.ds(i0, n_i), :][ii:ii + 1, :]
            pw = pw_scr[lt, h, pl.ds(i0, n_i), :][ii:ii + 1, :]
            state["w"] = state["w"] + jnp.where(sc_ref[lt, 2 * h + 1] >= thr, q_scr[lt, h], 0.0) * pw
            if h == PEER_HEADS - 1:
                x = h_r[rows, lanes]
                act = 0.5 * x * (1.0 + lax.erf(x * (2.0 ** -0.5)))
                g_w[rows, lanes] = (state["w"] * act).astype(g_w.dtype)

        return [functools.partial(head, h) for h in range(PEER_HEADS)]

    def mxu_steps(kind, c, k, reg, h_w, g_r):
        lhs_ref, rhs_ref, acc, n_k = ((u_ref, hnt_ref, ACC_PRE, d // MXU_TILE) if kind == "pre"
                                      else (vt_ref, g_r, ACC_OUT, eb // MXU_TILE))
        kc = slice(k * MXU_TILE, (k + 1) * MXU_TILE)

        def push():
            for q in range(MXU_COUNT):
                pltpu.matmul_push_rhs(rhs_ref[kc, q * MXU_TILE:(q + 1) * MXU_TILE], staging_register=reg,
                                      mxu_index=q)

        def piece(p):
            r0 = c * ACC_ROWS + p * ACC_PIECE
            lhs = lhs_ref[r0:r0 + ACC_PIECE, kc]
            for q in range(MXU_COUNT):
                pltpu.matmul_acc_lhs(acc + p * ACC_PIECE // 4, lhs, q, load_staged_rhs=reg if p == 0 else None)
            if p == ACC_ROWS // ACC_PIECE - 1 and k == n_k - 1:
                rows = slice(c * ACC_ROWS, (c + 1) * ACC_ROWS)
                for q in range(MXU_COUNT):
                    cols = slice(q * MXU_TILE, (q + 1) * MXU_TILE)
                    res = pltpu.matmul_pop(acc, (ACC_ROWS, MXU_TILE), F32, q)
                    if kind == "pre":
                        h_w[rows, cols] = res
                    else:
                        yt_ref[rows, cols] += res

        return push, [functools.partial(piece, p) for p in range(ACC_ROWS // ACC_PIECE)]

    def stage(h_w, h_r, g_w, g_r, pre, gating, out):
        pre_groups = [("pre", c, k) for c in range(eb // ACC_ROWS) for k in range(d // MXU_TILE)] if pre else []
        out_groups = [("out", c, k) for c in range(d // ACC_ROWS) for k in range(eb // MXU_TILE)] if out else []
        if pre and out:
            order = [g for pair in zip(pre_groups, out_groups) for g in pair]
        else:
            order = pre_groups + out_groups
        steps = [mxu_steps(kind, c, k, gi % 2, h_w, g_r) for gi, (kind, c, k) in enumerate(order)]
        mxu = [steps[0][0]]
        for gi, (_, pieces) in enumerate(steps):
            half = len(pieces) // 2
            mxu.extend(pieces[:half])
            if gi + 1 < len(steps):
                mxu.append(steps[gi + 1][0])
            mxu.extend(pieces[half:])
        vpu = [t for lt in range(n_lt) for ii in range(n_i) for t in gate_steps(lt, ii, h_r, g_w)] if gating else []
        im = iv = 0
        while im < len(mxu) or iv < len(vpu):
            if iv >= len(vpu) or (im < len(mxu) and im * len(vpu) <= iv * len(mxu)):
                mxu[im]()
                im += 1
            else:
                vpu[iv]()
                iv += 1

    last = n_e + 1

    @pl.when(s == 0)
    def _first():
        stage(h0_scr, None, None, None, True, False, False)

    @pl.when((s > 0) & (s < last) & (s % 2 == 0))
    def _even():
        stage(h0_scr, h1_scr, g1_scr, g0_scr, True, True, True)

    @pl.when((s > 0) & (s < last) & (s % 2 == 1))
    def _odd():
        stage(h1_scr, h0_scr, g0_scr, g1_scr, True, True, True)

    @pl.when((s == last) & (s % 2 == 0))
    def _last_even():
        stage(None, None, None, g0_scr, False, False, True)

    @pl.when((s == last) & (s % 2 == 1))
    def _last_odd():
        stage(None, None, None, g1_scr, False, False, True)


def _peer_dense(sc, hnt, u_bf16, vt_bf16, tb, eb):
    _, nhc, nk, _ = sc.shape
    d, t = hnt.shape
    n_exp = u_bf16.shape[0]
    n_lt = tb // LANES
    assert eb == SUBLANES * nk and n_exp == nk * nk and t % tb == 0 and tb % LANES == 0
    sel = pltpu.VMEM((n_lt, PEER_HEADS, nk, LANES), F32)
    pre = pltpu.VMEM((eb, tb + LANES), F32)
    gated = pltpu.VMEM((eb, tb), BF16)
    n_e = n_exp // eb
    return pl.pallas_call(
        _peer_dense_kernel,
        grid=(t // tb, n_e + 2),
        in_specs=[pl.BlockSpec((n_lt, nhc, nk, LANES), lambda i, s: (i, 0, 0, 0)),
                  pl.BlockSpec((d, tb), lambda i, s: (0, i)),
                  pl.BlockSpec((eb, d), lambda i, s: (jnp.minimum(s, n_e - 1), 0)),
                  pl.BlockSpec((d, eb), lambda i, s: (0, jnp.clip(s - 2, 0, n_e - 1)))],
        out_specs=pl.BlockSpec((d, tb), lambda i, s: (0, i)),
        out_shape=jax.ShapeDtypeStruct((d, t), F32),
        scratch_shapes=[sel, sel, sel, pre, pre, gated, gated],
        compiler_params=_cparams("parallel", "arbitrary"),
        name="peer_dense",
    )(sc, hnt, u_bf16, vt_bf16)


def _final_kernel(h_ref, yt_ref, nw_ref, o_ref):
    y = h_ref[...] + yt_ref[...].T
    o_ref[...] = y * lax.rsqrt(jnp.mean(y * y, axis=-1, keepdims=True) + EPS) * nw_ref[...]


def _final(h2d, yt, norm_w, tm):
    t, d = h2d.shape
    return pl.pallas_call(
        _final_kernel,
        grid=(t // tm,),
        in_specs=[pl.BlockSpec((tm, d), lambda i: (i, 0)),
                  pl.BlockSpec((d, tm), lambda i: (0, i)),
                  pl.BlockSpec((1, d), lambda i: (0, 0))],
        out_specs=pl.BlockSpec((tm, d), lambda i: (i, 0)),
        out_shape=jax.ShapeDtypeStruct((t, d), F32),
        compiler_params=_cparams("parallel"),
        name="final_norm",
    )(h2d, yt, norm_w.reshape(1, d))


TOKEN_BLOCK = 512
EXPERT_BLOCK = 1024
SAMPLE_T_PAD = 16
SAMPLE_K_PAD = 256


def _ffn(x2d, attn, oh, w, tb):
    h, hnt = _out_proj(x2d, attn, oh, w['wo'], w['norm_ffn'], tb)
    sc = _peer_scores(hnt, w['wqt'], w['keys'], tb)
    yt = _peer_dense(sc, hnt, w['u'], w['vt'], tb, EXPERT_BLOCK)
    return _final(h, yt, w['norm_final'], tb)


def kernel(x_prompt, x_sample, cache_k_win, cache_v_win, state_hgrn, norm_mix_w, w_in, attn_sinks,
           rel_bias_table, hg_lb, hg_norm_w, w_o, norm_ffn_w, peer_w_q, peer_sub_keys, peer_u, peer_v,
           norm_final_w):
    bsz, seq, d = x_prompt.shape
    dbsz, dseq, _ = x_sample.shape
    aw = ATTN_HEADS * HEAD_DIM
    kw = ATTN_KV_HEADS * HEAD_DIM
    hw = hg_norm_w.shape[1]
    wb = cache_k_win.shape[2]

    wi = w_in[0]
    w_in_r = jnp.concatenate([wi[:, :aw], wi[:, aw + 2 * kw:], wi[:, aw:aw + 2 * kw]], axis=1).astype(BF16)
    col_k = (aw + 4 * hw) // kw
    col_v = col_k + 1
    hg_cols = (1, 2, 3, 4)
    lb = jax.nn.softmax(hg_lb.astype(F32), axis=0)[0]
    nhc = PEER_HEADS * 2
    w = {
        'wo': w_o[0].astype(BF16),
        'norm_ffn': norm_ffn_w[0],
        'wqt': peer_w_q[0].T.astype(BF16),
        'keys': peer_sub_keys[0].reshape(nhc, peer_sub_keys.shape[3], peer_sub_keys.shape[4]).astype(BF16),
        'u': peer_u[0].astype(BF16),
        'vt': peer_v[0].T.astype(BF16),
        'norm_final': norm_final_w,
    }
    sinks = attn_sinks[0].astype(F32)

    proj_p = _in_proj(x_prompt.reshape(bsz * seq, d), norm_mix_w[0], w_in_r, TOKEN_BLOCK)
    proj_p3 = proj_p.reshape(bsz, seq, -1)
    blk = WINDOW
    dist_p = (jnp.arange(blk)[:, None] + blk) - jnp.arange(2 * blk)[None, :]
    bias_p = _masked_bias(rel_bias_table, blk, 2 * blk, blk, (dist_p >= 0) & (dist_p <= WINDOW))
    attn_p = _swa_prompt(proj_p3, sinks, bias_p, 0, col_k, col_v)
    oh_p, st_p = _hgrn(proj_p3, lb, hg_norm_w[0], None, hg_cols, HG_CHUNK, HG_CHUNK, bsz)
    y_p = _ffn(x_prompt.reshape(bsz * seq, d), attn_p.reshape(bsz * seq, aw),
               oh_p.reshape(bsz * seq, hw), w, TOKEN_BLOCK)
    k_off = aw + 4 * hw
    wp = min(WINDOW, seq)
    k_win_p = proj_p3[:, seq - wp:, k_off:k_off + kw].reshape(1, bsz, wp, ATTN_KV_HEADS, HEAD_DIM)
    v_win_p = proj_p3[:, seq - wp:, k_off + kw:k_off + 2 * kw].reshape(1, bsz, wp, ATTN_KV_HEADS, HEAD_DIM)

    tp = SAMPLE_T_PAD
    xs_pad = jnp.pad(x_sample, ((0, 0), (0, tp - dseq), (0, 0)))
    proj_s3 = _in_proj(xs_pad.reshape(dbsz * tp, d), norm_mix_w[0], w_in_r, TOKEN_BLOCK).reshape(dbsz, tp, -1)
    k_new = proj_s3[:, :dseq, k_off:k_off + kw]
    v_new = proj_s3[:, :dseq, k_off + kw:k_off + 2 * kw]
    kk = jnp.concatenate([cache_k_win[0].reshape(dbsz, wb, kw), k_new], axis=1)
    vv = jnp.concatenate([cache_v_win[0].reshape(dbsz, wb, kw), v_new], axis=1)
    kpad = SAMPLE_K_PAD - (wb + dseq)
    kk_pad = jnp.pad(kk, ((0, 0), (0, kpad), (0, 0)))
    vv_pad = jnp.pad(vv, ((0, 0), (0, kpad), (0, 0)))
    dist_s = (wb + jnp.arange(tp))[:, None] - jnp.arange(SAMPLE_K_PAD)[None, :]
    mask_s = ((dist_s >= 0) & (dist_s <= WINDOW) & (jnp.arange(tp)[:, None] < dseq)
              & (jnp.arange(SAMPLE_K_PAD)[None, :] < wb + dseq))
    bias_s = _masked_bias(rel_bias_table, tp, SAMPLE_K_PAD, wb, mask_s)
    attn_s = _swa_sample(proj_s3, kk_pad, vv_pad, sinks, bias_s, 0, 16)
    oh_s, st_s = _hgrn(proj_s3, lb, hg_norm_w[0], state_hgrn[0], hg_cols, tp, dseq, 8)
    y_s = _ffn(x_sample.reshape(dbsz * dseq, d), attn_s[:, :dseq].reshape(dbsz * dseq, aw),
               oh_s[:, :dseq].reshape(dbsz * dseq, hw), w, TOKEN_BLOCK)
    k_win_s = kk[:, dseq:].reshape(1, dbsz, wb, ATTN_KV_HEADS, HEAD_DIM)
    v_win_s = vv[:, dseq:].reshape(1, dbsz, wb, ATTN_KV_HEADS, HEAD_DIM)

    return (y_p.reshape(bsz, seq, d), y_s.reshape(dbsz, dseq, d), k_win_p, v_win_p, st_p[None],
            k_win_s, v_win_s, st_s[None])
```

```python
import functools
import math

import jax
import jax.numpy as jnp
from jax import lax
from jax.experimental import pallas as pl
from jax.experimental.pallas import tpu as pltpu

F32 = jnp.float32
BF16 = jnp.bfloat16

EPS = 1e-6
NEG = -1e30

ATTN_HEADS = 8
ATTN_KV_HEADS = 2
HEAD_DIM = 64
WINDOW = 128
REL_BUCKETS = 32
HG_HEADS = 4
HG_CHUNK = 64
PEER_HEADS = 8
PEER_TOPK = 16

LANES = 128
SUBLANES = 8
VMEM_LIMIT = 56 * 1024 * 1024


def _cparams(*sem):
    return pltpu.CompilerParams(dimension_semantics=sem, vmem_limit_bytes=VMEM_LIMIT)


def _nt(a, b):
    return lax.dot_general(a, b, (((1,), (1,)), ((), ())), preferred_element_type=F32)


def _tn(a, b):
    return lax.dot_general(a, b, (((0,), (0,)), ((), ())), preferred_element_type=F32)


def _dot(a, b):
    return jnp.dot(a, b, preferred_element_type=F32)


def _sigmoid(x):
    return 1.0 / (1.0 + jnp.exp(-x))


def _in_proj_kernel(x_ref, nw_ref, w_ref, o_ref):
    x = x_ref[...]
    xn = x * lax.rsqrt(jnp.mean(x * x, axis=-1, keepdims=True) + EPS) * nw_ref[...]
    o_ref[...] = _dot(xn.astype(BF16), w_ref[...])


def _in_proj(x2d, norm_w, w_bf16, tm):
    t, d = x2d.shape
    n = w_bf16.shape[1]
    return pl.pallas_call(
        _in_proj_kernel,
        grid=(t // tm,),
        in_specs=[pl.BlockSpec((tm, d), lambda i: (i, 0)),
                  pl.BlockSpec((1, d), lambda i: (0, 0)),
                  pl.BlockSpec((d, n), lambda i: (0, 0))],
        out_specs=pl.BlockSpec((tm, n), lambda i: (i, 0)),
        out_shape=jax.ShapeDtypeStruct((t, n), F32),
        compiler_params=_cparams("parallel"),
        name="in_proj",
    )(x2d, norm_w.reshape(1, d), w_bf16)


def _t5_bucket(dist):
    n = jnp.maximum(dist, 0)
    max_exact = REL_BUCKETS // 2
    nf = jnp.maximum(n, 1).astype(F32)
    large = max_exact + (jnp.log(nf / max_exact) / math.log(WINDOW / max_exact)
                         * (REL_BUCKETS - max_exact)).astype(jnp.int32)
    large = jnp.minimum(large, REL_BUCKETS - 1)
    return jnp.where(n < max_exact, n, large)


def _masked_bias(table, n_q, n_k, offset, mask):
    h = table.shape[1]
    diag = jnp.arange(n_q + n_k - 1) - (n_k - 1) + offset
    per_diag = table.astype(F32)[_t5_bucket(diag)].T
    w = jnp.pad(per_diag[:, ::-1], ((0, 0), (0, 1)))
    p = n_q + n_k
    skew = jnp.tile(w, (1, n_q))[:, :n_q * (p - 1)].reshape(h, n_q, p - 1)
    return jnp.where(mask[None], skew[:, :, n_q - 1:n_q - 1 + n_k], NEG)


def _swa_prompt_kernel(sink_ref, q_ref, kp_ref, kc_ref, vp_ref, vc_ref, bias_ref, o_ref):
    first = pl.program_id(1) == 0
    scale = HEAD_DIM ** -0.5
    group = ATTN_HEADS // ATTN_KV_HEADS
    blk = q_ref.shape[1]
    assert 2 * HEAD_DIM == LANES and kp_ref.shape[2] == LANES and ATTN_KV_HEADS == 2 and group % 2 == 0
    kk = jnp.concatenate([kp_ref[0], kc_ref[0]], axis=0)
    vv = jnp.concatenate([vp_ref[0], vc_ref[0]], axis=0)
    low = lax.broadcasted_iota(jnp.int32, kk.shape, 1) < HEAD_DIM
    col = lax.broadcasted_iota(jnp.int32, (blk, 2 * blk), 1)
    no_prev = (col < blk) & first
    def halves(x, kvh):
        own = jnp.where(low if kvh == 0 else ~low, x, 0.0)
        other = pltpu.roll(own, HEAD_DIM, axis=1)
        lo, hi = (own, other) if kvh == 0 else (other, own)
        return lo.astype(BF16), hi.astype(BF16)

    k_half = [halves(kk, kvh) for kvh in range(ATTN_KV_HEADS)]
    v_half = [halves(vv, kvh) for kvh in range(ATTN_KV_HEADS)]
    heads = range(ATTN_HEADS)
    kv_of = lambda h: h // group
    s = [_nt(q_ref[0, :, (h // 2) * LANES:(h // 2 + 1) * LANES].astype(BF16), k_half[kv_of(h)][h % 2])
         for h in heads]
    s = [jnp.where(no_prev, NEG, s[h] * scale + bias_ref[h]) for h in heads]
    m = [jnp.maximum(jnp.max(s[h], axis=-1, keepdims=True), sink_ref[h]) for h in heads]
    p = [jnp.exp(s[h] - m[h]) for h in heads]
    den = [jnp.sum(p[h], axis=-1, keepdims=True) + jnp.exp(sink_ref[h] - m[h]) for h in heads]
    p = [(p[h] * (1.0 / den[h])).astype(BF16) for h in heads]
    for tile in range(ATTN_HEADS // 2):
        kvh = kv_of(2 * tile)
        o = _dot(p[2 * tile], v_half[kvh][0]) + _dot(p[2 * tile + 1], v_half[kvh][1])
        o_ref[0, :, tile * LANES:(tile + 1) * LANES] = o.astype(o_ref.dtype)


def _swa_prompt(proj3, sinks, bias, col_q, col_k, col_v):
    bsz, seq, _ = proj3.shape
    blk = WINDOW
    aw = ATTN_HEADS * HEAD_DIM
    kw = ATTN_KV_HEADS * HEAD_DIM
    prev = lambda b, n: (b, jnp.maximum(n - 1, 0))
    return pl.pallas_call(
        _swa_prompt_kernel,
        grid=(bsz, seq // blk),
        in_specs=[pl.BlockSpec(memory_space=pltpu.SMEM),
                  pl.BlockSpec((1, blk, aw), lambda b, n: (b, n, col_q)),
                  pl.BlockSpec((1, blk, kw), lambda b, n: prev(b, n) + (col_k,)),
                  pl.BlockSpec((1, blk, kw), lambda b, n: (b, n, col_k)),
                  pl.BlockSpec((1, blk, kw), lambda b, n: prev(b, n) + (col_v,)),
                  pl.BlockSpec((1, blk, kw), lambda b, n: (b, n, col_v)),
                  pl.BlockSpec((ATTN_HEADS, blk, 2 * blk), lambda b, n: (0, 0, 0))],
        out_specs=pl.BlockSpec((1, blk, aw), lambda b, n: (b, n, 0)),
        out_shape=jax.ShapeDtypeStruct((bsz, seq, aw), BF16),
        compiler_params=_cparams("parallel", "arbitrary"),
        name="swa_prompt",
    )(sinks, proj3, proj3, proj3, proj3, proj3, bias)


def _swa_sample_kernel(sink_ref, q_ref, kk_ref, vv_ref, bias_ref, o_ref):
    scale = HEAD_DIM ** -0.5
    group = ATTN_HEADS // ATTN_KV_HEADS
    for h in range(ATTN_HEADS):
        kv = (h // group) * HEAD_DIM
        qh = q_ref[:, :, h * HEAD_DIM:(h + 1) * HEAD_DIM].astype(BF16)
        kh = kk_ref[:, :, kv:kv + HEAD_DIM].astype(BF16)
        vh = vv_ref[:, :, kv:kv + HEAD_DIM].astype(BF16)
        s = jnp.einsum('bqd,bkd->bqk', qh, kh, preferred_element_type=F32) * scale + bias_ref[h][None]
        sink = sink_ref[h]
        m = jnp.maximum(jnp.max(s, axis=-1, keepdims=True), sink)
        p = jnp.exp(s - m)
        den = jnp.sum(p, axis=-1, keepdims=True) + jnp.exp(sink - m)
        o = jnp.einsum('bqk,bkd->bqd', p.astype(BF16), vh, preferred_element_type=F32) / den
        o_ref[:, :, h * HEAD_DIM:(h + 1) * HEAD_DIM] = o.astype(o_ref.dtype)


def _swa_sample(proj3, kk, vv, sinks, bias, col_q, bb):
    bsz, tp, _ = proj3.shape
    kp = kk.shape[1]
    aw = ATTN_HEADS * HEAD_DIM
    kw = ATTN_KV_HEADS * HEAD_DIM
    return pl.pallas_call(
        _swa_sample_kernel,
        grid=(bsz // bb,),
        in_specs=[pl.BlockSpec(memory_space=pltpu.SMEM),
                  pl.BlockSpec((bb, tp, aw), lambda b: (b, 0, col_q)),
                  pl.BlockSpec((bb, kp, kw), lambda b: (b, 0, 0)),
                  pl.BlockSpec((bb, kp, kw), lambda b: (b, 0, 0)),
                  pl.BlockSpec((ATTN_HEADS, tp, kp), lambda b: (0, 0, 0))],
        out_specs=pl.BlockSpec((bb, tp, aw), lambda b: (b, 0, 0)),
        out_shape=jax.ShapeDtypeStruct((bsz, tp, aw), BF16),
        compiler_params=_cparams("parallel"),
        name="swa_sample",
    )(sinks, proj3, kk, vv, bias)


def _split3(x):
    hi = x.astype(BF16)
    r = x - hi.astype(F32)
    mid = r.astype(BF16)
    lo = (r - mid.astype(F32)).astype(BF16)
    return hi, mid, lo


def _hgrn_kernel(*refs, t_valid, has_state):
    if has_state:
        q_ref, f_ref, i_ref, g_ref, lb_ref, nw_ref, s0_ref, o_ref, s_ref, st_scr = refs
    else:
        q_ref, f_ref, i_ref, g_ref, lb_ref, nw_ref, o_ref, s_ref, st_scr = refs
    bb, chunk, width = q_ref.shape
    dk = width // HG_HEADS
    c = pl.program_id(1)

    @pl.when(c == 0)
    def _init():
        if has_state:
            def load(b, carry):
                for h in range(HG_HEADS):
                    st_scr[b, h] = s0_ref[b, h].T
                return carry
            lax.fori_loop(0, bb, load, 0)
        else:
            st_scr[...] = jnp.zeros_like(st_scr)

    row = lax.broadcasted_iota(jnp.int32, (chunk, chunk), 0)
    col = lax.broadcasted_iota(jnp.int32, (chunk, chunk), 1)
    causal = row >= col
    tri = jnp.where(causal, 1.0, 0.0).astype(BF16)
    valid = lax.broadcasted_iota(jnp.int32, (chunk, width), 0) < t_valid
    mid_row = chunk // 2

    def body(b, carry):
        qx = q_ref[b]
        q = qx * _sigmoid(qx)
        lb = lb_ref[...]
        f = lb + (1.0 - lb) * _sigmoid(f_ref[b])
        k = 1.0 - f
        lg = jnp.log(f)
        if t_valid < chunk:
            k = jnp.where(valid, k, 0.0)
            lg = jnp.where(valid, lg, 0.0)
        v = i_ref[b].astype(BF16)
        cum = sum(_dot(tri, part) for part in _split3(lg))
        cum_mid = cum[mid_row:mid_row + 1, :]
        cum_last = cum[chunk - 1:chunk, :]
        qt = (q * jnp.exp(cum - cum_mid)).astype(BF16)
        kt = (k * jnp.exp(cum_mid - cum)).astype(BF16)
        qe = (q * jnp.exp(cum)).astype(BF16)
        kd = (k * jnp.exp(cum_last - cum)).astype(BF16)
        decay = jnp.exp(cum_last)
        heads = range(HG_HEADS)
        sl = [slice(h * dk, (h + 1) * dk) for h in heads]
        st = [st_scr[b, h] for h in heads]
        a = [_nt(qt[:, sl[h]], kt[:, sl[h]]) for h in heads]
        inter = [_nt(qe[:, sl[h]], st[h].astype(BF16)) for h in heads]
        upd = [_tn(v[:, sl[h]], kd[:, sl[h]]) for h in heads]
        for h in heads:
            st_scr[b, h] = st[h] * decay[:, sl[h]] + upd[h]
        a = [jnp.where(causal, a[h], 0.0).astype(BF16) for h in heads]
        o = [_dot(a[h], v[:, sl[h]]) + inter[h] for h in heads]
        outs = [o[h] * lax.rsqrt(jnp.mean(o[h] * o[h], axis=-1, keepdims=True) + EPS) for h in heads]
        gx = g_ref[b]
        o = jnp.concatenate(outs, axis=1) * nw_ref[...] * (gx * _sigmoid(gx))
        o_ref[b] = o.astype(o_ref.dtype)
        return carry

    lax.fori_loop(0, bb, body, 0, unroll=2)

    @pl.when(c == pl.num_programs(1) - 1)
    def _final():
        def store(b, carry):
            for h in range(HG_HEADS):
                s_ref[b, h] = st_scr[b, h].T
            return carry
        lax.fori_loop(0, bb, store, 0)


def _hgrn(proj3, lb, norm_w, s0, cols, chunk, t_valid, bb):
    bsz, t, _ = proj3.shape
    width = lb.shape[0]
    dk = width // HG_HEADS
    has_state = s0 is not None
    spec = lambda cb: pl.BlockSpec((bb, chunk, width), lambda b, c: (b, c, cb))
    vec = pl.BlockSpec((1, width), lambda b, c: (0, 0))
    st_spec = pl.BlockSpec((bb, HG_HEADS, dk, dk), lambda b, c: (b, 0, 0, 0))
    in_specs = [spec(cols[0]), spec(cols[1]), spec(cols[2]), spec(cols[3]), vec, vec]
    args = [proj3, proj3, proj3, proj3, lb.reshape(1, width), norm_w.reshape(1, width)]
    if has_state:
        in_specs.append(st_spec)
        args.append(s0)
    return pl.pallas_call(
        functools.partial(_hgrn_kernel, t_valid=t_valid, has_state=has_state),
        grid=(bsz // bb, t // chunk),
        in_specs=in_specs,
        out_specs=[pl.BlockSpec((bb, chunk, width), lambda b, c: (b, c, 0)), st_spec],
        out_shape=[jax.ShapeDtypeStruct((bsz, t, width), BF16),
                   jax.ShapeDtypeStruct((bsz, HG_HEADS, dk, dk), F32)],
        scratch_shapes=[pltpu.VMEM((bb, HG_HEADS, dk, dk), F32)],
        compiler_params=_cparams("parallel", "arbitrary"),
        name="hgrn_state" if has_state else "hgrn_prompt",
    )(*args)


def _out_proj_kernel(x_ref, a_ref, oh_ref, wo_ref, nw_ref, h_ref, hnt_ref):
    aw = a_ref.shape[1]
    mix = _dot(a_ref[...], wo_ref[:aw, :]) + _dot(oh_ref[...], wo_ref[aw:, :])
    h = x_ref[...] + mix
    h_ref[...] = h
    hn = h * lax.rsqrt(jnp.mean(h * h, axis=-1, keepdims=True) + EPS) * nw_ref[...]
    hnt_ref[...] = hn.T.astype(hnt_ref.dtype)


def _out_proj(x2d, attn, oh, wo_bf16, norm_w, tm):
    t, d = x2d.shape
    aw, hw = attn.shape[1], oh.shape[1]
    return pl.pallas_call(
        _out_proj_kernel,
        grid=(t // tm,),
        in_specs=[pl.BlockSpec((tm, d), lambda i: (i, 0)),
                  pl.BlockSpec((tm, aw), lambda i: (i, 0)),
                  pl.BlockSpec((tm, hw), lambda i: (i, 0)),
                  pl.BlockSpec((aw + hw, d), lambda i: (0, 0)),
                  pl.BlockSpec((1, d), lambda i: (0, 0))],
        out_specs=[pl.BlockSpec((tm, d), lambda i: (i, 0)),
                   pl.BlockSpec((d, tm), lambda i: (0, i))],
        out_shape=[jax.ShapeDtypeStruct((t, d), F32),
                   jax.ShapeDtypeStruct((d, t), BF16)],
        compiler_params=_cparams("parallel"),
        name="out_proj",
    )(x2d, attn, oh, wo_bf16, norm_w.reshape(1, d))


def _peer_scores_kernel(hnt_ref, wqt_ref, keys_ref, sc_ref):
    qt = _dot(wqt_ref[...], hnt_ref[...])
    half = keys_ref.shape[2]
    for hc in range(keys_ref.shape[0]):
        s = _dot(keys_ref[hc], qt[hc * half:(hc + 1) * half, :].astype(BF16))
        for lt in range(sc_ref.shape[0]):
            sc_ref[lt, hc] = s[:, lt * LANES:(lt + 1) * LANES]


def _peer_scores(hnt, wqt_bf16, keys_bf16, tb):
    d, t = hnt.shape
    nhc, nk, half = keys_bf16.shape
    return pl.pallas_call(
        _peer_scores_kernel,
        grid=(t // tb,),
        in_specs=[pl.BlockSpec((d, tb), lambda i: (0, i)),
                  pl.BlockSpec((nhc * half, d), lambda i: (0, 0)),
                  pl.BlockSpec((nhc, nk, half), lambda i: (0, 0, 0))],
        out_specs=pl.BlockSpec((tb // LANES, nhc, nk, LANES), lambda i: (i, 0, 0, 0)),
        out_shape=jax.ShapeDtypeStruct((t // LANES, nhc, nk, LANES), F32),
        compiler_params=_cparams("parallel"),
        name="peer_scores",
    )(hnt, wqt_bf16, keys_bf16)


def _sort16_pairs():
    def merge(lo, hi, r):
        step = r * 2
        if step < hi - lo:
            yield from merge(lo, hi, step)
            yield from merge(lo + r, hi, step)
            yield from [(i, i + r) for i in range(lo + r, hi - r, step)]
        else:
            yield (lo, lo + r)

    def sort(lo, hi):
        if hi - lo >= 1:
            mid = lo + (hi - lo) // 2
            yield from sort(lo, mid)
            yield from sort(mid + 1, hi)
            yield from merge(lo, hi, 1)

    return tuple(sort(0, PEER_TOPK - 1))


_SORT16 = _sort16_pairs()


def _bitonic_to_sorted(z):
    z = list(z)
    d = PEER_TOPK // 2
    while d >= 1:
        for i in range(PEER_TOPK):
            if i & d == 0:
                hi, lo = jnp.maximum(z[i], z[i + d]), jnp.minimum(z[i], z[i + d])
                z[i], z[i + d] = hi, lo
        d //= 2
    return z


def _merge_bitonic(top, other):
    z = list(top)
    m = len(other)
    for r in range(PEER_TOPK - m, PEER_TOPK):
        z[r] = jnp.maximum(top[r], other[PEER_TOPK - 1 - r])
    return z


def _top16_desc(x):
    n = x.shape[0] // SUBLANES
    xs = [x[g * SUBLANES:(g + 1) * SUBLANES, :] for g in range(n)]
    for i, j in _SORT16:
        xs[i], xs[j] = jnp.maximum(xs[i], xs[j]), jnp.minimum(xs[i], xs[j])
    shift = SUBLANES // 2
    while shift >= 1:
        ys = [pltpu.roll(v, shift, axis=0) for v in xs]
        xs = _bitonic_to_sorted(_merge_bitonic(xs, ys))
        shift //= 2
    return xs


def _peer_select(sc_ref, thr_scr, pw_scr, q_scr, h, lt):
    s0 = sc_ref[lt, 2 * h]
    s1 = sc_ref[lt, 2 * h + 1]
    a = _top16_desc(s0)
    b = _top16_desc(s1)
    lists = [[a[r] + b[c] for c in range(PEER_TOPK // (r + 1))] for r in range(SUBLANES)]
    lists.append([a[r] + b[0] for r in range(SUBLANES, PEER_TOPK)])
    top = lists[0]
    for other in lists[1:-1]:
        top = _bitonic_to_sorted(_merge_bitonic(top, other))
    z = _merge_bitonic(top, lists[-1])
    tau = functools.reduce(jnp.minimum, z)
    best = a[0] + b[0]
    zsum = jnp.zeros_like(tau)
    inf = jnp.full_like(tau, jnp.inf)
    thr_rank = []
    for r, cand in enumerate(lists):
        hits = [v >= tau for v in cand]
        for v, hit in zip(cand, hits):
            zsum = zsum + jnp.where(hit, jnp.exp(v - best), 0.0)
        if r < SUBLANES:
            t = inf
            for c, hit in enumerate(hits):
                t = jnp.where(hit, b[c], t)
            thr_rank.append(t)
        else:
            thr_rank.extend(jnp.where(hit, b[0], inf) for hit in hits)
    inv = 1.0 / zsum
    for g in range(s0.shape[0] // SUBLANES):
        rows = slice(g * SUBLANES, (g + 1) * SUBLANES)
        x0 = s0[rows, :]
        thr = jnp.full_like(x0, jnp.inf)
        for r in range(PEER_TOPK):
            thr = jnp.where(x0 == a[r], thr_rank[r], thr)
        thr_scr[lt, h, rows, :] = thr
        pw_scr[lt, h, rows, :] = jnp.exp(x0 - a[0]) * inv
        q_scr[lt, h, rows, :] = jnp.exp(s1[rows, :] - b[0])


MXU_TILE = 256
MXU_COUNT = 2
ACC_ROWS = 512
ACC_PIECE = 32
ACC_PRE = 0
ACC_OUT = ACC_ROWS // 4


def _peer_dense_kernel(sc_ref, hnt_ref, u_ref, vt_ref, yt_ref, thr_scr, pw_scr, q_scr,
                       h0_scr, h1_scr, g0_scr, g1_scr):
    s = pl.program_id(1)
    n_e = pl.num_programs(1) - 2
    d, tb = hnt_ref.shape
    n_lt, _, nk, _ = sc_ref.shape
    eb = u_ref.shape[0]
    n_i = eb // nk
    assert tb == MXU_COUNT * MXU_TILE and eb % ACC_ROWS == 0 and d % ACC_ROWS == 0

    @pl.when(s == 0)
    def _select():
        def body(it, carry):
            _peer_select(sc_ref, thr_scr, pw_scr, q_scr, it // n_lt, it % n_lt)
            return carry
        lax.fori_loop(0, PEER_HEADS * n_lt, body, 0)
        yt_ref[...] = jnp.zeros_like(yt_ref)
        g0_scr[...] = jnp.zeros_like(g0_scr)
        g1_scr[...] = jnp.zeros_like(g1_scr)

    i0 = pl.multiple_of(jnp.clip(s - 1, 0, n_e - 1) * n_i, SUBLANES)

    def gate_steps(lt, ii, h_r, g_w):
        lanes = slice(lt * LANES, (lt + 1) * LANES)
        rows = slice(ii * nk, (ii + 1) * nk)
        state = {"w": jnp.zeros((nk, LANES), F32)}

        def head(h):
            thr = thr_scr[lt, h, pl.ds(i0, n_i), :][ii:ii + 1, :]
            pw = pw_scr[lt, h, pl.ds(i0, n_i), :][ii:ii + 1, :]
            state["w"] = state["w"] + jnp.where(sc_ref[lt, 2 * h + 1] >= thr, q_scr[lt, h], 0.0) * pw
            if h == PEER_HEADS - 1:
                x = h_r[rows, lanes]
                act = 0.5 * x * (1.0 + lax.erf(x * (2.0 ** -0.5)))
                g_w[rows, lanes] = (state["w"] * act).astype(g_w.dtype)

        return [functools.partial(head, h) for h in range(PEER_HEADS)]

    def mxu_steps(kind, c, k, reg, h_w, g_r):
        lhs_ref, rhs_ref, acc, n_k = ((u_ref, hnt_ref, ACC_PRE, d // MXU_TILE) if kind == "pre"
                                      else (vt_ref, g_r, ACC_OUT, eb // MXU_TILE))
        kc = slice(k * MXU_TILE, (k + 1) * MXU_TILE)

        def push():
            for q in range(MXU_COUNT):
                pltpu.matmul_push_rhs(rhs_ref[kc, q * MXU_TILE:(q + 1) * MXU_TILE], staging_register=reg,
                                      mxu_index=q)

        def piece(p):
            r0 = c * ACC_ROWS + p * ACC_PIECE
            lhs = lhs_ref[r0:r0 + ACC_PIECE, kc]
            for q in range(MXU_COUNT):
                pltpu.matmul_acc_lhs(acc + p * ACC_PIECE // 4, lhs, q, load_staged_rhs=reg if p == 0 else None)

        def pop(p):
            rows = slice(c * ACC_ROWS + p * ACC_PIECE, c * ACC_ROWS + (p + 1) * ACC_PIECE)
            for q in range(MXU_COUNT):
                cols = slice(q * MXU_TILE, (q + 1) * MXU_TILE)
                res = pltpu.matmul_pop(acc + p * ACC_PIECE // 4, (ACC_PIECE, MXU_TILE), F32, q)
                if kind == "pre":
                    h_w[rows, cols] = res
                else:
                    yt_ref[rows, cols] += res

        n_p = ACC_ROWS // ACC_PIECE
        pops = [functools.partial(pop, p) for p in range(n_p)] if k == n_k - 1 else []
        return push, [functools.partial(piece, p) for p in range(n_p)], pops

    def stage(h_w, h_r, g_w, g_r, pre, gating, out):
        pre_groups = [("pre", c, k) for c in range(eb // ACC_ROWS) for k in range(d // MXU_TILE)] if pre else []
        out_groups = [("out", c, k) for c in range(d // ACC_ROWS) for k in range(eb // MXU_TILE)] if out else []
        if pre and out:
            order = [g for pair in zip(pre_groups, out_groups) for g in pair]
        else:
            order = pre_groups + out_groups
        steps = [mxu_steps(kind, c, k, gi % 2, h_w, g_r) for gi, (kind, c, k) in enumerate(order)]
        mxu = [steps[0][0]]
        lagged = []
        for gi, (_, pieces, pops) in enumerate(steps):
            half = len(pieces) // 2
            for p, piece in enumerate(pieces):
                if p == half and gi + 1 < len(steps):
                    mxu.append(steps[gi + 1][0])
                mxu.append(piece)
                if lagged:
                    mxu.append(lagged.pop(0))
            same_acc_next = gi + 1 < len(steps) and order[gi + 1][0] == order[gi][0]
            if same_acc_next or gi + 1 == len(steps):
                mxu.extend(pops)
            else:
                lagged = list(pops)
        vpu = [t for lt in range(n_lt) for ii in range(n_i) for t in gate_steps(lt, ii, h_r, g_w)] if gating else []
        im = iv = 0
        while im < len(mxu) or iv < len(vpu):
            if iv >= len(vpu) or (im < len(mxu) and im * len(vpu) <= iv * len(mxu)):
                mxu[im]()
                im += 1
            else:
                vpu[iv]()
                iv += 1

    last = n_e + 1

    @pl.when(s == 0)
    def _first():
        stage(h0_scr, None, None, None, True, False, False)

    @pl.when((s > 0) & (s < last) & (s % 2 == 0))
    def _even():
        stage(h0_scr, h1_scr, g1_scr, g0_scr, True, True, True)

    @pl.when((s > 0) & (s < last) & (s % 2 == 1))
    def _odd():
        stage(h1_scr, h0_scr, g0_scr, g1_scr, True, True, True)

    @pl.when((s == last) & (s % 2 == 0))
    def _last_even():
        stage(None, None, None, g0_scr, False, False, True)

    @pl.when((s == last) & (s % 2 == 1))
    def _last_odd():
        stage(None, None, None, g1_scr, False, False, True)


def _peer_dense(sc, hnt, u_bf16, vt_bf16, tb, eb):
    _, nhc, nk, _ = sc.shape
    d, t = hnt.shape
    n_exp = u_bf16.shape[0]
    n_lt = tb // LANES
    assert eb == SUBLANES * nk and n_exp == nk * nk and t % tb == 0 and tb % LANES == 0
    sel = pltpu.VMEM((n_lt, PEER_HEADS, nk, LANES), F32)
    pre = pltpu.VMEM((eb, tb + LANES), F32)
    gated = pltpu.VMEM((eb, tb), BF16)
    n_e = n_exp // eb
    return pl.pallas_call(
        _peer_dense_kernel,
        grid=(t // tb, n_e + 2),
        in_specs=[pl.BlockSpec((n_lt, nhc, nk, LANES), lambda i, s: (i, 0, 0, 0)),
                  pl.BlockSpec((d, tb), lambda i, s: (0, i)),
                  pl.BlockSpec((eb, d), lambda i, s: (jnp.minimum(s, n_e - 1), 0)),
                  pl.BlockSpec((d, eb), lambda i, s: (0, jnp.clip(s - 2, 0, n_e - 1)))],
        out_specs=pl.BlockSpec((d, tb), lambda i, s: (0, i)),
        out_shape=jax.ShapeDtypeStruct((d, t), F32),
        scratch_shapes=[sel, sel, sel, pre, pre, gated, gated],
        compiler_params=_cparams("parallel", "arbitrary"),
        name="peer_dense",
    )(sc, hnt, u_bf16, vt_bf16)


def _final_kernel(h_ref, yt_ref, nw_ref, o_ref):
    y = h_ref[...] + yt_ref[...].T
    o_ref[...] = y * lax.rsqrt(jnp.mean(y * y, axis=-1, keepdims=True) + EPS) * nw_ref[...]


def _final(h2d, yt, norm_w, tm):
    t, d = h2d.shape
    return pl.pallas_call(
        _final_kernel,
        grid=(t // tm,),
        in_specs=[pl.BlockSpec((tm, d), lambda i: (i, 0)),
                  pl.BlockSpec((d, tm), lambda i: (0, i)),
                  pl.BlockSpec((1, d), lambda i: (0, 0))],
        out_specs=pl.BlockSpec((tm, d), lambda i: (i, 0)),
        out_shape=jax.ShapeDtypeStruct((t, d), F32),
        compiler_params=_cparams("parallel"),
        name="final_norm",
    )(h2d, yt, norm_w.reshape(1, d))


TOKEN_BLOCK = 512
EXPERT_BLOCK = 1024
SAMPLE_T_PAD = 16
SAMPLE_K_PAD = 256


def _ffn(x2d, attn, oh, w, tb):
    h, hnt = _out_proj(x2d, attn, oh, w['wo'], w['norm_ffn'], tb)
    sc = _peer_scores(hnt, w['wqt'], w['keys'], tb)
    yt = _peer_dense(sc, hnt, w['u'], w['vt'], tb, EXPERT_BLOCK)
    return _final(h, yt, w['norm_final'], tb)


def kernel(x_prompt, x_sample, cache_k_win, cache_v_win, state_hgrn, norm_mix_w, w_in, attn_sinks,
           rel_bias_table, hg_lb, hg_norm_w, w_o, norm_ffn_w, peer_w_q, peer_sub_keys, peer_u, peer_v,
           norm_final_w):
    bsz, seq, d = x_prompt.shape
    dbsz, dseq, _ = x_sample.shape
    aw = ATTN_HEADS * HEAD_DIM
    kw = ATTN_KV_HEADS * HEAD_DIM
    hw = hg_norm_w.shape[1]
    wb = cache_k_win.shape[2]

    wi = w_in[0]
    w_in_r = jnp.concatenate([wi[:, :aw], wi[:, aw + 2 * kw:], wi[:, aw:aw + 2 * kw]], axis=1).astype(BF16)
    col_k = (aw + 4 * hw) // kw
    col_v = col_k + 1
    hg_cols = (1, 2, 3, 4)
    lb = jax.nn.softmax(hg_lb.astype(F32), axis=0)[0]
    nhc = PEER_HEADS * 2
    w = {
        'wo': w_o[0].astype(BF16),
        'norm_ffn': norm_ffn_w[0],
        'wqt': peer_w_q[0].T.astype(BF16),
        'keys': peer_sub_keys[0].reshape(nhc, peer_sub_keys.shape[3], peer_sub_keys.shape[4]).astype(BF16),
        'u': peer_u[0].astype(BF16),
        'vt': peer_v[0].T.astype(BF16),
        'norm_final': norm_final_w,
    }
    sinks = attn_sinks[0].astype(F32)

    proj_p = _in_proj(x_prompt.reshape(bsz * seq, d), norm_mix_w[0], w_in_r, TOKEN_BLOCK)
    proj_p3 = proj_p.reshape(bsz, seq, -1)
    blk = WINDOW
    dist_p = (jnp.arange(blk)[:, None] + blk) - jnp.arange(2 * blk)[None, :]
    bias_p = _masked_bias(rel_bias_table, blk, 2 * blk, blk, (dist_p >= 0) & (dist_p <= WINDOW))
    attn_p = _swa_prompt(proj_p3, sinks, bias_p, 0, col_k, col_v)
    oh_p, st_p = _hgrn(proj_p3, lb, hg_norm_w[0], None, hg_cols, HG_CHUNK, HG_CHUNK, bsz)
    y_p = _ffn(x_prompt.reshape(bsz * seq, d), attn_p.reshape(bsz * seq, aw),
               oh_p.reshape(bsz * seq, hw), w, TOKEN_BLOCK)
    k_off = aw + 4 * hw
    wp = min(WINDOW, seq)
    k_win_p = proj_p3[:, seq - wp:, k_off:k_off + kw].reshape(1, bsz, wp, ATTN_KV_HEADS, HEAD_DIM)
    v_win_p = proj_p3[:, seq - wp:, k_off + kw:k_off + 2 * kw].reshape(1, bsz, wp, ATTN_KV_HEADS, HEAD_DIM)

    tp = SAMPLE_T_PAD
    xs_pad = jnp.pad(x_sample, ((0, 0), (0, tp - dseq), (0, 0)))
    proj_s3 = _in_proj(xs_pad.reshape(dbsz * tp, d), norm_mix_w[0], w_in_r, TOKEN_BLOCK).reshape(dbsz, tp, -1)
    k_new = proj_s3[:, :dseq, k_off:k_off + kw]
    v_new = proj_s3[:, :dseq, k_off + kw:k_off + 2 * kw]
    kk = jnp.concatenate([cache_k_win[0].reshape(dbsz, wb, kw), k_new], axis=1)
    vv = jnp.concatenate([cache_v_win[0].reshape(dbsz, wb, kw), v_new], axis=1)
    kpad = SAMPLE_K_PAD - (wb + dseq)
    kk_pad = jnp.pad(kk, ((0, 0), (0, kpad), (0, 0)))
    vv_pad = jnp.pad(vv, ((0, 0), (0, kpad), (0, 0)))
    dist_s = (wb + jnp.arange(tp))[:, None] - jnp.arange(SAMPLE_K_PAD)[None, :]
    mask_s = ((dist_s >= 0) & (dist_s <= WINDOW) & (jnp.arange(tp)[:, None] < dseq)
              & (jnp.arange(SAMPLE_K_PAD)[None, :] < wb + dseq))
    bias_s = _masked_bias(rel_bias_table, tp, SAMPLE_K_PAD, wb, mask_s)
    attn_s = _swa_sample(proj_s3, kk_pad, vv_pad, sinks, bias_s, 0, 16)
    oh_s, st_s = _hgrn(proj_s3, lb, hg_norm_w[0], state_hgrn[0], hg_cols, tp, dseq, 8)
    y_s = _ffn(x_sample.reshape(dbsz * dseq, d), attn_s[:, :dseq].reshape(dbsz * dseq, aw),
               oh_s[:, :dseq].reshape(dbsz * dseq, hw), w, TOKEN_BLOCK)
    k_win_s = kk[:, dseq:].reshape(1, dbsz, wb, ATTN_KV_HEADS, HEAD_DIM)
    v_win_s = vv[:, dseq:].reshape(1, dbsz, wb, ATTN_KV_HEADS, HEAD_DIM)

    return (y_p.reshape(bsz, seq, d), y_s.reshape(dbsz, dseq, d), k_win_p, v_win_p, st_p[None],
            k_win_s, v_win_s, st_s[None])
```

```python
import functools
import math

import jax
import jax.numpy as jnp
from jax import lax
from jax.experimental import pallas as pl
from jax.experimental.pallas import tpu as pltpu

F32 = jnp.float32
BF16 = jnp.bfloat16

EPS = 1e-6
NEG = -1e30

ATTN_HEADS = 8
ATTN_KV_HEADS = 2
HEAD_DIM = 64
WINDOW = 128
REL_BUCKETS = 32
HG_HEADS = 4
HG_CHUNK = 64
PEER_HEADS = 8
PEER_TOPK = 16

LANES = 128
SUBLANES = 8
VMEM_LIMIT = 56 * 1024 * 1024


def _cparams(*sem):
    return pltpu.CompilerParams(dimension_semantics=sem, vmem_limit_bytes=VMEM_LIMIT)


def _nt(a, b):
    return lax.dot_general(a, b, (((1,), (1,)), ((), ())), preferred_element_type=F32)


def _tn(a, b):
    return lax.dot_general(a, b, (((0,), (0,)), ((), ())), preferred_element_type=F32)


def _dot(a, b):
    return jnp.dot(a, b, preferred_element_type=F32)


def _sigmoid(x):
    return 1.0 / (1.0 + jnp.exp(-x))


def _in_proj_kernel(x_ref, nw_ref, w_ref, o_ref):
    x = x_ref[...]
    xn = x * lax.rsqrt(jnp.mean(x * x, axis=-1, keepdims=True) + EPS) * nw_ref[...]
    o_ref[...] = _dot(xn.astype(BF16), w_ref[...])


def _in_proj(x2d, norm_w, w_bf16, tm):
    t, d = x2d.shape
    n = w_bf16.shape[1]
    return pl.pallas_call(
        _in_proj_kernel,
        grid=(t // tm,),
        in_specs=[pl.BlockSpec((tm, d), lambda i: (i, 0)),
                  pl.BlockSpec((1, d), lambda i: (0, 0)),
                  pl.BlockSpec((d, n), lambda i: (0, 0))],
        out_specs=pl.BlockSpec((tm, n), lambda i: (i, 0)),
        out_shape=jax.ShapeDtypeStruct((t, n), F32),
        compiler_params=_cparams("parallel"),
        name="in_proj",
    )(x2d, norm_w.reshape(1, d), w_bf16)


def _t5_bucket(dist):
    n = jnp.maximum(dist, 0)
    max_exact = REL_BUCKETS // 2
    nf = jnp.maximum(n, 1).astype(F32)
    large = max_exact + (jnp.log(nf / max_exact) / math.log(WINDOW / max_exact)
                         * (REL_BUCKETS - max_exact)).astype(jnp.int32)
    large = jnp.minimum(large, REL_BUCKETS - 1)
    return jnp.where(n < max_exact, n, large)


def _masked_bias(table, n_q, n_k, offset, mask):
    h = table.shape[1]
    diag = jnp.arange(n_q + n_k - 1) - (n_k - 1) + offset
    per_diag = table.astype(F32)[_t5_bucket(diag)].T
    w = jnp.pad(per_diag[:, ::-1], ((0, 0), (0, 1)))
    p = n_q + n_k
    skew = jnp.tile(w, (1, n_q))[:, :n_q * (p - 1)].reshape(h, n_q, p - 1)
    return jnp.where(mask[None], skew[:, :, n_q - 1:n_q - 1 + n_k], NEG)


def _swa_prompt_kernel(sink_ref, q_ref, kp_ref, kc_ref, vp_ref, vc_ref, bias_ref, o_ref):
    first = pl.program_id(1) == 0
    scale = HEAD_DIM ** -0.5
    group = ATTN_HEADS // ATTN_KV_HEADS
    blk = q_ref.shape[1]
    assert 2 * HEAD_DIM == LANES and kp_ref.shape[2] == LANES and ATTN_KV_HEADS == 2 and group % 2 == 0
    kk = jnp.concatenate([kp_ref[0], kc_ref[0]], axis=0)
    vv = jnp.concatenate([vp_ref[0], vc_ref[0]], axis=0)
    low = lax.broadcasted_iota(jnp.int32, kk.shape, 1) < HEAD_DIM
    col = lax.broadcasted_iota(jnp.int32, (blk, 2 * blk), 1)
    no_prev = (col < blk) & first
    def halves(x, kvh):
        own = jnp.where(low if kvh == 0 else ~low, x, 0.0)
        other = pltpu.roll(own, HEAD_DIM, axis=1)
        lo, hi = (own, other) if kvh == 0 else (other, own)
        return lo.astype(BF16), hi.astype(BF16)

    k_half = [halves(kk, kvh) for kvh in range(ATTN_KV_HEADS)]
    v_half = [halves(vv, kvh) for kvh in range(ATTN_KV_HEADS)]
    heads = range(ATTN_HEADS)
    kv_of = lambda h: h // group
    s = [_nt(q_ref[0, :, (h // 2) * LANES:(h // 2 + 1) * LANES].astype(BF16), k_half[kv_of(h)][h % 2])
         for h in heads]
    s = [jnp.where(no_prev, NEG, s[h] * scale + bias_ref[h]) for h in heads]
    m = [jnp.maximum(jnp.max(s[h], axis=-1, keepdims=True), sink_ref[h]) for h in heads]
    p = [jnp.exp(s[h] - m[h]) for h in heads]
    den = [jnp.sum(p[h], axis=-1, keepdims=True) + jnp.exp(sink_ref[h] - m[h]) for h in heads]
    p = [(p[h] * (1.0 / den[h])).astype(BF16) for h in heads]
    for tile in range(ATTN_HEADS // 2):
        kvh = kv_of(2 * tile)
        o = _dot(p[2 * tile], v_half[kvh][0]) + _dot(p[2 * tile + 1], v_half[kvh][1])
        o_ref[0, :, tile * LANES:(tile + 1) * LANES] = o.astype(o_ref.dtype)


def _swa_prompt(proj3, sinks, bias, col_q, col_k, col_v):
    bsz, seq, _ = proj3.shape
    blk = WINDOW
    aw = ATTN_HEADS * HEAD_DIM
    kw = ATTN_KV_HEADS * HEAD_DIM
    prev = lambda b, n: (b, jnp.maximum(n - 1, 0))
    return pl.pallas_call(
        _swa_prompt_kernel,
        grid=(bsz, seq // blk),
        in_specs=[pl.BlockSpec(memory_space=pltpu.SMEM),
                  pl.BlockSpec((1, blk, aw), lambda b, n: (b, n, col_q)),
                  pl.BlockSpec((1, blk, kw), lambda b, n: prev(b, n) + (col_k,)),
                  pl.BlockSpec((1, blk, kw), lambda b, n: (b, n, col_k)),
                  pl.BlockSpec((1, blk, kw), lambda b, n: prev(b, n) + (col_v,)),
                  pl.BlockSpec((1, blk, kw), lambda b, n: (b, n, col_v)),
                  pl.BlockSpec((ATTN_HEADS, blk, 2 * blk), lambda b, n: (0, 0, 0))],
        out_specs=pl.BlockSpec((1, blk, aw), lambda b, n: (b, n, 0)),
        out_shape=jax.ShapeDtypeStruct((bsz, seq, aw), BF16),
        compiler_params=_cparams("parallel", "arbitrary"),
        name="swa_prompt",
    )(sinks, proj3, proj3, proj3, proj3, proj3, bias)


def _swa_sample_kernel(sink_ref, q_ref, kk_ref, vv_ref, bias_ref, o_ref):
    scale = HEAD_DIM ** -0.5
    group = ATTN_HEADS // ATTN_KV_HEADS
    for h in range(ATTN_HEADS):
        kv = (h // group) * HEAD_DIM
        qh = q_ref[:, :, h * HEAD_DIM:(h + 1) * HEAD_DIM].astype(BF16)
        kh = kk_ref[:, :, kv:kv + HEAD_DIM].astype(BF16)
        vh = vv_ref[:, :, kv:kv + HEAD_DIM].astype(BF16)
        s = jnp.einsum('bqd,bkd->bqk', qh, kh, preferred_element_type=F32) * scale + bias_ref[h][None]
        sink = sink_ref[h]
        m = jnp.maximum(jnp.max(s, axis=-1, keepdims=True), sink)
        p = jnp.exp(s - m)
        den = jnp.sum(p, axis=-1, keepdims=True) + jnp.exp(sink - m)
        o = jnp.einsum('bqk,bkd->bqd', p.astype(BF16), vh, preferred_element_type=F32) / den
        o_ref[:, :, h * HEAD_DIM:(h + 1) * HEAD_DIM] = o.astype(o_ref.dtype)


def _swa_sample(proj3, kk, vv, sinks, bias, col_q, bb):
    bsz, tp, _ = proj3.shape
    kp = kk.shape[1]
    aw = ATTN_HEADS * HEAD_DIM
    kw = ATTN_KV_HEADS * HEAD_DIM
    return pl.pallas_call(
        _swa_sample_kernel,
        grid=(bsz // bb,),
        in_specs=[pl.BlockSpec(memory_space=pltpu.SMEM),
                  pl.BlockSpec((bb, tp, aw), lambda b: (b, 0, col_q)),
                  pl.BlockSpec((bb, kp, kw), lambda b: (b, 0, 0)),
                  pl.BlockSpec((bb, kp, kw), lambda b: (b, 0, 0)),
                  pl.BlockSpec((ATTN_HEADS, tp, kp), lambda b: (0, 0, 0))],
        out_specs=pl.BlockSpec((bb, tp, aw), lambda b: (b, 0, 0)),
        out_shape=jax.ShapeDtypeStruct((bsz, tp, aw), BF16),
        compiler_params=_cparams("parallel"),
        name="swa_sample",
    )(sinks, proj3, kk, vv, bias)


def _split3(x):
    hi = x.astype(BF16)
    r = x - hi.astype(F32)
    mid = r.astype(BF16)
    lo = (r - mid.astype(F32)).astype(BF16)
    return hi, mid, lo


def _hgrn_kernel(*refs, t_valid, has_state):
    if has_state:
        q_ref, f_ref, i_ref, g_ref, lb_ref, nw_ref, s0_ref, o_ref, s_ref, st_scr = refs
    else:
        q_ref, f_ref, i_ref, g_ref, lb_ref, nw_ref, o_ref, s_ref, st_scr = refs
    bb, chunk, width = q_ref.shape
    dk = width // HG_HEADS
    c = pl.program_id(1)

    @pl.when(c == 0)
    def _init():
        if has_state:
            def load(b, carry):
                for h in range(HG_HEADS):
                    st_scr[b, h] = s0_ref[b, h].T
                return carry
            lax.fori_loop(0, bb, load, 0)
        else:
            st_scr[...] = jnp.zeros_like(st_scr)

    row = lax.broadcasted_iota(jnp.int32, (chunk, chunk), 0)
    col = lax.broadcasted_iota(jnp.int32, (chunk, chunk), 1)
    causal = row >= col
    tri = jnp.where(causal, 1.0, 0.0).astype(BF16)
    valid = lax.broadcasted_iota(jnp.int32, (chunk, width), 0) < t_valid
    mid_row = chunk // 2

    def body(b, carry):
        qx = q_ref[b]
        q = qx * _sigmoid(qx)
        lb = lb_ref[...]
        f = lb + (1.0 - lb) * _sigmoid(f_ref[b])
        k = 1.0 - f
        lg = jnp.log(f)
        if t_valid < chunk:
            k = jnp.where(valid, k, 0.0)
            lg = jnp.where(valid, lg, 0.0)
        v = i_ref[b].astype(BF16)
        cum = sum(_dot(tri, part) for part in _split3(lg))
        cum_mid = cum[mid_row:mid_row + 1, :]
        cum_last = cum[chunk - 1:chunk, :]
        qt = (q * jnp.exp(cum - cum_mid)).astype(BF16)
        kt = (k * jnp.exp(cum_mid - cum)).astype(BF16)
        qe = (q * jnp.exp(cum)).astype(BF16)
        kd = (k * jnp.exp(cum_last - cum)).astype(BF16)
        decay = jnp.exp(cum_last)
        heads = range(HG_HEADS)
        sl = [slice(h * dk, (h + 1) * dk) for h in heads]
        st = [st_scr[b, h] for h in heads]
        a = [_nt(qt[:, sl[h]], kt[:, sl[h]]) for h in heads]
        inter = [_nt(qe[:, sl[h]], st[h].astype(BF16)) for h in heads]
        upd = [_tn(v[:, sl[h]], kd[:, sl[h]]) for h in heads]
        for h in heads:
            st_scr[b, h] = st[h] * decay[:, sl[h]] + upd[h]
        a = [jnp.where(causal, a[h], 0.0).astype(BF16) for h in heads]
        o = [_dot(a[h], v[:, sl[h]]) + inter[h] for h in heads]
        outs = [o[h] * lax.rsqrt(jnp.mean(o[h] * o[h], axis=-1, keepdims=True) + EPS) for h in heads]
        gx = g_ref[b]
        o = jnp.concatenate(outs, axis=1) * nw_ref[...] * (gx * _sigmoid(gx))
        o_ref[b] = o.astype(o_ref.dtype)
        return carry

    lax.fori_loop(0, bb, body, 0, unroll=2)

    @pl.when(c == pl.num_programs(1) - 1)
    def _final():
        def store(b, carry):
            for h in range(HG_HEADS):
                s_ref[b, h] = st_scr[b, h].T
            return carry
        lax.fori_loop(0, bb, store, 0)


def _hgrn(proj3, lb, norm_w, s0, cols, chunk, t_valid, bb):
    bsz, t, _ = proj3.shape
    width = lb.shape[0]
    dk = width // HG_HEADS
    has_state = s0 is not None
    spec = lambda cb: pl.BlockSpec((bb, chunk, width), lambda b, c: (b, c, cb))
    vec = pl.BlockSpec((1, width), lambda b, c: (0, 0))
    st_spec = pl.BlockSpec((bb, HG_HEADS, dk, dk), lambda b, c: (b, 0, 0, 0))
    in_specs = [spec(cols[0]), spec(cols[1]), spec(cols[2]), spec(cols[3]), vec, vec]
    args = [proj3, proj3, proj3, proj3, lb.reshape(1, width), norm_w.reshape(1, width)]
    if has_state:
        in_specs.append(st_spec)
        args.append(s0)
    return pl.pallas_call(
        functools.partial(_hgrn_kernel, t_valid=t_valid, has_state=has_state),
        grid=(bsz // bb, t // chunk),
        in_specs=in_specs,
        out_specs=[pl.BlockSpec((bb, chunk, width), lambda b, c: (b, c, 0)), st_spec],
        out_shape=[jax.ShapeDtypeStruct((bsz, t, width), BF16),
                   jax.ShapeDtypeStruct((bsz, HG_HEADS, dk, dk), F32)],
        scratch_shapes=[pltpu.VMEM((bb, HG_HEADS, dk, dk), F32)],
        compiler_params=_cparams("parallel", "arbitrary"),
        name="hgrn_state" if has_state else "hgrn_prompt",
    )(*args)


def _out_proj_kernel(x_ref, a_ref, oh_ref, wo_ref, nw_ref, h_ref, hnt_ref):
    aw = a_ref.shape[1]
    mix = _dot(a_ref[...], wo_ref[:aw, :]) + _dot(oh_ref[...], wo_ref[aw:, :])
    h = x_ref[...] + mix
    h_ref[...] = h
    hn = h * lax.rsqrt(jnp.mean(h * h, axis=-1, keepdims=True) + EPS) * nw_ref[...]
    hnt_ref[...] = hn.T.astype(hnt_ref.dtype)


def _out_proj(x2d, attn, oh, wo_bf16, norm_w, tm):
    t, d = x2d.shape
    aw, hw = attn.shape[1], oh.shape[1]
    return pl.pallas_call(
        _out_proj_kernel,
        grid=(t // tm,),
        in_specs=[pl.BlockSpec((tm, d), lambda i: (i, 0)),
                  pl.BlockSpec((tm, aw), lambda i: (i, 0)),
                  pl.BlockSpec((tm, hw), lambda i: (i, 0)),
                  pl.BlockSpec((aw + hw, d), lambda i: (0, 0)),
                  pl.BlockSpec((1, d), lambda i: (0, 0))],
        out_specs=[pl.BlockSpec((tm, d), lambda i: (i, 0)),
                   pl.BlockSpec((d, tm), lambda i: (0, i))],
        out_shape=[jax.ShapeDtypeStruct((t, d), F32),
                   jax.ShapeDtypeStruct((d, t), BF16)],
        compiler_params=_cparams("parallel"),
        name="out_proj",
    )(x2d, attn, oh, wo_bf16, norm_w.reshape(1, d))


def _peer_scores_kernel(hnt_ref, wqt_ref, keys_ref, sc_ref):
    qt = _dot(wqt_ref[...], hnt_ref[...])
    half = keys_ref.shape[2]
    for hc in range(keys_ref.shape[0]):
        s = _dot(keys_ref[hc], qt[hc * half:(hc + 1) * half, :].astype(BF16))
        for lt in range(sc_ref.shape[0]):
            sc_ref[lt, hc] = s[:, lt * LANES:(lt + 1) * LANES]


def _peer_scores(hnt, wqt_bf16, keys_bf16, tb):
    d, t = hnt.shape
    nhc, nk, half = keys_bf16.shape
    return pl.pallas_call(
        _peer_scores_kernel,
        grid=(t // tb,),
        in_specs=[pl.BlockSpec((d, tb), lambda i: (0, i)),
                  pl.BlockSpec((nhc * half, d), lambda i: (0, 0)),
                  pl.BlockSpec((nhc, nk, half), lambda i: (0, 0, 0))],
        out_specs=pl.BlockSpec((tb // LANES, nhc, nk, LANES), lambda i: (i, 0, 0, 0)),
        out_shape=jax.ShapeDtypeStruct((t // LANES, nhc, nk, LANES), F32),
        compiler_params=_cparams("parallel"),
        name="peer_scores",
    )(hnt, wqt_bf16, keys_bf16)


def _sort16_pairs():
    def merge(lo, hi, r):
        step = r * 2
        if step < hi - lo:
            yield from merge(lo, hi, step)
            yield from merge(lo + r, hi, step)
            yield from [(i, i + r) for i in range(lo + r, hi - r, step)]
        else:
            yield (lo, lo + r)

    def sort(lo, hi):
        if hi - lo >= 1:
            mid = lo + (hi - lo) // 2
            yield from sort(lo, mid)
            yield from sort(mid + 1, hi)
            yield from merge(lo, hi, 1)

    return tuple(sort(0, PEER_TOPK - 1))


_SORT16 = _sort16_pairs()


def _bitonic_to_sorted(z):
    z = list(z)
    d = PEER_TOPK // 2
    while d >= 1:
        for i in range(PEER_TOPK):
            if i & d == 0:
                hi, lo = jnp.maximum(z[i], z[i + d]), jnp.minimum(z[i], z[i + d])
                z[i], z[i + d] = hi, lo
        d //= 2
    return z


def _merge_bitonic(top, other):
    z = list(top)
    m = len(other)
    for r in range(PEER_TOPK - m, PEER_TOPK):
        z[r] = jnp.maximum(top[r], other[PEER_TOPK - 1 - r])
    return z


def _top16_desc(x):
    n = x.shape[0] // SUBLANES
    xs = [x[g * SUBLANES:(g + 1) * SUBLANES, :] for g in range(n)]
    for i, j in _SORT16:
        xs[i], xs[j] = jnp.maximum(xs[i], xs[j]), jnp.minimum(xs[i], xs[j])
    shift = SUBLANES // 2
    while shift >= 1:
        ys = [pltpu.roll(v, shift, axis=0) for v in xs]
        xs = _bitonic_to_sorted(_merge_bitonic(xs, ys))
        shift //= 2
    return xs


def _peer_select(sc_ref, thr_scr, pw_scr, q_scr, lt):
    assert PEER_HEADS == SUBLANES
    sub = lax.broadcasted_iota(jnp.int32, (SUBLANES, LANES), 0)
    a = b = None
    for h in range(PEER_HEADS):
        a_h = _top16_desc(sc_ref[lt, 2 * h])
        b_h = _top16_desc(sc_ref[lt, 2 * h + 1])
        a = a_h if h == 0 else [jnp.where(sub == h, new, old) for new, old in zip(a_h, a)]
        b = b_h if h == 0 else [jnp.where(sub == h, new, old) for new, old in zip(b_h, b)]
    lists = [[a[r] + b[c] for c in range(PEER_TOPK // (r + 1))] for r in range(SUBLANES)]
    lists.append([a[r] + b[0] for r in range(SUBLANES, PEER_TOPK)])
    top = lists[0]
    for other in lists[1:-1]:
        top = _bitonic_to_sorted(_merge_bitonic(top, other))
    z = _merge_bitonic(top, lists[-1])
    tau = functools.reduce(jnp.minimum, z)
    best = a[0] + b[0]
    zsum = jnp.zeros_like(tau)
    inf = jnp.full_like(tau, jnp.inf)
    thr_rank = []
    for r, cand in enumerate(lists):
        hits = [v >= tau for v in cand]
        for v, hit in zip(cand, hits):
            zsum = zsum + jnp.where(hit, jnp.exp(v - best), 0.0)
        if r < SUBLANES:
            t = inf
            for c, hit in enumerate(hits):
                t = jnp.where(hit, b[c], t)
            thr_rank.append(t)
        else:
            thr_rank.extend(jnp.where(hit, b[0], inf) for hit in hits)
    inv = 1.0 / zsum
    for h in range(PEER_HEADS):
        own = lambda v: jnp.broadcast_to(v[h:h + 1, :], (SUBLANES, LANES))
        a_h = [own(v) for v in a]
        thr_h = [own(v) for v in thr_rank]
        inv_h, b0_h = own(inv), own(b[0])
        s0 = sc_ref[lt, 2 * h]
        s1 = sc_ref[lt, 2 * h + 1]
        for g in range(s0.shape[0] // SUBLANES):
            rows = slice(g * SUBLANES, (g + 1) * SUBLANES)
            x0 = s0[rows, :]
            thr = jnp.full_like(x0, jnp.inf)
            for r in range(PEER_TOPK):
                thr = jnp.where(x0 == a_h[r], thr_h[r], thr)
            thr_scr[lt, h, rows, :] = thr
            pw_scr[lt, h, rows, :] = jnp.exp(x0 - a_h[0]) * inv_h
            q_scr[lt, h, rows, :] = jnp.exp(s1[rows, :] - b0_h)


MXU_TILE = 256
MXU_COUNT = 2
ACC_ROWS = 512
ACC_PIECE = 32
ACC_PRE = 0
ACC_OUT = ACC_ROWS // 4


def _peer_dense_kernel(sc_ref, hnt_ref, u_ref, vt_ref, yt_ref, thr_scr, pw_scr, q_scr,
                       h0_scr, h1_scr, g0_scr, g1_scr):
    s = pl.program_id(1)
    n_e = pl.num_programs(1) - 2
    d, tb = hnt_ref.shape
    n_lt, _, nk, _ = sc_ref.shape
    eb = u_ref.shape[0]
    n_i = eb // nk
    assert tb == MXU_COUNT * MXU_TILE and eb % ACC_ROWS == 0 and d % ACC_ROWS == 0

    @pl.when(s == 0)
    def _select():
        def body(lt, carry):
            _peer_select(sc_ref, thr_scr, pw_scr, q_scr, lt)
            return carry
        lax.fori_loop(0, n_lt, body, 0)
        yt_ref[...] = jnp.zeros_like(yt_ref)
        g0_scr[...] = jnp.zeros_like(g0_scr)
        g1_scr[...] = jnp.zeros_like(g1_scr)

    i0 = pl.multiple_of(jnp.clip(s - 1, 0, n_e - 1) * n_i, SUBLANES)

    def gate_steps(lt, ii, h_r, g_w):
        lanes = slice(lt * LANES, (lt + 1) * LANES)
        rows = slice(ii * nk, (ii + 1) * nk)
        state = {"w": jnp.zeros((nk, LANES), F32)}

        def head(h):
            thr = thr_scr[lt, h, pl.ds(i0, n_i), :][ii:ii + 1, :]
            pw = pw_scr[lt, h, pl.ds(i0, n_i), :][ii:ii + 1, :]
            state["w"] = state["w"] + jnp.where(sc_ref[lt, 2 * h + 1] >= thr, q_scr[lt, h], 0.0) * pw
            if h == PEER_HEADS - 1:
                x = h_r[rows, lanes]
                act = 0.5 * x * (1.0 + lax.erf(x * (2.0 ** -0.5)))
                g_w[rows, lanes] = (state["w"] * act).astype(g_w.dtype)

        return [functools.partial(head, h) for h in range(PEER_HEADS)]

    def mxu_steps(kind, c, k, reg, h_w, g_r):
        lhs_ref, rhs_ref, acc, n_k = ((u_ref, hnt_ref, ACC_PRE, d // MXU_TILE) if kind == "pre"
                                      else (vt_ref, g_r, ACC_OUT, eb // MXU_TILE))
        kc = slice(k * MXU_TILE, (k + 1) * MXU_TILE)

        def push():
            for q in range(MXU_COUNT):
                pltpu.matmul_push_rhs(rhs_ref[kc, q * MXU_TILE:(q + 1) * MXU_TILE], staging_register=reg,
                                      mxu_index=q)

        def piece(p):
            r0 = c * ACC_ROWS + p * ACC_PIECE
            lhs = lhs_ref[r0:r0 + ACC_PIECE, kc]
            for q in range(MXU_COUNT):
                pltpu.matmul_acc_lhs(acc + p * ACC_PIECE // 4, lhs, q, load_staged_rhs=reg if p == 0 else None)

        def pop(p):
            rows = slice(c * ACC_ROWS + p * ACC_PIECE, c * ACC_ROWS + (p + 1) * ACC_PIECE)
            for q in range(MXU_COUNT):
                cols = slice(q * MXU_TILE, (q + 1) * MXU_TILE)
                res = pltpu.matmul_pop(acc + p * ACC_PIECE // 4, (ACC_PIECE, MXU_TILE), F32, q)
                if kind == "pre":
                    h_w[rows, cols] = res
                else:
                    yt_ref[rows, cols] += res

        n_p = ACC_ROWS // ACC_PIECE
        pops = [functools.partial(pop, p) for p in range(n_p)] if k == n_k - 1 else []
        return push, [functools.partial(piece, p) for p in range(n_p)], pops

    def stage(h_w, h_r, g_w, g_r, pre, gating, out):
        pre_groups = [("pre", c, k) for c in range(eb // ACC_ROWS) for k in range(d // MXU_TILE)] if pre else []
        out_groups = [("out", c, k) for c in range(d // ACC_ROWS) for k in range(eb // MXU_TILE)] if out else []
        if pre and out:
            order = [g for pair in zip(pre_groups, out_groups) for g in pair]
        else:
            order = pre_groups + out_groups
        steps = [mxu_steps(kind, c, k, gi % 2, h_w, g_r) for gi, (kind, c, k) in enumerate(order)]
        mxu = [steps[0][0]]
        lagged = []
        for gi, (_, pieces, pops) in enumerate(steps):
            half = len(pieces) // 2
            for p, piece in enumerate(pieces):
                if p == half and gi + 1 < len(steps):
                    mxu.append(steps[gi + 1][0])
                mxu.append(piece)
                if lagged:
                    mxu.append(lagged.pop(0))
            same_acc_next = gi + 1 < len(steps) and order[gi + 1][0] == order[gi][0]
            if same_acc_next or gi + 1 == len(steps):
                mxu.extend(pops)
            else:
                lagged = list(pops)
        vpu = [t for lt in range(n_lt) for ii in range(n_i) for t in gate_steps(lt, ii, h_r, g_w)] if gating else []
        im = iv = 0
        while im < len(mxu) or iv < len(vpu):
            if iv >= len(vpu) or (im < len(mxu) and im * len(vpu) <= iv * len(mxu)):
                mxu[im]()
                im += 1
            else:
                vpu[iv]()
                iv += 1

    last = n_e + 1

    @pl.when(s == 0)
    def _first():
        stage(h0_scr, None, None, None, True, False, False)

    @pl.when((s > 0) & (s < last) & (s % 2 == 0))
    def _even():
        stage(h0_scr, h1_scr, g1_scr, g0_scr, True, True, True)

    @pl.when((s > 0) & (s < last) & (s % 2 == 1))
    def _odd():
        stage(h1_scr, h0_scr, g0_scr, g1_scr, True, True, True)

    @pl.when((s == last) & (s % 2 == 0))
    def _last_even():
        stage(None, None, None, g0_scr, False, False, True)

    @pl.when((s == last) & (s % 2 == 1))
    def _last_odd():
        stage(None, None, None, g1_scr, False, False, True)


def _peer_dense(sc, hnt, u_bf16, vt_bf16, tb, eb):
    _, nhc, nk, _ = sc.shape
    d, t = hnt.shape
    n_exp = u_bf16.shape[0]
    n_lt = tb // LANES
    assert eb == SUBLANES * nk and n_exp == nk * nk and t % tb == 0 and tb % LANES == 0
    sel = pltpu.VMEM((n_lt, PEER_HEADS, nk, LANES), F32)
    pre = pltpu.VMEM((eb, tb + LANES), F32)
    gated = pltpu.VMEM((eb, tb), BF16)
    n_e = n_exp // eb
    return pl.pallas_call(
        _peer_dense_kernel,
        grid=(t // tb, n_e + 2),
        in_specs=[pl.BlockSpec((n_lt, nhc, nk, LANES), lambda i, s: (i, 0, 0, 0)),
                  pl.BlockSpec((d, tb), lambda i, s: (0, i)),
                  pl.BlockSpec((eb, d), lambda i, s: (jnp.minimum(s, n_e - 1), 0)),
                  pl.BlockSpec((d, eb), lambda i, s: (0, jnp.clip(s - 2, 0, n_e - 1)))],
        out_specs=pl.BlockSpec((d, tb), lambda i, s: (0, i)),
        out_shape=jax.ShapeDtypeStruct((d, t), F32),
        scratch_shapes=[sel, sel, sel, pre, pre, gated, gated],
        compiler_params=_cparams("parallel", "arbitrary"),
        name="peer_dense",
    )(sc, hnt, u_bf16, vt_bf16)


def _final_kernel(h_ref, yt_ref, nw_ref, o_ref):
    y = h_ref[...] + yt_ref[...].T
    o_ref[...] = y * lax.rsqrt(jnp.mean(y * y, axis=-1, keepdims=True) + EPS) * nw_ref[...]


def _final(h2d, yt, norm_w, tm):
    t, d = h2d.shape
    return pl.pallas_call(
        _final_kernel,
        grid=(t // tm,),
        in_specs=[pl.BlockSpec((tm, d), lambda i: (i, 0)),
                  pl.BlockSpec((d, tm), lambda i: (0, i)),
                  pl.BlockSpec((1, d), lambda i: (0, 0))],
        out_specs=pl.BlockSpec((tm, d), lambda i: (i, 0)),
        out_shape=jax.ShapeDtypeStruct((t, d), F32),
        compiler_params=_cparams("parallel"),
        name="final_norm",
    )(h2d, yt, norm_w.reshape(1, d))


TOKEN_BLOCK = 512
EXPERT_BLOCK = 1024
SAMPLE_T_PAD = 16
SAMPLE_K_PAD = 256


def _ffn(x2d, attn, oh, w, tb):
    h, hnt = _out_proj(x2d, attn, oh, w['wo'], w['norm_ffn'], tb)
    sc = _peer_scores(hnt, w['wqt'], w['keys'], tb)
    yt = _peer_dense(sc, hnt, w['u'], w['vt'], tb, EXPERT_BLOCK)
    return _final(h, yt, w['norm_final'], tb)


def kernel(x_prompt, x_sample, cache_k_win, cache_v_win, state_hgrn, norm_mix_w, w_in, attn_sinks,
           rel_bias_table, hg_lb, hg_norm_w, w_o, norm_ffn_w, peer_w_q, peer_sub_keys, peer_u, peer_v,
           norm_final_w):
    bsz, seq, d = x_prompt.shape
    dbsz, dseq, _ = x_sample.shape
    aw = ATTN_HEADS * HEAD_DIM
    kw = ATTN_KV_HEADS * HEAD_DIM
    hw = hg_norm_w.shape[1]
    wb = cache_k_win.shape[2]

    wi = w_in[0]
    w_in_r = jnp.concatenate([wi[:, :aw], wi[:, aw + 2 * kw:], wi[:, aw:aw + 2 * kw]], axis=1).astype(BF16)
    col_k = (aw + 4 * hw) // kw
    col_v = col_k + 1
    hg_cols = (1, 2, 3, 4)
    lb = jax.nn.softmax(hg_lb.astype(F32), axis=0)[0]
    nhc = PEER_HEADS * 2
    w = {
        'wo': w_o[0].astype(BF16),
        'norm_ffn': norm_ffn_w[0],
        'wqt': peer_w_q[0].T.astype(BF16),
        'keys': peer_sub_keys[0].reshape(nhc, peer_sub_keys.shape[3], peer_sub_keys.shape[4]).astype(BF16),
        'u': peer_u[0].astype(BF16),
        'vt': peer_v[0].T.astype(BF16),
        'norm_final': norm_final_w,
    }
    sinks = attn_sinks[0].astype(F32)

    proj_p = _in_proj(x_prompt.reshape(bsz * seq, d), norm_mix_w[0], w_in_r, TOKEN_BLOCK)
    proj_p3 = proj_p.reshape(bsz, seq, -1)
    blk = WINDOW
    dist_p = (jnp.arange(blk)[:, None] + blk) - jnp.arange(2 * blk)[None, :]
    bias_p = _masked_bias(rel_bias_table, blk, 2 * blk, blk, (dist_p >= 0) & (dist_p <= WINDOW))
    attn_p = _swa_prompt(proj_p3, sinks, bias_p, 0, col_k, col_v)
    oh_p, st_p = _hgrn(proj_p3, lb, hg_norm_w[0], None, hg_cols, HG_CHUNK, HG_CHUNK, bsz)
    y_p = _ffn(x_prompt.reshape(bsz * seq, d), attn_p.reshape(bsz * seq, aw),
               oh_p.reshape(bsz * seq, hw), w, TOKEN_BLOCK)
    k_off = aw + 4 * hw
    wp = min(WINDOW, seq)
    k_win_p = proj_p3[:, seq - wp:, k_off:k_off + kw].reshape(1, bsz, wp, ATTN_KV_HEADS, HEAD_DIM)
    v_win_p = proj_p3[:, seq - wp:, k_off + kw:k_off + 2 * kw].reshape(1, bsz, wp, ATTN_KV_HEADS, HEAD_DIM)

    tp = SAMPLE_T_PAD
    xs_pad = jnp.pad(x_sample, ((0, 0), (0, tp - dseq), (0, 0)))
    proj_s3 = _in_proj(xs_pad.reshape(dbsz * tp, d), norm_mix_w[0], w_in_r, TOKEN_BLOCK).reshape(dbsz, tp, -1)
    k_new = proj_s3[:, :dseq, k_off:k_off + kw]
    v_new = proj_s3[:, :dseq, k_off + kw:k_off + 2 * kw]
    kk = jnp.concatenate([cache_k_win[0].reshape(dbsz, wb, kw), k_new], axis=1)
    vv = jnp.concatenate([cache_v_win[0].reshape(dbsz, wb, kw), v_new], axis=1)
    kpad = SAMPLE_K_PAD - (wb + dseq)
    kk_pad = jnp.pad(kk, ((0, 0), (0, kpad), (0, 0)))
    vv_pad = jnp.pad(vv, ((0, 0), (0, kpad), (0, 0)))
    dist_s = (wb + jnp.arange(tp))[:, None] - jnp.arange(SAMPLE_K_PAD)[None, :]
    mask_s = ((dist_s >= 0) & (dist_s <= WINDOW) & (jnp.arange(tp)[:, None] < dseq)
              & (jnp.arange(SAMPLE_K_PAD)[None, :] < wb + dseq))
    bias_s = _masked_bias(rel_bias_table, tp, SAMPLE_K_PAD, wb, mask_s)
    attn_s = _swa_sample(proj_s3, kk_pad, vv_pad, sinks, bias_s, 0, 16)
    oh_s, st_s = _hgrn(proj_s3, lb, hg_norm_w[0], state_hgrn[0], hg_cols, tp, dseq, 8)
    y_s = _ffn(x_sample.reshape(dbsz * dseq, d), attn_s[:, :dseq].reshape(dbsz * dseq, aw),
               oh_s[:, :dseq].reshape(dbsz * dseq, hw), w, TOKEN_BLOCK)
    k_win_s = kk[:, dseq:].reshape(1, dbsz, wb, ATTN_KV_HEADS, HEAD_DIM)
    v_win_s = vv[:, dseq:].reshape(1, dbsz, wb, ATTN_KV_HEADS, HEAD_DIM)

    return (y_p.reshape(bsz, seq, d), y_s.reshape(dbsz, dseq, d), k_win_p, v_win_p, st_p[None],
            k_win_s, v_win_s, st_s[None])
```

```python
import functools
import math

import jax
import jax.numpy as jnp
from jax import lax
from jax.experimental import pallas as pl
from jax.experimental.pallas import tpu as pltpu

F32 = jnp.float32
BF16 = jnp.bfloat16

EPS = 1e-6
NEG = -1e30

ATTN_HEADS = 8
ATTN_KV_HEADS = 2
HEAD_DIM = 64
WINDOW = 128
REL_BUCKETS = 32
HG_HEADS = 4
HG_CHUNK = 64
PEER_HEADS = 8
PEER_TOPK = 16

LANES = 128
SUBLANES = 8
VMEM_LIMIT = 56 * 1024 * 1024


def _cparams(*sem):
    return pltpu.CompilerParams(dimension_semantics=sem, vmem_limit_bytes=VMEM_LIMIT)


def _nt(a, b):
    return lax.dot_general(a, b, (((1,), (1,)), ((), ())), preferred_element_type=F32)


def _tn(a, b):
    return lax.dot_general(a, b, (((0,), (0,)), ((), ())), preferred_element_type=F32)


def _dot(a, b):
    return jnp.dot(a, b, preferred_element_type=F32)


def _sigmoid(x):
    return 1.0 / (1.0 + jnp.exp(-x))


def _in_proj_kernel(x_ref, nw_ref, w_ref, o_ref):
    x = x_ref[...]
    xn = x * lax.rsqrt(jnp.mean(x * x, axis=-1, keepdims=True) + EPS) * nw_ref[...]
    o_ref[...] = _dot(xn.astype(BF16), w_ref[...])


def _in_proj(x2d, norm_w, w_bf16, tm):
    t, d = x2d.shape
    n = w_bf16.shape[1]
    return pl.pallas_call(
        _in_proj_kernel,
        grid=(t // tm,),
        in_specs=[pl.BlockSpec((tm, d), lambda i: (i, 0)),
                  pl.BlockSpec((1, d), lambda i: (0, 0)),
                  pl.BlockSpec((d, n), lambda i: (0, 0))],
        out_specs=pl.BlockSpec((tm, n), lambda i: (i, 0)),
        out_shape=jax.ShapeDtypeStruct((t, n), F32),
        compiler_params=_cparams("parallel"),
        name="in_proj",
    )(x2d, norm_w.reshape(1, d), w_bf16)


def _t5_bucket(dist):
    n = jnp.maximum(dist, 0)
    max_exact = REL_BUCKETS // 2
    nf = jnp.maximum(n, 1).astype(F32)
    large = max_exact + (jnp.log(nf / max_exact) / math.log(WINDOW / max_exact)
                         * (REL_BUCKETS - max_exact)).astype(jnp.int32)
    large = jnp.minimum(large, REL_BUCKETS - 1)
    return jnp.where(n < max_exact, n, large)


def _masked_bias(table, n_q, n_k, offset, mask):
    h = table.shape[1]
    diag = jnp.arange(n_q + n_k - 1) - (n_k - 1) + offset
    per_diag = table.astype(F32)[_t5_bucket(diag)].T
    w = jnp.pad(per_diag[:, ::-1], ((0, 0), (0, 1)))
    p = n_q + n_k
    skew = jnp.tile(w, (1, n_q))[:, :n_q * (p - 1)].reshape(h, n_q, p - 1)
    return jnp.where(mask[None], skew[:, :, n_q - 1:n_q - 1 + n_k], NEG)


def _swa_prompt_kernel(sink_ref, q_ref, kp_ref, kc_ref, vp_ref, vc_ref, bias_ref, o_ref):
    first = pl.program_id(1) == 0
    scale = HEAD_DIM ** -0.5
    group = ATTN_HEADS // ATTN_KV_HEADS
    blk = q_ref.shape[1]
    assert 2 * HEAD_DIM == LANES and kp_ref.shape[2] == LANES and ATTN_KV_HEADS == 2 and group % 2 == 0
    kk = jnp.concatenate([kp_ref[0], kc_ref[0]], axis=0)
    vv = jnp.concatenate([vp_ref[0], vc_ref[0]], axis=0)
    low = lax.broadcasted_iota(jnp.int32, kk.shape, 1) < HEAD_DIM
    col = lax.broadcasted_iota(jnp.int32, (blk, 2 * blk), 1)
    no_prev = (col < blk) & first
    def halves(x, kvh):
        own = jnp.where(low if kvh == 0 else ~low, x, 0.0)
        other = pltpu.roll(own, HEAD_DIM, axis=1)
        lo, hi = (own, other) if kvh == 0 else (other, own)
        return lo.astype(BF16), hi.astype(BF16)

    k_half = [halves(kk, kvh) for kvh in range(ATTN_KV_HEADS)]
    v_half = [halves(vv, kvh) for kvh in range(ATTN_KV_HEADS)]
    heads = range(ATTN_HEADS)
    kv_of = lambda h: h // group
    s = [_nt(q_ref[0, :, (h // 2) * LANES:(h // 2 + 1) * LANES].astype(BF16), k_half[kv_of(h)][h % 2])
         for h in heads]
    s = [jnp.where(no_prev, NEG, s[h] * scale + bias_ref[h]) for h in heads]
    m = [jnp.maximum(jnp.max(s[h], axis=-1, keepdims=True), sink_ref[h]) for h in heads]
    p = [jnp.exp(s[h] - m[h]) for h in heads]
    den = [jnp.sum(p[h], axis=-1, keepdims=True) + jnp.exp(sink_ref[h] - m[h]) for h in heads]
    p = [(p[h] * (1.0 / den[h])).astype(BF16) for h in heads]
    for tile in range(ATTN_HEADS // 2):
        kvh = kv_of(2 * tile)
        o = _dot(p[2 * tile], v_half[kvh][0]) + _dot(p[2 * tile + 1], v_half[kvh][1])
        o_ref[0, :, tile * LANES:(tile + 1) * LANES] = o.astype(o_ref.dtype)


def _swa_prompt(proj3, sinks, bias, col_q, col_k, col_v):
    bsz, seq, _ = proj3.shape
    blk = WINDOW
    aw = ATTN_HEADS * HEAD_DIM
    kw = ATTN_KV_HEADS * HEAD_DIM
    prev = lambda b, n: (b, jnp.maximum(n - 1, 0))
    return pl.pallas_call(
        _swa_prompt_kernel,
        grid=(bsz, seq // blk),
        in_specs=[pl.BlockSpec(memory_space=pltpu.SMEM),
                  pl.BlockSpec((1, blk, aw), lambda b, n: (b, n, col_q)),
                  pl.BlockSpec((1, blk, kw), lambda b, n: prev(b, n) + (col_k,)),
                  pl.BlockSpec((1, blk, kw), lambda b, n: (b, n, col_k)),
                  pl.BlockSpec((1, blk, kw), lambda b, n: prev(b, n) + (col_v,)),
                  pl.BlockSpec((1, blk, kw), lambda b, n: (b, n, col_v)),
                  pl.BlockSpec((ATTN_HEADS, blk, 2 * blk), lambda b, n: (0, 0, 0))],
        out_specs=pl.BlockSpec((1, blk, aw), lambda b, n: (b, n, 0)),
        out_shape=jax.ShapeDtypeStruct((bsz, seq, aw), BF16),
        compiler_params=_cparams("parallel", "arbitrary"),
        name="swa_prompt",
    )(sinks, proj3, proj3, proj3, proj3, proj3, bias)


def _swa_sample_kernel(sink_ref, q_ref, kk_ref, vv_ref, bias_ref, o_ref):
    scale = HEAD_DIM ** -0.5
    group = ATTN_HEADS // ATTN_KV_HEADS
    for h in range(ATTN_HEADS):
        kv = (h // group) * HEAD_DIM
        qh = q_ref[:, :, h * HEAD_DIM:(h + 1) * HEAD_DIM].astype(BF16)
        kh = kk_ref[:, :, kv:kv + HEAD_DIM].astype(BF16)
        vh = vv_ref[:, :, kv:kv + HEAD_DIM].astype(BF16)
        s = jnp.einsum('bqd,bkd->bqk', qh, kh, preferred_element_type=F32) * scale + bias_ref[h][None]
        sink = sink_ref[h]
        m = jnp.maximum(jnp.max(s, axis=-1, keepdims=True), sink)
        p = jnp.exp(s - m)
        den = jnp.sum(p, axis=-1, keepdims=True) + jnp.exp(sink - m)
        o = jnp.einsum('bqk,bkd->bqd', p.astype(BF16), vh, preferred_element_type=F32) / den
        o_ref[:, :, h * HEAD_DIM:(h + 1) * HEAD_DIM] = o.astype(o_ref.dtype)


def _swa_sample(proj3, kk, vv, sinks, bias, col_q, bb):
    bsz, tp, _ = proj3.shape
    kp = kk.shape[1]
    aw = ATTN_HEADS * HEAD_DIM
    kw = ATTN_KV_HEADS * HEAD_DIM
    return pl.pallas_call(
        _swa_sample_kernel,
        grid=(bsz // bb,),
        in_specs=[pl.BlockSpec(memory_space=pltpu.SMEM),
                  pl.BlockSpec((bb, tp, aw), lambda b: (b, 0, col_q)),
                  pl.BlockSpec((bb, kp, kw), lambda b: (b, 0, 0)),
                  pl.BlockSpec((bb, kp, kw), lambda b: (b, 0, 0)),
                  pl.BlockSpec((ATTN_HEADS, tp, kp), lambda b: (0, 0, 0))],
        out_specs=pl.BlockSpec((bb, tp, aw), lambda b: (b, 0, 0)),
        out_shape=jax.ShapeDtypeStruct((bsz, tp, aw), BF16),
        compiler_params=_cparams("parallel"),
        name="swa_sample",
    )(sinks, proj3, kk, vv, bias)


def _split3(x):
    hi = x.astype(BF16)
    r = x - hi.astype(F32)
    mid = r.astype(BF16)
    lo = (r - mid.astype(F32)).astype(BF16)
    return hi, mid, lo


def _hgrn_kernel(*refs, t_valid, has_state):
    if has_state:
        q_ref, f_ref, i_ref, g_ref, lb_ref, nw_ref, s0_ref, o_ref, s_ref, st_scr = refs
    else:
        q_ref, f_ref, i_ref, g_ref, lb_ref, nw_ref, o_ref, s_ref, st_scr = refs
    bb, chunk, width = q_ref.shape
    dk = width // HG_HEADS
    c = pl.program_id(1)

    @pl.when(c == 0)
    def _init():
        if has_state:
            def load(b, carry):
                for h in range(HG_HEADS):
                    st_scr[b, h] = s0_ref[b, h].T
                return carry
            lax.fori_loop(0, bb, load, 0)
        else:
            st_scr[...] = jnp.zeros_like(st_scr)

    row = lax.broadcasted_iota(jnp.int32, (chunk, chunk), 0)
    col = lax.broadcasted_iota(jnp.int32, (chunk, chunk), 1)
    causal = row >= col
    tri = jnp.where(causal, 1.0, 0.0).astype(BF16)
    valid = lax.broadcasted_iota(jnp.int32, (chunk, width), 0) < t_valid
    mid_row = chunk // 2

    def body(b, carry):
        qx = q_ref[b]
        q = qx * _sigmoid(qx)
        lb = lb_ref[...]
        f = lb + (1.0 - lb) * _sigmoid(f_ref[b])
        k = 1.0 - f
        lg = jnp.log(f)
        if t_valid < chunk:
            k = jnp.where(valid, k, 0.0)
            lg = jnp.where(valid, lg, 0.0)
        v = i_ref[b].astype(BF16)
        cum = sum(_dot(tri, part) for part in _split3(lg))
        cum_mid = cum[mid_row:mid_row + 1, :]
        cum_last = cum[chunk - 1:chunk, :]
        qt = (q * jnp.exp(cum - cum_mid)).astype(BF16)
        kt = (k * jnp.exp(cum_mid - cum)).astype(BF16)
        qe = (q * jnp.exp(cum)).astype(BF16)
        kd = (k * jnp.exp(cum_last - cum)).astype(BF16)
        decay = jnp.exp(cum_last)
        heads = range(HG_HEADS)
        sl = [slice(h * dk, (h + 1) * dk) for h in heads]
        st = [st_scr[b, h] for h in heads]
        a = [_nt(qt[:, sl[h]], kt[:, sl[h]]) for h in heads]
        inter = [_nt(qe[:, sl[h]], st[h].astype(BF16)) for h in heads]
        upd = [_tn(v[:, sl[h]], kd[:, sl[h]]) for h in heads]
        for h in heads:
            st_scr[b, h] = st[h] * decay[:, sl[h]] + upd[h]
        a = [jnp.where(causal, a[h], 0.0).astype(BF16) for h in heads]
        o = [_dot(a[h], v[:, sl[h]]) + inter[h] for h in heads]
        outs = [o[h] * lax.rsqrt(jnp.mean(o[h] * o[h], axis=-1, keepdims=True) + EPS) for h in heads]
        gx = g_ref[b]
        o = jnp.concatenate(outs, axis=1) * nw_ref[...] * (gx * _sigmoid(gx))
        o_ref[b] = o.astype(o_ref.dtype)
        return carry

    lax.fori_loop(0, bb, body, 0, unroll=2)

    @pl.when(c == pl.num_programs(1) - 1)
    def _final():
        def store(b, carry):
            for h in range(HG_HEADS):
                s_ref[b, h] = st_scr[b, h].T
            return carry
        lax.fori_loop(0, bb, store, 0)


def _hgrn(proj3, lb, norm_w, s0, cols, chunk, t_valid, bb):
    bsz, t, _ = proj3.shape
    width = lb.shape[0]
    dk = width // HG_HEADS
    has_state = s0 is not None
    spec = lambda cb: pl.BlockSpec((bb, chunk, width), lambda b, c: (b, c, cb))
    vec = pl.BlockSpec((1, width), lambda b, c: (0, 0))
    st_spec = pl.BlockSpec((bb, HG_HEADS, dk, dk), lambda b, c: (b, 0, 0, 0))
    in_specs = [spec(cols[0]), spec(cols[1]), spec(cols[2]), spec(cols[3]), vec, vec]
    args = [proj3, proj3, proj3, proj3, lb.reshape(1, width), norm_w.reshape(1, width)]
    if has_state:
        in_specs.append(st_spec)
        args.append(s0)
    return pl.pallas_call(
        functools.partial(_hgrn_kernel, t_valid=t_valid, has_state=has_state),
        grid=(bsz // bb, t // chunk),
        in_specs=in_specs,
        out_specs=[pl.BlockSpec((bb, chunk, width), lambda b, c: (b, c, 0)), st_spec],
        out_shape=[jax.ShapeDtypeStruct((bsz, t, width), BF16),
                   jax.ShapeDtypeStruct((bsz, HG_HEADS, dk, dk), F32)],
        scratch_shapes=[pltpu.VMEM((bb, HG_HEADS, dk, dk), F32)],
        compiler_params=_cparams("parallel", "arbitrary"),
        name="hgrn_state" if has_state else "hgrn_prompt",
    )(*args)


def _out_proj_kernel(x_ref, a_ref, oh_ref, wo_ref, nw_ref, h_ref, hnt_ref):
    aw = a_ref.shape[1]
    mix = _dot(a_ref[...], wo_ref[:aw, :]) + _dot(oh_ref[...], wo_ref[aw:, :])
    h = x_ref[...] + mix
    h_ref[...] = h
    hn = h * lax.rsqrt(jnp.mean(h * h, axis=-1, keepdims=True) + EPS) * nw_ref[...]
    hnt_ref[...] = hn.T.astype(hnt_ref.dtype)


def _out_proj(x2d, attn, oh, wo_bf16, norm_w, tm):
    t, d = x2d.shape
    aw, hw = attn.shape[1], oh.shape[1]
    return pl.pallas_call(
        _out_proj_kernel,
        grid=(t // tm,),
        in_specs=[pl.BlockSpec((tm, d), lambda i: (i, 0)),
                  pl.BlockSpec((tm, aw), lambda i: (i, 0)),
                  pl.BlockSpec((tm, hw), lambda i: (i, 0)),
                  pl.BlockSpec((aw + hw, d), lambda i: (0, 0)),
                  pl.BlockSpec((1, d), lambda i: (0, 0))],
        out_specs=[pl.BlockSpec((tm, d), lambda i: (i, 0)),
                   pl.BlockSpec((d, tm), lambda i: (0, i))],
        out_shape=[jax.ShapeDtypeStruct((t, d), F32),
                   jax.ShapeDtypeStruct((d, t), BF16)],
        compiler_params=_cparams("parallel"),
        name="out_proj",
    )(x2d, attn, oh, wo_bf16, norm_w.reshape(1, d))


def _peer_scores_kernel(hnt_ref, wqt_ref, keys_ref, sc_ref):
    qt = _dot(wqt_ref[...], hnt_ref[...])
    half = keys_ref.shape[2]
    for hc in range(keys_ref.shape[0]):
        s = _dot(keys_ref[hc], qt[hc * half:(hc + 1) * half, :].astype(BF16))
        for lt in range(sc_ref.shape[0]):
            sc_ref[lt, hc] = s[:, lt * LANES:(lt + 1) * LANES]


def _peer_scores(hnt, wqt_bf16, keys_bf16, tb):
    d, t = hnt.shape
    nhc, nk, half = keys_bf16.shape
    return pl.pallas_call(
        _peer_scores_kernel,
        grid=(t // tb,),
        in_specs=[pl.BlockSpec((d, tb), lambda i: (0, i)),
                  pl.BlockSpec((nhc * half, d), lambda i: (0, 0)),
                  pl.BlockSpec((nhc, nk, half), lambda i: (0, 0, 0))],
        out_specs=pl.BlockSpec((tb // LANES, nhc, nk, LANES), lambda i: (i, 0, 0, 0)),
        out_shape=jax.ShapeDtypeStruct((t // LANES, nhc, nk, LANES), F32),
        compiler_params=_cparams("parallel"),
        name="peer_scores",
    )(hnt, wqt_bf16, keys_bf16)


def _sort16_pairs():
    def merge(lo, hi, r):
        step = r * 2
        if step < hi - lo:
            yield from merge(lo, hi, step)
            yield from merge(lo + r, hi, step)
            yield from [(i, i + r) for i in range(lo + r, hi - r, step)]
        else:
            yield (lo, lo + r)

    def sort(lo, hi):
        if hi - lo >= 1:
            mid = lo + (hi - lo) // 2
            yield from sort(lo, mid)
            yield from sort(mid + 1, hi)
            yield from merge(lo, hi, 1)

    return tuple(sort(0, PEER_TOPK - 1))


_SORT16 = _sort16_pairs()


def _bitonic_to_sorted(z):
    z = list(z)
    d = PEER_TOPK // 2
    while d >= 1:
        for i in range(PEER_TOPK):
            if i & d == 0:
                hi, lo = jnp.maximum(z[i], z[i + d]), jnp.minimum(z[i], z[i + d])
                z[i], z[i + d] = hi, lo
        d //= 2
    return z


def _merge_bitonic(top, other):
    z = list(top)
    m = len(other)
    for r in range(PEER_TOPK - m, PEER_TOPK):
        z[r] = jnp.maximum(top[r], other[PEER_TOPK - 1 - r])
    return z


def _top16_desc(x):
    n = x.shape[0] // SUBLANES
    xs = [x[g * SUBLANES:(g + 1) * SUBLANES, :] for g in range(n)]
    for i, j in _SORT16:
        xs[i], xs[j] = jnp.maximum(xs[i], xs[j]), jnp.minimum(xs[i], xs[j])
    shift = SUBLANES // 2
    while shift >= 1:
        ys = [pltpu.roll(v, shift, axis=0) for v in xs]
        xs = _bitonic_to_sorted(_merge_bitonic(xs, ys))
        shift //= 2
    return xs


def _peer_select(sc_ref, thr_scr, pw_scr, q_scr, lt):
    assert PEER_HEADS == SUBLANES
    sub = lax.broadcasted_iota(jnp.int32, (SUBLANES, LANES), 0)
    a = b = None
    for h in range(PEER_HEADS):
        a_h = _top16_desc(sc_ref[lt, 2 * h])
        b_h = _top16_desc(sc_ref[lt, 2 * h + 1])
        a = a_h if h == 0 else [jnp.where(sub == h, new, old) for new, old in zip(a_h, a)]
        b = b_h if h == 0 else [jnp.where(sub == h, new, old) for new, old in zip(b_h, b)]
    lists = [[a[r] + b[c] for c in range(PEER_TOPK // (r + 1))] for r in range(SUBLANES)]
    lists.append([a[r] + b[0] for r in range(SUBLANES, PEER_TOPK)])
    top = lists[0]
    for other in lists[1:-1]:
        top = _bitonic_to_sorted(_merge_bitonic(top, other))
    z = _merge_bitonic(top, lists[-1])
    tau = functools.reduce(jnp.minimum, z)
    best = a[0] + b[0]
    zsum = jnp.zeros_like(tau)
    inf = jnp.full_like(tau, jnp.inf)
    thr_rank = []
    for r, cand in enumerate(lists):
        hits = [v >= tau for v in cand]
        for v, hit in zip(cand, hits):
            zsum = zsum + jnp.where(hit, jnp.exp(v - best), 0.0)
        if r < SUBLANES:
            t = inf
            for c, hit in enumerate(hits):
                t = jnp.where(hit, b[c], t)
            thr_rank.append(t)
        else:
            thr_rank.extend(jnp.where(hit, b[0], inf) for hit in hits)
    inv = 0.5 / zsum
    for h in range(PEER_HEADS):
        own = lambda v: jnp.broadcast_to(v[h:h + 1, :], (SUBLANES, LANES))
        a_h = [own(v) for v in a]
        thr_h = [own(v) for v in thr_rank]
        inv_h, b0_h = own(inv), own(b[0])
        s0 = sc_ref[lt, 2 * h]
        s1 = sc_ref[lt, 2 * h + 1]
        for g in range(s0.shape[0] // SUBLANES):
            rows = slice(g * SUBLANES, (g + 1) * SUBLANES)
            x0 = s0[rows, :]
            thr = jnp.full_like(x0, jnp.inf)
            for r in range(PEER_TOPK):
                thr = jnp.where(x0 == a_h[r], thr_h[r], thr)
            thr_scr[lt, h, rows, :] = thr
            pw_scr[lt, h, rows, :] = jnp.exp(x0 - a_h[0]) * inv_h
            q_scr[lt, h, rows, :] = jnp.exp(s1[rows, :] - b0_h)


MXU_TILE = 256
MXU_COUNT = 2
ACC_ROWS = 512
ACC_PIECE = 32
ACC_PRE = 0
ACC_OUT = ACC_ROWS // 4


def _peer_dense_kernel(sc_ref, hnt_ref, u_ref, vt_ref, res_ref, nw_ref, y_ref, thr_scr, pw_scr, q_scr,
                       h0_scr, h1_scr, g0_scr, g1_scr, yt_ref):
    s = pl.program_id(1)
    n_e = pl.num_programs(1) - 2
    d, tb = hnt_ref.shape
    n_lt, _, nk, _ = sc_ref.shape
    eb = u_ref.shape[0]
    n_i = eb // nk
    assert tb == MXU_COUNT * MXU_TILE and eb % ACC_ROWS == 0 and d % ACC_ROWS == 0

    @pl.when(s == 0)
    def _select():
        def body(lt, carry):
            _peer_select(sc_ref, thr_scr, pw_scr, q_scr, lt)
            return carry
        lax.fori_loop(0, n_lt, body, 0)
        yt_ref[...] = jnp.zeros_like(yt_ref)
        g0_scr[...] = jnp.zeros_like(g0_scr)
        g1_scr[...] = jnp.zeros_like(g1_scr)

    i0 = pl.multiple_of(jnp.clip(s - 1, 0, n_e - 1) * n_i, SUBLANES)

    def gate_steps(lt, ii, h_r, g_w):
        lanes = slice(lt * LANES, (lt + 1) * LANES)
        rows = slice(ii * nk, (ii + 1) * nk)
        state = {"w": jnp.zeros((nk, LANES), F32)}

        def head(h):
            thr = thr_scr[lt, h, pl.ds(i0, n_i), :][ii:ii + 1, :]
            pw = pw_scr[lt, h, pl.ds(i0, n_i), :][ii:ii + 1, :]
            state["w"] = state["w"] + jnp.where(sc_ref[lt, 2 * h + 1] >= thr, q_scr[lt, h], 0.0) * pw
            if h == PEER_HEADS - 1:
                x = h_r[rows, lanes]
                act = x + x * lax.erf(x * (2.0 ** -0.5))
                g_w[rows, lanes] = (state["w"] * act).astype(g_w.dtype)

        return [functools.partial(head, h) for h in range(PEER_HEADS)]

    def mxu_steps(kind, c, k, reg, h_w, g_r):
        lhs_ref, rhs_ref, acc, n_k = ((u_ref, hnt_ref, ACC_PRE, d // MXU_TILE) if kind == "pre"
                                      else (vt_ref, g_r, ACC_OUT, eb // MXU_TILE))
        kc = slice(k * MXU_TILE, (k + 1) * MXU_TILE)

        def push():
            for q in range(MXU_COUNT):
                pltpu.matmul_push_rhs(rhs_ref[kc, q * MXU_TILE:(q + 1) * MXU_TILE], staging_register=reg,
                                      mxu_index=q)

        def piece(p):
            r0 = c * ACC_ROWS + p * ACC_PIECE
            lhs = lhs_ref[r0:r0 + ACC_PIECE, kc]
            for q in range(MXU_COUNT):
                pltpu.matmul_acc_lhs(acc + p * ACC_PIECE // 4, lhs, q, load_staged_rhs=reg if p == 0 else None)

        def pop(p):
            rows = slice(c * ACC_ROWS + p * ACC_PIECE, c * ACC_ROWS + (p + 1) * ACC_PIECE)
            for q in range(MXU_COUNT):
                cols = slice(q * MXU_TILE, (q + 1) * MXU_TILE)
                res = pltpu.matmul_pop(acc + p * ACC_PIECE // 4, (ACC_PIECE, MXU_TILE), F32, q)
                if kind == "pre":
                    h_w[rows, cols] = res
                else:
                    yt_ref[rows, cols] += res

        n_p = ACC_ROWS // ACC_PIECE
        pops = [functools.partial(pop, p) for p in range(n_p)] if k == n_k - 1 else []
        return push, [functools.partial(piece, p) for p in range(n_p)], pops

    def stage(h_w, h_r, g_w, g_r, pre, gating, out):
        pre_groups = [("pre", c, k) for c in range(eb // ACC_ROWS) for k in range(d // MXU_TILE)] if pre else []
        out_groups = [("out", c, k) for c in range(d // ACC_ROWS) for k in range(eb // MXU_TILE)] if out else []
        if pre and out:
            order = [g for pair in zip(pre_groups, out_groups) for g in pair]
        else:
            order = pre_groups + out_groups
        steps = [mxu_steps(kind, c, k, gi % 2, h_w, g_r) for gi, (kind, c, k) in enumerate(order)]
        mxu = [steps[0][0]]
        lagged = []
        for gi, (_, pieces, pops) in enumerate(steps):
            half = len(pieces) // 2
            for p, piece in enumerate(pieces):
                if p == half and gi + 1 < len(steps):
                    mxu.append(steps[gi + 1][0])
                mxu.append(piece)
                if lagged:
                    mxu.append(lagged.pop(0))
            same_acc_next = gi + 1 < len(steps) and order[gi + 1][0] == order[gi][0]
            if same_acc_next or gi + 1 == len(steps):
                mxu.extend(pops)
            else:
                lagged = list(pops)
        vpu = [t for lt in range(n_lt) for ii in range(n_i) for t in gate_steps(lt, ii, h_r, g_w)] if gating else []
        im = iv = 0
        while im < len(mxu) or iv < len(vpu):
            if iv >= len(vpu) or (im < len(mxu) and im * len(vpu) <= iv * len(mxu)):
                mxu[im]()
                im += 1
            else:
                vpu[iv]()
                iv += 1

    last = n_e + 1

    @pl.when(s == 0)
    def _first():
        stage(h0_scr, None, None, None, True, False, False)

    @pl.when((s > 0) & (s < last) & (s % 2 == 0))
    def _even():
        stage(h0_scr, h1_scr, g1_scr, g0_scr, True, True, True)

    @pl.when((s > 0) & (s < last) & (s % 2 == 1))
    def _odd():
        stage(h1_scr, h0_scr, g0_scr, g1_scr, True, True, True)

    def finish(g_r):
        stage(None, None, None, g_r, False, False, True)
        y = res_ref[...] + yt_ref[...].T
        y_ref[...] = y * lax.rsqrt(jnp.mean(y * y, axis=-1, keepdims=True) + EPS) * nw_ref[...]

    @pl.when((s == last) & (s % 2 == 0))
    def _last_even():
        finish(g0_scr)

    @pl.when((s == last) & (s % 2 == 1))
    def _last_odd():
        finish(g1_scr)


def _peer_dense(sc, hnt, u_bf16, vt_bf16, resid, norm_w, tb, eb):
    _, nhc, nk, _ = sc.shape
    d, t = hnt.shape
    n_exp = u_bf16.shape[0]
    n_lt = tb // LANES
    assert eb == SUBLANES * nk and n_exp == nk * nk and t % tb == 0 and tb % LANES == 0
    sel = pltpu.VMEM((n_lt, PEER_HEADS, nk, LANES), F32)
    pre = pltpu.VMEM((eb, tb + LANES), F32)
    gated = pltpu.VMEM((eb, tb), BF16)
    n_e = n_exp // eb
    return pl.pallas_call(
        _peer_dense_kernel,
        grid=(t // tb, n_e + 2),
        in_specs=[pl.BlockSpec((n_lt, nhc, nk, LANES), lambda i, s: (i, 0, 0, 0)),
                  pl.BlockSpec((d, tb), lambda i, s: (0, i)),
                  pl.BlockSpec((eb, d), lambda i, s: (jnp.minimum(s, n_e - 1), 0)),
                  pl.BlockSpec((d, eb), lambda i, s: (0, jnp.clip(s - 2, 0, n_e - 1))),
                  pl.BlockSpec((tb, d), lambda i, s: (i, 0)),
                  pl.BlockSpec((1, d), lambda i, s: (0, 0))],
        out_specs=pl.BlockSpec((tb, d), lambda i, s: (i, 0)),
        out_shape=jax.ShapeDtypeStruct((t, d), F32),
        scratch_shapes=[sel, sel, sel, pre, pre, gated, gated, pltpu.VMEM((d, tb), F32)],
        compiler_params=_cparams("parallel", "arbitrary"),
        name="peer_dense",
    )(sc, hnt, u_bf16, vt_bf16, resid, norm_w.reshape(1, d))


TOKEN_BLOCK = 512
EXPERT_BLOCK = 1024
SAMPLE_T_PAD = 16
SAMPLE_K_PAD = 256


def _ffn(x2d, attn, oh, w, tb):
    h, hnt = _out_proj(x2d, attn, oh, w['wo'], w['norm_ffn'], tb)
    sc = _peer_scores(hnt, w['wqt'], w['keys'], tb)
    return _peer_dense(sc, hnt, w['u'], w['vt'], h, w['norm_final'], tb, EXPERT_BLOCK)


def kernel(x_prompt, x_sample, cache_k_win, cache_v_win, state_hgrn, norm_mix_w, w_in, attn_sinks,
           rel_bias_table, hg_lb, hg_norm_w, w_o, norm_ffn_w, peer_w_q, peer_sub_keys, peer_u, peer_v,
           norm_final_w):
    bsz, seq, d = x_prompt.shape
    dbsz, dseq, _ = x_sample.shape
    aw = ATTN_HEADS * HEAD_DIM
    kw = ATTN_KV_HEADS * HEAD_DIM
    hw = hg_norm_w.shape[1]
    wb = cache_k_win.shape[2]

    wi = w_in[0]
    w_in_r = jnp.concatenate([wi[:, :aw], wi[:, aw + 2 * kw:], wi[:, aw:aw + 2 * kw]], axis=1).astype(BF16)
    col_k = (aw + 4 * hw) // kw
    col_v = col_k + 1
    hg_cols = (1, 2, 3, 4)
    lb = jax.nn.softmax(hg_lb.astype(F32), axis=0)[0]
    nhc = PEER_HEADS * 2
    w = {
        'wo': w_o[0].astype(BF16),
        'norm_ffn': norm_ffn_w[0],
        'wqt': peer_w_q[0].astype(BF16).T,
        'keys': peer_sub_keys[0].reshape(nhc, peer_sub_keys.shape[3], peer_sub_keys.shape[4]).astype(BF16),
        'u': peer_u[0].astype(BF16),
        'vt': peer_v[0].astype(BF16).T,
        'norm_final': norm_final_w,
    }
    sinks = attn_sinks[0].astype(F32)

    proj_p = _in_proj(x_prompt.reshape(bsz * seq, d), norm_mix_w[0], w_in_r, TOKEN_BLOCK)
    proj_p3 = proj_p.reshape(bsz, seq, -1)
    blk = WINDOW
    dist_p = (jnp.arange(blk)[:, None] + blk) - jnp.arange(2 * blk)[None, :]
    bias_p = _masked_bias(rel_bias_table, blk, 2 * blk, blk, (dist_p >= 0) & (dist_p <= WINDOW))
    attn_p = _swa_prompt(proj_p3, sinks, bias_p, 0, col_k, col_v)
    oh_p, st_p = _hgrn(proj_p3, lb, hg_norm_w[0], None, hg_cols, HG_CHUNK, HG_CHUNK, bsz)
    y_p = _ffn(x_prompt.reshape(bsz * seq, d), attn_p.reshape(bsz * seq, aw),
               oh_p.reshape(bsz * seq, hw), w, TOKEN_BLOCK)
    k_off = aw + 4 * hw
    wp = min(WINDOW, seq)
    k_win_p = proj_p3[:, seq - wp:, k_off:k_off + kw].reshape(1, bsz, wp, ATTN_KV_HEADS, HEAD_DIM)
    v_win_p = proj_p3[:, seq - wp:, k_off + kw:k_off + 2 * kw].reshape(1, bsz, wp, ATTN_KV_HEADS, HEAD_DIM)

    tp = SAMPLE_T_PAD
    xs_pad = jnp.pad(x_sample, ((0, 0), (0, tp - dseq), (0, 0)))
    proj_s3 = _in_proj(xs_pad.reshape(dbsz * tp, d), norm_mix_w[0], w_in_r, TOKEN_BLOCK).reshape(dbsz, tp, -1)
    k_new = proj_s3[:, :dseq, k_off:k_off + kw]
    v_new = proj_s3[:, :dseq, k_off + kw:k_off + 2 * kw]
    kk = jnp.concatenate([cache_k_win[0].reshape(dbsz, wb, kw), k_new], axis=1)
    vv = jnp.concatenate([cache_v_win[0].reshape(dbsz, wb, kw), v_new], axis=1)
    kpad = SAMPLE_K_PAD - (wb + dseq)
    kk_pad = jnp.pad(kk, ((0, 0), (0, kpad), (0, 0)))
    vv_pad = jnp.pad(vv, ((0, 0), (0, kpad), (0, 0)))
    dist_s = (wb + jnp.arange(tp))[:, None] - jnp.arange(SAMPLE_K_PAD)[None, :]
    mask_s = ((dist_s >= 0) & (dist_s <= WINDOW) & (jnp.arange(tp)[:, None] < dseq)
              & (jnp.arange(SAMPLE_K_PAD)[None, :] < wb + dseq))
    bias_s = _masked_bias(rel_bias_table, tp, SAMPLE_K_PAD, wb, mask_s)
    attn_s = _swa_sample(proj_s3, kk_pad, vv_pad, sinks, bias_s, 0, 16)
    oh_s, st_s = _hgrn(proj_s3, lb, hg_norm_w[0], state_hgrn[0], hg_cols, tp, dseq, 8)
    y_s = _ffn(x_sample.reshape(dbsz * dseq, d), attn_s[:, :dseq].reshape(dbsz * dseq, aw),
               oh_s[:, :dseq].reshape(dbsz * dseq, hw), w, TOKEN_BLOCK)
    k_win_s = kk[:, dseq:].reshape(1, dbsz, wb, ATTN_KV_HEADS, HEAD_DIM)
    v_win_s = vv[:, dseq:].reshape(1, dbsz, wb, ATTN_KV_HEADS, HEAD_DIM)

    return (y_p.reshape(bsz, seq, d), y_s.reshape(dbsz, dseq, d), k_win_p, v_win_p, st_p[None],
            k_win_s, v_win_s, st_s[None])
```

```python
import functools
import math

import jax
import jax.numpy as jnp
from jax import lax
from jax.experimental import pallas as pl
from jax.experimental.pallas import tpu as pltpu

F32 = jnp.float32
BF16 = jnp.bfloat16

EPS = 1e-6
NEG = -1e30

ATTN_HEADS = 8
ATTN_KV_HEADS = 2
HEAD_DIM = 64
WINDOW = 128
REL_BUCKETS = 32
HG_HEADS = 4
HG_CHUNK = 64
PEER_HEADS = 8
PEER_TOPK = 16

LANES = 128
SUBLANES = 8
VMEM_LIMIT = 56 * 1024 * 1024


def _cparams(*sem):
    return pltpu.CompilerParams(dimension_semantics=sem, vmem_limit_bytes=VMEM_LIMIT)


def _nt(a, b):
    return lax.dot_general(a, b, (((1,), (1,)), ((), ())), preferred_element_type=F32)


def _tn(a, b):
    return lax.dot_general(a, b, (((0,), (0,)), ((), ())), preferred_element_type=F32)


def _dot(a, b):
    return jnp.dot(a, b, preferred_element_type=F32)


def _sigmoid(x):
    return 1.0 / (1.0 + jnp.exp(-x))


def _in_proj_kernel(x_ref, nw_ref, w_ref, o_ref):
    x = x_ref[...]
    xn = x * lax.rsqrt(jnp.mean(x * x, axis=-1, keepdims=True) + EPS) * nw_ref[...]
    o_ref[...] = _dot(xn.astype(BF16), w_ref[...])


def _in_proj(x2d, norm_w, w_bf16, tm):
    t, d = x2d.shape
    n = w_bf16.shape[1]
    return pl.pallas_call(
        _in_proj_kernel,
        grid=(t // tm,),
        in_specs=[pl.BlockSpec((tm, d), lambda i: (i, 0)),
                  pl.BlockSpec((1, d), lambda i: (0, 0)),
                  pl.BlockSpec((d, n), lambda i: (0, 0))],
        out_specs=pl.BlockSpec((tm, n), lambda i: (i, 0)),
        out_shape=jax.ShapeDtypeStruct((t, n), F32),
        compiler_params=_cparams("parallel"),
        name="in_proj",
    )(x2d, norm_w.reshape(1, d), w_bf16)


def _t5_bucket(dist):
    n = jnp.maximum(dist, 0)
    max_exact = REL_BUCKETS // 2
    nf = jnp.maximum(n, 1).astype(F32)
    large = max_exact + (jnp.log(nf / max_exact) / math.log(WINDOW / max_exact)
                         * (REL_BUCKETS - max_exact)).astype(jnp.int32)
    large = jnp.minimum(large, REL_BUCKETS - 1)
    return jnp.where(n < max_exact, n, large)


def _masked_bias(table, n_q, n_k, offset, mask):
    h = table.shape[1]
    diag = jnp.arange(n_q + n_k - 1) - (n_k - 1) + offset
    per_diag = table.astype(F32)[_t5_bucket(diag)].T
    w = jnp.pad(per_diag[:, ::-1], ((0, 0), (0, 1)))
    p = n_q + n_k
    skew = jnp.tile(w, (1, n_q))[:, :n_q * (p - 1)].reshape(h, n_q, p - 1)
    return jnp.where(mask[None], skew[:, :, n_q - 1:n_q - 1 + n_k], NEG)


def _swa_prompt_kernel(sink_ref, q_ref, kp_ref, kc_ref, vp_ref, vc_ref, bias_ref, o_ref):
    first = pl.program_id(1) == 0
    scale = HEAD_DIM ** -0.5
    group = ATTN_HEADS // ATTN_KV_HEADS
    blk = q_ref.shape[1]
    assert 2 * HEAD_DIM == LANES and kp_ref.shape[2] == LANES and ATTN_KV_HEADS == 2 and group % 2 == 0
    kk = jnp.concatenate([kp_ref[0], kc_ref[0]], axis=0)
    vv = jnp.concatenate([vp_ref[0], vc_ref[0]], axis=0)
    low = lax.broadcasted_iota(jnp.int32, kk.shape, 1) < HEAD_DIM
    col = lax.broadcasted_iota(jnp.int32, (blk, 2 * blk), 1)
    no_prev = (col < blk) & first
    def halves(x, kvh):
        own = jnp.where(low if kvh == 0 else ~low, x, 0.0)
        other = pltpu.roll(own, HEAD_DIM, axis=1)
        lo, hi = (own, other) if kvh == 0 else (other, own)
        return lo.astype(BF16), hi.astype(BF16)

    k_half = [halves(kk, kvh) for kvh in range(ATTN_KV_HEADS)]
    v_half = [halves(vv, kvh) for kvh in range(ATTN_KV_HEADS)]
    heads = range(ATTN_HEADS)
    kv_of = lambda h: h // group
    s = [_nt(q_ref[0, :, (h // 2) * LANES:(h // 2 + 1) * LANES].astype(BF16), k_half[kv_of(h)][h % 2])
         for h in heads]
    s = [jnp.where(no_prev, NEG, s[h] * scale + bias_ref[h]) for h in heads]
    m = [jnp.maximum(jnp.max(s[h], axis=-1, keepdims=True), sink_ref[h]) for h in heads]
    p = [jnp.exp(s[h] - m[h]) for h in heads]
    den = [jnp.sum(p[h], axis=-1, keepdims=True) + jnp.exp(sink_ref[h] - m[h]) for h in heads]
    p = [(p[h] * (1.0 / den[h])).astype(BF16) for h in heads]
    for tile in range(ATTN_HEADS // 2):
        kvh = kv_of(2 * tile)
        o = _dot(p[2 * tile], v_half[kvh][0]) + _dot(p[2 * tile + 1], v_half[kvh][1])
        o_ref[0, :, tile * LANES:(tile + 1) * LANES] = o.astype(o_ref.dtype)


def _swa_prompt(proj3, sinks, bias, col_q, col_k, col_v):
    bsz, seq, _ = proj3.shape
    blk = WINDOW
    aw = ATTN_HEADS * HEAD_DIM
    kw = ATTN_KV_HEADS * HEAD_DIM
    prev = lambda b, n: (b, jnp.maximum(n - 1, 0))
    return pl.pallas_call(
        _swa_prompt_kernel,
        grid=(bsz, seq // blk),
        in_specs=[pl.BlockSpec(memory_space=pltpu.SMEM),
                  pl.BlockSpec((1, blk, aw), lambda b, n: (b, n, col_q)),
                  pl.BlockSpec((1, blk, kw), lambda b, n: prev(b, n) + (col_k,)),
                  pl.BlockSpec((1, blk, kw), lambda b, n: (b, n, col_k)),
                  pl.BlockSpec((1, blk, kw), lambda b, n: prev(b, n) + (col_v,)),
                  pl.BlockSpec((1, blk, kw), lambda b, n: (b, n, col_v)),
                  pl.BlockSpec((ATTN_HEADS, blk, 2 * blk), lambda b, n: (0, 0, 0))],
        out_specs=pl.BlockSpec((1, blk, aw), lambda b, n: (b, n, 0)),
        out_shape=jax.ShapeDtypeStruct((bsz, seq, aw), BF16),
        compiler_params=_cparams("parallel", "arbitrary"),
        name="swa_prompt",
    )(sinks, proj3, proj3, proj3, proj3, proj3, bias)


def _swa_sample_kernel(sink_ref, q_ref, kn_ref, vn_ref, kc_ref, vc_ref, bias_c_ref, bias_n_ref, o_ref):
    scale = HEAD_DIM ** -0.5
    group = ATTN_HEADS // ATTN_KV_HEADS
    for h in range(ATTN_HEADS):
        kv = slice((h // group) * HEAD_DIM, (h // group + 1) * HEAD_DIM)
        qh = q_ref[:, :, h * HEAD_DIM:(h + 1) * HEAD_DIM].astype(BF16)
        sc = jnp.einsum('bqd,bkd->bqk', qh, kc_ref[:, :, kv].astype(BF16),
                        preferred_element_type=F32) * scale + bias_c_ref[h][None]
        sn = jnp.einsum('bqd,bkd->bqk', qh, kn_ref[:, :, kv].astype(BF16),
                        preferred_element_type=F32) * scale + bias_n_ref[h][None]
        sink = sink_ref[h]
        m = jnp.maximum(jnp.maximum(jnp.max(sc, axis=-1, keepdims=True),
                                    jnp.max(sn, axis=-1, keepdims=True)), sink)
        pc = jnp.exp(sc - m)
        pn = jnp.exp(sn - m)
        den = (jnp.sum(pc, axis=-1, keepdims=True) + jnp.sum(pn, axis=-1, keepdims=True)
               + jnp.exp(sink - m))
        o = (jnp.einsum('bqk,bkd->bqd', pc.astype(BF16), vc_ref[:, :, kv].astype(BF16), preferred_element_type=F32)
             + jnp.einsum('bqk,bkd->bqd', pn.astype(BF16), vn_ref[:, :, kv].astype(BF16),
                          preferred_element_type=F32)) / den
        o_ref[:, :, h * HEAD_DIM:(h + 1) * HEAD_DIM] = o.astype(o_ref.dtype)


def _swa_sample(proj3, k_cache, v_cache, sinks, bias_c, bias_n, col_q, col_k, col_v, bb):
    bsz, tp, _ = proj3.shape
    wb = k_cache.shape[1]
    aw = ATTN_HEADS * HEAD_DIM
    kw = ATTN_KV_HEADS * HEAD_DIM
    return pl.pallas_call(
        _swa_sample_kernel,
        grid=(bsz // bb,),
        in_specs=[pl.BlockSpec(memory_space=pltpu.SMEM),
                  pl.BlockSpec((bb, tp, aw), lambda b: (b, 0, col_q)),
                  pl.BlockSpec((bb, tp, kw), lambda b: (b, 0, col_k)),
                  pl.BlockSpec((bb, tp, kw), lambda b: (b, 0, col_v)),
                  pl.BlockSpec((bb, wb, kw), lambda b: (b, 0, 0)),
                  pl.BlockSpec((bb, wb, kw), lambda b: (b, 0, 0)),
                  pl.BlockSpec((ATTN_HEADS, tp, wb), lambda b: (0, 0, 0)),
                  pl.BlockSpec((ATTN_HEADS, tp, tp), lambda b: (0, 0, 0))],
        out_specs=pl.BlockSpec((bb, tp, aw), lambda b: (b, 0, 0)),
        out_shape=jax.ShapeDtypeStruct((bsz, tp, aw), BF16),
        compiler_params=_cparams("parallel"),
        name="swa_sample",
    )(sinks, proj3, proj3, proj3, k_cache, v_cache, bias_c, bias_n)


def _split3(x):
    hi = x.astype(BF16)
    r = x - hi.astype(F32)
    mid = r.astype(BF16)
    lo = (r - mid.astype(F32)).astype(BF16)
    return hi, mid, lo


def _hgrn_kernel(*refs, t_valid, has_state):
    if has_state:
        q_ref, f_ref, i_ref, g_ref, lb_ref, nw_ref, s0_ref, o_ref, s_ref, st_scr = refs
    else:
        q_ref, f_ref, i_ref, g_ref, lb_ref, nw_ref, o_ref, s_ref, st_scr = refs
    bb, chunk, width = q_ref.shape
    dk = width // HG_HEADS
    c = pl.program_id(1)

    @pl.when(c == 0)
    def _init():
        if has_state:
            st_scr[...] = s0_ref[...]
        else:
            st_scr[...] = jnp.zeros_like(st_scr)

    row = lax.broadcasted_iota(jnp.int32, (chunk, chunk), 0)
    col = lax.broadcasted_iota(jnp.int32, (chunk, chunk), 1)
    causal = row >= col
    tri = jnp.where(causal, 1.0, 0.0).astype(BF16)
    row_w = lax.broadcasted_iota(jnp.int32, (chunk, width), 0)
    valid = row_w < t_valid
    ones_rows = jnp.where(lax.broadcasted_iota(jnp.int32, (chunk, dk), 0) < 3, 1.0, 0.0).astype(BF16)
    mid_row = chunk // 2

    def body(b, carry):
        qx = q_ref[b]
        q = qx * _sigmoid(qx)
        lb = lb_ref[...]
        f = lb + (1.0 - lb) * _sigmoid(f_ref[b])
        k = 1.0 - f
        lg = jnp.log(f)
        if t_valid < chunk:
            k = jnp.where(valid, k, 0.0)
            lg = jnp.where(valid, lg, 0.0)
        v = i_ref[b].astype(BF16)
        cum = sum(_dot(tri, part) for part in _split3(lg))
        cum_mid = cum[mid_row:mid_row + 1, :]
        cum_last = cum[chunk - 1:chunk, :]
        qt = (q * jnp.exp(cum - cum_mid)).astype(BF16)
        kt = (k * jnp.exp(cum_mid - cum)).astype(BF16)
        qe = (q * jnp.exp(cum)).astype(BF16)
        kd = (k * jnp.exp(cum_last - cum)).astype(BF16)
        dec_rows = jnp.zeros((chunk, width), F32)
        for j, part in enumerate(_split3(jnp.exp(cum_last))):
            dec_rows = jnp.where(row_w == j, part.astype(F32), dec_rows)
        dec_rows = dec_rows.astype(BF16)
        heads = range(HG_HEADS)
        sl = [slice(h * dk, (h + 1) * dk) for h in heads]
        st = [st_scr[b, h] for h in heads]
        a = [_nt(qt[:, sl[h]], kt[:, sl[h]]) for h in heads]
        inter = [_dot(qe[:, sl[h]], st[h].astype(BF16)) for h in heads]
        upd = [_tn(kd[:, sl[h]], v[:, sl[h]]) for h in heads]
        decay = [_tn(dec_rows[:, sl[h]], ones_rows) for h in heads]
        for h in heads:
            st_scr[b, h] = st[h] * decay[h] + upd[h]
        a = [jnp.where(causal, a[h], 0.0).astype(BF16) for h in heads]
        o = [_dot(a[h], v[:, sl[h]]) + inter[h] for h in heads]
        outs = [o[h] * lax.rsqrt(jnp.mean(o[h] * o[h], axis=-1, keepdims=True) + EPS) for h in heads]
        gx = g_ref[b]
        o = jnp.concatenate(outs, axis=1) * nw_ref[...] * (gx * _sigmoid(gx))
        o_ref[b] = o.astype(o_ref.dtype)
        return carry

    lax.fori_loop(0, bb, body, 0, unroll=2)

    @pl.when(c == pl.num_programs(1) - 1)
    def _final():
        s_ref[...] = st_scr[...]


def _hgrn(proj3, lb, norm_w, s0, cols, chunk, t_valid, bb):
    bsz, t, _ = proj3.shape
    width = lb.shape[0]
    dk = width // HG_HEADS
    has_state = s0 is not None
    spec = lambda cb: pl.BlockSpec((bb, chunk, width), lambda b, c: (b, c, cb))
    vec = pl.BlockSpec((1, width), lambda b, c: (0, 0))
    st_spec = pl.BlockSpec((bb, HG_HEADS, dk, dk), lambda b, c: (b, 0, 0, 0))
    in_specs = [spec(cols[0]), spec(cols[1]), spec(cols[2]), spec(cols[3]), vec, vec]
    args = [proj3, proj3, proj3, proj3, lb.reshape(1, width), norm_w.reshape(1, width)]
    if has_state:
        in_specs.append(st_spec)
        args.append(s0)
    return pl.pallas_call(
        functools.partial(_hgrn_kernel, t_valid=t_valid, has_state=has_state),
        grid=(bsz // bb, t // chunk),
        in_specs=in_specs,
        out_specs=[pl.BlockSpec((bb, chunk, width), lambda b, c: (b, c, 0)), st_spec],
        out_shape=[jax.ShapeDtypeStruct((bsz, t, width), BF16),
                   jax.ShapeDtypeStruct((bsz, HG_HEADS, dk, dk), F32)],
        scratch_shapes=[pltpu.VMEM((bb, HG_HEADS, dk, dk), F32)],
        compiler_params=_cparams("parallel", "arbitrary"),
        name="hgrn_state" if has_state else "hgrn_prompt",
    )(*args)


def _out_proj_kernel(x_ref, a_ref, oh_ref, wo_ref, nw_ref, h_ref, hnt_ref):
    aw = a_ref.shape[1]
    mix = _dot(a_ref[...], wo_ref[:aw, :]) + _dot(oh_ref[...], wo_ref[aw:, :])
    h = x_ref[...] + mix
    h_ref[...] = h
    hn = h * lax.rsqrt(jnp.mean(h * h, axis=-1, keepdims=True) + EPS) * nw_ref[...]
    hnt_ref[...] = hn.T.astype(hnt_ref.dtype)


def _out_proj(x2d, attn, oh, wo_bf16, norm_w, tm):
    t, d = x2d.shape
    aw, hw = attn.shape[1], oh.shape[1]
    return pl.pallas_call(
        _out_proj_kernel,
        grid=(t // tm,),
        in_specs=[pl.BlockSpec((tm, d), lambda i: (i, 0)),
                  pl.BlockSpec((tm, aw), lambda i: (i, 0)),
                  pl.BlockSpec((tm, hw), lambda i: (i, 0)),
                  pl.BlockSpec((aw + hw, d), lambda i: (0, 0)),
                  pl.BlockSpec((1, d), lambda i: (0, 0))],
        out_specs=[pl.BlockSpec((tm, d), lambda i: (i, 0)),
                   pl.BlockSpec((d, tm), lambda i: (0, i))],
        out_shape=[jax.ShapeDtypeStruct((t, d), F32),
                   jax.ShapeDtypeStruct((d, t), BF16)],
        compiler_params=_cparams("parallel"),
        name="out_proj",
    )(x2d, attn, oh, wo_bf16, norm_w.reshape(1, d))


def _peer_scores_kernel(hnt_ref, wqt_ref, keys_ref, sc_ref):
    qt = _dot(wqt_ref[...], hnt_ref[...])
    half = keys_ref.shape[2]
    for hc in range(keys_ref.shape[0]):
        s = _dot(keys_ref[hc], qt[hc * half:(hc + 1) * half, :].astype(BF16))
        for lt in range(sc_ref.shape[0]):
            sc_ref[lt, hc] = s[:, lt * LANES:(lt + 1) * LANES]


def _peer_scores(hnt, wqt_bf16, keys_bf16, tb):
    d, t = hnt.shape
    nhc, nk, half = keys_bf16.shape
    return pl.pallas_call(
        _peer_scores_kernel,
        grid=(t // tb,),
        in_specs=[pl.BlockSpec((d, tb), lambda i: (0, i)),
                  pl.BlockSpec((nhc * half, d), lambda i: (0, 0)),
                  pl.BlockSpec((nhc, nk, half), lambda i: (0, 0, 0))],
        out_specs=pl.BlockSpec((tb // LANES, nhc, nk, LANES), lambda i: (i, 0, 0, 0)),
        out_shape=jax.ShapeDtypeStruct((t // LANES, nhc, nk, LANES), F32),
        compiler_params=_cparams("parallel"),
        name="peer_scores",
    )(hnt, wqt_bf16, keys_bf16)


def _sort16_pairs():
    def merge(lo, hi, r):
        step = r * 2
        if step < hi - lo:
            yield from merge(lo, hi, step)
            yield from merge(lo + r, hi, step)
            yield from [(i, i + r) for i in range(lo + r, hi - r, step)]
        else:
            yield (lo, lo + r)

    def sort(lo, hi):
        if hi - lo >= 1:
            mid = lo + (hi - lo) // 2
            yield from sort(lo, mid)
            yield from sort(mid + 1, hi)
            yield from merge(lo, hi, 1)

    return tuple(sort(0, PEER_TOPK - 1))


_SORT16 = _sort16_pairs()


def _bitonic_to_sorted(z):
    z = list(z)
    d = PEER_TOPK // 2
    while d >= 1:
        for i in range(PEER_TOPK):
            if i & d == 0:
                hi, lo = jnp.maximum(z[i], z[i + d]), jnp.minimum(z[i], z[i + d])
                z[i], z[i + d] = hi, lo
        d //= 2
    return z


def _merge_bitonic(top, other):
    z = list(top)
    m = len(other)
    for r in range(PEER_TOPK - m, PEER_TOPK):
        z[r] = jnp.maximum(top[r], other[PEER_TOPK - 1 - r])
    return z


def _top16_desc(x):
    n = x.shape[0] // SUBLANES
    xs = [x[g * SUBLANES:(g + 1) * SUBLANES, :] for g in range(n)]
    for i, j in _SORT16:
        xs[i], xs[j] = jnp.maximum(xs[i], xs[j]), jnp.minimum(xs[i], xs[j])
    shift = SUBLANES // 2
    while shift >= 1:
        ys = [pltpu.roll(v, shift, axis=0) for v in xs]
        xs = _bitonic_to_sorted(_merge_bitonic(xs, ys))
        shift //= 2
    return xs


def _peer_select(sc_ref, thr_scr, pw_scr, q_scr, lt):
    assert PEER_HEADS == SUBLANES
    sub = lax.broadcasted_iota(jnp.int32, (SUBLANES, LANES), 0)
    a = b = None
    for h in range(PEER_HEADS):
        a_h = _top16_desc(sc_ref[lt, 2 * h])
        b_h = _top16_desc(sc_ref[lt, 2 * h + 1])
        a = a_h if h == 0 else [jnp.where(sub == h, new, old) for new, old in zip(a_h, a)]
        b = b_h if h == 0 else [jnp.where(sub == h, new, old) for new, old in zip(b_h, b)]
    lists = [[a[r] + b[c] for c in range(PEER_TOPK // (r + 1))] for r in range(SUBLANES)]
    lists.append([a[r] + b[0] for r in range(SUBLANES, PEER_TOPK)])
    top = lists[0]
    for other in lists[1:-1]:
        top = _bitonic_to_sorted(_merge_bitonic(top, other))
    z = _merge_bitonic(top, lists[-1])
    tau = functools.reduce(jnp.minimum, z)
    best = a[0] + b[0]
    zsum = jnp.zeros_like(tau)
    inf = jnp.full_like(tau, jnp.inf)
    thr_rank = []
    for r, cand in enumerate(lists):
        hits = [v >= tau for v in cand]
        for v, hit in zip(cand, hits):
            zsum = zsum + jnp.where(hit, jnp.exp(v - best), 0.0)
        if r < SUBLANES:
            t = inf
            for c, hit in enumerate(hits):
                t = jnp.where(hit, b[c], t)
            thr_rank.append(t)
        else:
            thr_rank.extend(jnp.where(hit, b[0], inf) for hit in hits)
    inv = 0.5 / zsum
    for h in range(PEER_HEADS):
        own = lambda v: jnp.broadcast_to(v[h:h + 1, :], (SUBLANES, LANES))
        a_h = [own(v) for v in a]
        thr_h = [own(v) for v in thr_rank]
        inv_h, b0_h = own(inv), own(b[0])
        s0 = sc_ref[lt, 2 * h]
        s1 = sc_ref[lt, 2 * h + 1]
        for g in range(s0.shape[0] // SUBLANES):
            rows = slice(g * SUBLANES, (g + 1) * SUBLANES)
            x0 = s0[rows, :]
            thr = jnp.full_like(x0, jnp.inf)
            for r in range(PEER_TOPK):
                thr = jnp.where(x0 == a_h[r], thr_h[r], thr)
            thr_scr[lt, h, rows, :] = thr
            pw_scr[lt, h, rows, :] = jnp.exp(x0 - a_h[0]) * inv_h
            q_scr[lt, h, rows, :] = jnp.exp(s1[rows, :] - b0_h)


MXU_TILE = 256
MXU_COUNT = 2
ACC_ROWS = 512
ACC_PIECE = 32
ACC_PRE = 0
ACC_OUT = ACC_ROWS // 4


def _peer_dense_kernel(sc_ref, hnt_ref, u_ref, vt_ref, res_ref, nw_ref, y_ref, thr_scr, pw_scr, q_scr,
                       h0_scr, h1_scr, g0_scr, g1_scr, yt_ref):
    s = pl.program_id(1)
    n_e = pl.num_programs(1) - 2
    d, tb = hnt_ref.shape
    n_lt, _, nk, _ = sc_ref.shape
    eb = u_ref.shape[0]
    n_i = eb // nk
    assert tb == MXU_COUNT * MXU_TILE and eb % ACC_ROWS == 0 and d % ACC_ROWS == 0

    @pl.when(s == 0)
    def _select():
        def body(lt, carry):
            _peer_select(sc_ref, thr_scr, pw_scr, q_scr, lt)
            return carry
        lax.fori_loop(0, n_lt, body, 0)
        yt_ref[...] = jnp.zeros_like(yt_ref)
        g0_scr[...] = jnp.zeros_like(g0_scr)
        g1_scr[...] = jnp.zeros_like(g1_scr)

    i0 = pl.multiple_of(jnp.clip(s - 1, 0, n_e - 1) * n_i, SUBLANES)

    def gate_steps(lt, ii, h_r, g_w):
        lanes = slice(lt * LANES, (lt + 1) * LANES)
        rows = slice(ii * nk, (ii + 1) * nk)
        state = {"w": jnp.zeros((nk, LANES), F32)}

        def head(h):
            thr = thr_scr[lt, h, pl.ds(i0, n_i), :][ii:ii + 1, :]
            pw = pw_scr[lt, h, pl.ds(i0, n_i), :][ii:ii + 1, :]
            state["w"] = state["w"] + jnp.where(sc_ref[lt, 2 * h + 1] >= thr, q_scr[lt, h], 0.0) * pw
            if h == PEER_HEADS - 1:
                x = h_r[rows, lanes]
                act = x + x * lax.erf(x * (2.0 ** -0.5))
                g_w[rows, lanes] = (state["w"] * act).astype(g_w.dtype)

        return [functools.partial(head, h) for h in range(PEER_HEADS)]

    def mxu_steps(kind, c, k, reg, h_w, g_r):
        lhs_ref, rhs_ref, acc, n_k = ((u_ref, hnt_ref, ACC_PRE, d // MXU_TILE) if kind == "pre"
                                      else (vt_ref, g_r, ACC_OUT, eb // MXU_TILE))
        kc = slice(k * MXU_TILE, (k + 1) * MXU_TILE)

        def push():
            for q in range(MXU_COUNT):
                pltpu.matmul_push_rhs(rhs_ref[kc, q * MXU_TILE:(q + 1) * MXU_TILE], staging_register=reg,
                                      mxu_index=q)

        def piece(p):
            r0 = c * ACC_ROWS + p * ACC_PIECE
            lhs = lhs_ref[r0:r0 + ACC_PIECE, kc]
            for q in range(MXU_COUNT):
                pltpu.matmul_acc_lhs(acc + p * ACC_PIECE // 4, lhs, q, load_staged_rhs=reg if p == 0 else None)

        def pop(p):
            rows = slice(c * ACC_ROWS + p * ACC_PIECE, c * ACC_ROWS + (p + 1) * ACC_PIECE)
            for q in range(MXU_COUNT):
                cols = slice(q * MXU_TILE, (q + 1) * MXU_TILE)
                res = pltpu.matmul_pop(acc + p * ACC_PIECE // 4, (ACC_PIECE, MXU_TILE), F32, q)
                if kind == "pre":
                    h_w[rows, cols] = res
                else:
                    yt_ref[rows, cols] += res

        n_p = ACC_ROWS // ACC_PIECE
        pops = [functools.partial(pop, p) for p in range(n_p)] if k == n_k - 1 else []
        return push, [functools.partial(piece, p) for p in range(n_p)], pops

    def stage(h_w, h_r, g_w, g_r, pre, gating, out):
        pre_groups = [("pre", c, k) for c in range(eb // ACC_ROWS) for k in range(d // MXU_TILE)] if pre else []
        out_groups = [("out", c, k) for c in range(d // ACC_ROWS) for k in range(eb // MXU_TILE)] if out else []
        if pre and out:
            order = [g for pair in zip(pre_groups, out_groups) for g in pair]
        else:
            order = pre_groups + out_groups
        steps = [mxu_steps(kind, c, k, gi % 2, h_w, g_r) for gi, (kind, c, k) in enumerate(order)]
        mxu = [steps[0][0]]
        lagged = []
        for gi, (_, pieces, pops) in enumerate(steps):
            half = len(pieces) // 2
            for p, piece in enumerate(pieces):
                if p == half and gi + 1 < len(steps):
                    mxu.append(steps[gi + 1][0])
                mxu.append(piece)
                if lagged:
                    mxu.append(lagged.pop(0))
            same_acc_next = gi + 1 < len(steps) and order[gi + 1][0] == order[gi][0]
            if same_acc_next or gi + 1 == len(steps):
                mxu.extend(pops)
            else:
                lagged = list(pops)
        vpu = [t for lt in range(n_lt) for ii in range(n_i) for t in gate_steps(lt, ii, h_r, g_w)] if gating else []
        im = iv = 0
        while im < len(mxu) or iv < len(vpu):
            if iv >= len(vpu) or (im < len(mxu) and im * len(vpu) <= iv * len(mxu)):
                mxu[im]()
                im += 1
            else:
                vpu[iv]()
                iv += 1

    last = n_e + 1

    @pl.when(s == 0)
    def _first():
        stage(h0_scr, None, None, None, True, False, False)

    @pl.when((s > 0) & (s < last) & (s % 2 == 0))
    def _even():
        stage(h0_scr, h1_scr, g1_scr, g0_scr, True, True, True)

    @pl.when((s > 0) & (s < last) & (s % 2 == 1))
    def _odd():
        stage(h1_scr, h0_scr, g0_scr, g1_scr, True, True, True)

    def finish(g_r):
        stage(None, None, None, g_r, False, False, True)
        y = res_ref[...] + yt_ref[...].T
        y_ref[...] = y * lax.rsqrt(jnp.mean(y * y, axis=-1, keepdims=True) + EPS) * nw_ref[...]

    @pl.when((s == last) & (s % 2 == 0))
    def _last_even():
        finish(g0_scr)

    @pl.when((s == last) & (s % 2 == 1))
    def _last_odd():
        finish(g1_scr)


def _peer_dense(sc, hnt, u_bf16, vt_bf16, resid, norm_w, tb, eb):
    _, nhc, nk, _ = sc.shape
    d, t = hnt.shape
    n_exp = u_bf16.shape[0]
    n_lt = tb // LANES
    assert eb == SUBLANES * nk and n_exp == nk * nk and t % tb == 0 and tb % LANES == 0
    sel = pltpu.VMEM((n_lt, PEER_HEADS, nk, LANES), F32)
    pre = pltpu.VMEM((eb, tb + LANES), F32)
    gated = pltpu.VMEM((eb, tb), BF16)
    n_e = n_exp // eb
    return pl.pallas_call(
        _peer_dense_kernel,
        grid=(t // tb, n_e + 2),
        in_specs=[pl.BlockSpec((n_lt, nhc, nk, LANES), lambda i, s: (i, 0, 0, 0)),
                  pl.BlockSpec((d, tb), lambda i, s: (0, i)),
                  pl.BlockSpec((eb, d), lambda i, s: (jnp.minimum(s, n_e - 1), 0)),
                  pl.BlockSpec((d, eb), lambda i, s: (0, jnp.clip(s - 2, 0, n_e - 1))),
                  pl.BlockSpec((tb, d), lambda i, s: (i, 0)),
                  pl.BlockSpec((1, d), lambda i, s: (0, 0))],
        out_specs=pl.BlockSpec((tb, d), lambda i, s: (i, 0)),
        out_shape=jax.ShapeDtypeStruct((t, d), F32),
        scratch_shapes=[sel, sel, sel, pre, pre, gated, gated, pltpu.VMEM((d, tb), F32)],
        compiler_params=_cparams("parallel", "arbitrary"),
        name="peer_dense",
    )(sc, hnt, u_bf16, vt_bf16, resid, norm_w.reshape(1, d))


TOKEN_BLOCK = 512
EXPERT_BLOCK = 1024
SAMPLE_T_PAD = 16


def _ffn(x2d, attn, oh, w, tb):
    h, hnt = _out_proj(x2d, attn, oh, w['wo'], w['norm_ffn'], tb)
    sc = _peer_scores(hnt, w['wqt'], w['keys'], tb)
    return _peer_dense(sc, hnt, w['u'], w['vt'], h, w['norm_final'], tb, EXPERT_BLOCK)


def kernel(x_prompt, x_sample, cache_k_win, cache_v_win, state_hgrn, norm_mix_w, w_in, attn_sinks,
           rel_bias_table, hg_lb, hg_norm_w, w_o, norm_ffn_w, peer_w_q, peer_sub_keys, peer_u, peer_v,
           norm_final_w):
    bsz, seq, d = x_prompt.shape
    dbsz, dseq, _ = x_sample.shape
    aw = ATTN_HEADS * HEAD_DIM
    kw = ATTN_KV_HEADS * HEAD_DIM
    hw = hg_norm_w.shape[1]
    wb = cache_k_win.shape[2]

    wi = w_in[0]
    w_in_r = jnp.concatenate([wi[:, :aw], wi[:, aw + 2 * kw:], wi[:, aw:aw + 2 * kw]], axis=1).astype(BF16)
    col_k = (aw + 4 * hw) // kw
    col_v = col_k + 1
    hg_cols = (1, 2, 3, 4)
    lb = jax.nn.softmax(hg_lb.astype(F32), axis=0)[0]
    nhc = PEER_HEADS * 2
    w = {
        'wo': w_o[0].astype(BF16),
        'norm_ffn': norm_ffn_w[0],
        'wqt': peer_w_q[0].astype(BF16).T,
        'keys': peer_sub_keys[0].reshape(nhc, peer_sub_keys.shape[3], peer_sub_keys.shape[4]).astype(BF16),
        'u': peer_u[0].astype(BF16),
        'vt': peer_v[0].astype(BF16).T,
        'norm_final': norm_final_w,
    }
    sinks = attn_sinks[0].astype(F32)

    proj_p = _in_proj(x_prompt.reshape(bsz * seq, d), norm_mix_w[0], w_in_r, TOKEN_BLOCK)
    proj_p3 = proj_p.reshape(bsz, seq, -1)
    blk = WINDOW
    dist_p = (jnp.arange(blk)[:, None] + blk) - jnp.arange(2 * blk)[None, :]
    bias_p = _masked_bias(rel_bias_table, blk, 2 * blk, blk, (dist_p >= 0) & (dist_p <= WINDOW))
    attn_p = _swa_prompt(proj_p3, sinks, bias_p, 0, col_k, col_v)
    oh_p, st_p = _hgrn(proj_p3, lb, hg_norm_w[0], None, hg_cols, HG_CHUNK, HG_CHUNK, bsz)
    y_p = _ffn(x_prompt.reshape(bsz * seq, d), attn_p.reshape(bsz * seq, aw),
               oh_p.reshape(bsz * seq, hw), w, TOKEN_BLOCK)
    k_off = aw + 4 * hw
    wp = min(WINDOW, seq)
    k_win_p = proj_p3[:, seq - wp:, k_off:k_off + kw].reshape(1, bsz, wp, ATTN_KV_HEADS, HEAD_DIM)
    v_win_p = proj_p3[:, seq - wp:, k_off + kw:k_off + 2 * kw].reshape(1, bsz, wp, ATTN_KV_HEADS, HEAD_DIM)

    tp = SAMPLE_T_PAD
    xs_pad = jnp.pad(x_sample, ((0, 0), (0, tp - dseq), (0, 0)))
    proj_s3 = _in_proj(xs_pad.reshape(dbsz * tp, d), norm_mix_w[0], w_in_r, TOKEN_BLOCK).reshape(dbsz, tp, -1)
    k_new = proj_s3[:, :dseq, k_off:k_off + kw]
    v_new = proj_s3[:, :dseq, k_off + kw:k_off + 2 * kw]
    k_cache = cache_k_win[0].reshape(dbsz, wb, kw)
    v_cache = cache_v_win[0].reshape(dbsz, wb, kw)
    kk = jnp.concatenate([k_cache, k_new], axis=1)
    vv = jnp.concatenate([v_cache, v_new], axis=1)
    q_pos = jnp.arange(tp)[:, None]
    k_pos = jnp.arange(wb + tp)[None, :]
    dist_s = wb + q_pos - k_pos
    mask_s = (dist_s >= 0) & (dist_s <= WINDOW) & (q_pos < dseq) & (k_pos < wb + dseq)
    bias_s = _masked_bias(rel_bias_table, tp, wb + tp, wb, mask_s)
    attn_s = _swa_sample(proj_s3, k_cache, v_cache, sinks, bias_s[:, :, :wb], bias_s[:, :, wb:],
                         0, col_k, col_v, 16)
    oh_s, st_s = _hgrn(proj_s3, lb, hg_norm_w[0], state_hgrn[0], hg_cols, tp, dseq, 8)
    y_s = _ffn(x_sample.reshape(dbsz * dseq, d), attn_s[:, :dseq].reshape(dbsz * dseq, aw),
               oh_s[:, :dseq].reshape(dbsz * dseq, hw), w, TOKEN_BLOCK)
    k_win_s = kk[:, dseq:].reshape(1, dbsz, wb, ATTN_KV_HEADS, HEAD_DIM)
    v_win_s = vv[:, dseq:].reshape(1, dbsz, wb, ATTN_KV_HEADS, HEAD_DIM)

    return (y_p.reshape(bsz, seq, d), y_s.reshape(dbsz, dseq, d), k_win_p, v_win_p, st_p[None],
            k_win_s, v_win_s, st_s[None])
```

```python
import functools
import math

import jax
import jax.numpy as jnp
from jax import lax
from jax.experimental import pallas as pl
from jax.experimental.pallas import tpu as pltpu

F32 = jnp.float32
BF16 = jnp.bfloat16

EPS = 1e-6
NEG = -1e30

ATTN_HEADS = 8
ATTN_KV_HEADS = 2
HEAD_DIM = 64
WINDOW = 128
REL_BUCKETS = 32
HG_HEADS = 4
HG_CHUNK = 64
PEER_HEADS = 8
PEER_TOPK = 16

LANES = 128
SUBLANES = 8
VMEM_LIMIT = 56 * 1024 * 1024


def _cparams(*sem):
    return pltpu.CompilerParams(dimension_semantics=sem, vmem_limit_bytes=VMEM_LIMIT)


def _nt(a, b):
    return lax.dot_general(a, b, (((1,), (1,)), ((), ())), preferred_element_type=F32)


def _tn(a, b):
    return lax.dot_general(a, b, (((0,), (0,)), ((), ())), preferred_element_type=F32)


def _dot(a, b):
    return jnp.dot(a, b, preferred_element_type=F32)


def _sigmoid(x):
    return 1.0 / (1.0 + jnp.exp(-x))


def _in_proj_kernel(x_ref, nw_ref, w_ref, o_ref):
    x = x_ref[...]
    xn = x * lax.rsqrt(jnp.mean(x * x, axis=-1, keepdims=True) + EPS) * nw_ref[...]
    o_ref[...] = _dot(xn.astype(BF16), w_ref[...])


def _in_proj(x2d, norm_w, w_bf16, tm):
    t, d = x2d.shape
    n = w_bf16.shape[1]
    return pl.pallas_call(
        _in_proj_kernel,
        grid=(t // tm,),
        in_specs=[pl.BlockSpec((tm, d), lambda i: (i, 0)),
                  pl.BlockSpec((1, d), lambda i: (0, 0)),
                  pl.BlockSpec((d, n), lambda i: (0, 0))],
        out_specs=pl.BlockSpec((tm, n), lambda i: (i, 0)),
        out_shape=jax.ShapeDtypeStruct((t, n), F32),
        compiler_params=_cparams("parallel"),
        name="in_proj",
    )(x2d, norm_w.reshape(1, d), w_bf16)


def _t5_bucket(dist):
    n = jnp.maximum(dist, 0)
    max_exact = REL_BUCKETS // 2
    nf = jnp.maximum(n, 1).astype(F32)
    large = max_exact + (jnp.log(nf / max_exact) / math.log(WINDOW / max_exact)
                         * (REL_BUCKETS - max_exact)).astype(jnp.int32)
    large = jnp.minimum(large, REL_BUCKETS - 1)
    return jnp.where(n < max_exact, n, large)


def _masked_bias(table, n_q, n_k, offset, mask):
    h = table.shape[1]
    diag = jnp.arange(n_q + n_k - 1) - (n_k - 1) + offset
    per_diag = table.astype(F32)[_t5_bucket(diag)].T
    w = jnp.pad(per_diag[:, ::-1], ((0, 0), (0, 1)))
    p = n_q + n_k
    skew = jnp.tile(w, (1, n_q))[:, :n_q * (p - 1)].reshape(h, n_q, p - 1)
    return jnp.where(mask[None], skew[:, :, n_q - 1:n_q - 1 + n_k], NEG)


def _swa_prompt_kernel(sink_ref, q_ref, kp_ref, kc_ref, vp_ref, vc_ref, bias_ref, o_ref):
    scale = HEAD_DIM ** -0.5
    assert math.frexp(scale)[0] == 0.5
    group = ATTN_HEADS // ATTN_KV_HEADS
    assert 2 * HEAD_DIM == LANES and kp_ref.shape[2] == LANES and ATTN_KV_HEADS == 2 and group % 2 == 0
    kk = jnp.concatenate([kp_ref[0], kc_ref[0]], axis=0)
    vv = jnp.concatenate([vp_ref[0], vc_ref[0]], axis=0)
    low = lax.broadcasted_iota(jnp.int32, kk.shape, 1) < HEAD_DIM
    def halves(x, kvh):
        own = jnp.where(low if kvh == 0 else ~low, x, 0.0)
        other = pltpu.roll(own, HEAD_DIM, axis=1)
        lo, hi = (own, other) if kvh == 0 else (other, own)
        return lo.astype(BF16), hi.astype(BF16)

    k_half = [halves(kk, kvh) for kvh in range(ATTN_KV_HEADS)]
    v_half = [halves(vv, kvh) for kvh in range(ATTN_KV_HEADS)]
    heads = range(ATTN_HEADS)
    kv_of = lambda h: h // group
    qt = [(q_ref[0, :, t * LANES:(t + 1) * LANES] * scale).astype(BF16) for t in range(ATTN_HEADS // 2)]
    s = [_nt(qt[h // 2], k_half[kv_of(h)][h % 2]) + bias_ref[0, h] for h in heads]
    m = [jnp.maximum(jnp.max(s[h], axis=-1, keepdims=True), sink_ref[h]) for h in heads]
    p = [jnp.exp(s[h] - m[h]) for h in heads]
    den = [jnp.sum(p[h], axis=-1, keepdims=True) + jnp.exp(sink_ref[h] - m[h]) for h in heads]
    p = [(p[h] * (1.0 / den[h])).astype(BF16) for h in heads]
    for tile in range(ATTN_HEADS // 2):
        kvh = kv_of(2 * tile)
        o = _dot(p[2 * tile], v_half[kvh][0]) + _dot(p[2 * tile + 1], v_half[kvh][1])
        o_ref[0, :, tile * LANES:(tile + 1) * LANES] = o.astype(o_ref.dtype)


def _swa_prompt(proj3, sinks, bias, col_q, col_k, col_v):
    bsz, seq, _ = proj3.shape
    blk = WINDOW
    aw = ATTN_HEADS * HEAD_DIM
    kw = ATTN_KV_HEADS * HEAD_DIM
    prev = lambda b, n: (b, jnp.maximum(n - 1, 0))
    return pl.pallas_call(
        _swa_prompt_kernel,
        grid=(bsz, seq // blk),
        in_specs=[pl.BlockSpec(memory_space=pltpu.SMEM),
                  pl.BlockSpec((1, blk, aw), lambda b, n: (b, n, col_q)),
                  pl.BlockSpec((1, blk, kw), lambda b, n: prev(b, n) + (col_k,)),
                  pl.BlockSpec((1, blk, kw), lambda b, n: (b, n, col_k)),
                  pl.BlockSpec((1, blk, kw), lambda b, n: prev(b, n) + (col_v,)),
                  pl.BlockSpec((1, blk, kw), lambda b, n: (b, n, col_v)),
                  pl.BlockSpec((1, ATTN_HEADS, blk, 2 * blk), lambda b, n: (jnp.minimum(n, 1), 0, 0, 0))],
        out_specs=pl.BlockSpec((1, blk, aw), lambda b, n: (b, n, 0)),
        out_shape=jax.ShapeDtypeStruct((bsz, seq, aw), BF16),
        compiler_params=_cparams("parallel", "arbitrary"),
        name="swa_prompt",
    )(sinks, proj3, proj3, proj3, proj3, proj3, bias)


def _swa_sample_kernel(sink_ref, q_ref, kn_ref, vn_ref, kc_ref, vc_ref, bias_c_ref, bias_n_ref, o_ref):
    scale = HEAD_DIM ** -0.5
    group = ATTN_HEADS // ATTN_KV_HEADS
    for h in range(ATTN_HEADS):
        kv = slice((h // group) * HEAD_DIM, (h // group + 1) * HEAD_DIM)
        qh = q_ref[:, :, h * HEAD_DIM:(h + 1) * HEAD_DIM].astype(BF16)
        sc = jnp.einsum('bqd,bkd->bqk', qh, kc_ref[:, :, kv].astype(BF16),
                        preferred_element_type=F32) * scale + bias_c_ref[h][None]
        sn = jnp.einsum('bqd,bkd->bqk', qh, kn_ref[:, :, kv].astype(BF16),
                        preferred_element_type=F32) * scale + bias_n_ref[h][None]
        sink = sink_ref[h]
        m = jnp.maximum(jnp.maximum(jnp.max(sc, axis=-1, keepdims=True),
                                    jnp.max(sn, axis=-1, keepdims=True)), sink)
        pc = jnp.exp(sc - m)
        pn = jnp.exp(sn - m)
        den = (jnp.sum(pc, axis=-1, keepdims=True) + jnp.sum(pn, axis=-1, keepdims=True)
               + jnp.exp(sink - m))
        o = (jnp.einsum('bqk,bkd->bqd', pc.astype(BF16), vc_ref[:, :, kv].astype(BF16), preferred_element_type=F32)
             + jnp.einsum('bqk,bkd->bqd', pn.astype(BF16), vn_ref[:, :, kv].astype(BF16),
                          preferred_element_type=F32)) / den
        o_ref[:, :, h * HEAD_DIM:(h + 1) * HEAD_DIM] = o.astype(o_ref.dtype)


def _swa_sample(proj3, k_cache, v_cache, sinks, bias_c, bias_n, col_q, col_k, col_v, bb):
    bsz, tp, _ = proj3.shape
    wb = k_cache.shape[1]
    aw = ATTN_HEADS * HEAD_DIM
    kw = ATTN_KV_HEADS * HEAD_DIM
    return pl.pallas_call(
        _swa_sample_kernel,
        grid=(bsz // bb,),
        in_specs=[pl.BlockSpec(memory_space=pltpu.SMEM),
                  pl.BlockSpec((bb, tp, aw), lambda b: (b, 0, col_q)),
                  pl.BlockSpec((bb, tp, kw), lambda b: (b, 0, col_k)),
                  pl.BlockSpec((bb, tp, kw), lambda b: (b, 0, col_v)),
                  pl.BlockSpec((bb, wb, kw), lambda b: (b, 0, 0)),
                  pl.BlockSpec((bb, wb, kw), lambda b: (b, 0, 0)),
                  pl.BlockSpec((ATTN_HEADS, tp, wb), lambda b: (0, 0, 0)),
                  pl.BlockSpec((ATTN_HEADS, tp, tp), lambda b: (0, 0, 0))],
        out_specs=pl.BlockSpec((bb, tp, aw), lambda b: (b, 0, 0)),
        out_shape=jax.ShapeDtypeStruct((bsz, tp, aw), BF16),
        compiler_params=_cparams("parallel"),
        name="swa_sample",
    )(sinks, proj3, proj3, proj3, k_cache, v_cache, bias_c, bias_n)


def _split3(x):
    hi = x.astype(BF16)
    r = x - hi.astype(F32)
    mid = r.astype(BF16)
    lo = (r - mid.astype(F32)).astype(BF16)
    return hi, mid, lo


def _hgrn_kernel(*refs, t_valid, has_state):
    if has_state:
        q_ref, f_ref, i_ref, g_ref, lb_ref, nw_ref, s0_ref, o_ref, s_ref, st_scr = refs
    else:
        q_ref, f_ref, i_ref, g_ref, lb_ref, nw_ref, o_ref, s_ref, st_scr = refs
    bb, chunk, width = q_ref.shape
    dk = width // HG_HEADS
    c = pl.program_id(1)

    @pl.when(c == 0)
    def _init():
        if has_state:
            st_scr[...] = s0_ref[...]
        else:
            st_scr[...] = jnp.zeros_like(st_scr)

    row = lax.broadcasted_iota(jnp.int32, (chunk, chunk), 0)
    col = lax.broadcasted_iota(jnp.int32, (chunk, chunk), 1)
    causal = row >= col
    tri = jnp.where(causal, 1.0, 0.0).astype(BF16)
    row_w = lax.broadcasted_iota(jnp.int32, (chunk, width), 0)
    valid = row_w < t_valid
    ones_rows = jnp.where(lax.broadcasted_iota(jnp.int32, (chunk, dk), 0) < 3, 1.0, 0.0).astype(BF16)
    mid_row = chunk // 2

    def body(b, carry):
        qx = q_ref[b]
        q = qx * _sigmoid(qx)
        lb = lb_ref[...]
        f = lb + (1.0 - lb) * _sigmoid(f_ref[b])
        k = 1.0 - f
        lg = jnp.log(f)
        if t_valid < chunk:
            k = jnp.where(valid, k, 0.0)
            lg = jnp.where(valid, lg, 0.0)
        v = i_ref[b].astype(BF16)
        cum = sum(_dot(tri, part) for part in _split3(lg))
        cum_mid = cum[mid_row:mid_row + 1, :]
        cum_last = cum[chunk - 1:chunk, :]
        qt = (q * jnp.exp(cum - cum_mid)).astype(BF16)
        kt = (k * jnp.exp(cum_mid - cum)).astype(BF16)
        qe = (q * jnp.exp(cum)).astype(BF16)
        kd = (k * jnp.exp(cum_last - cum)).astype(BF16)
        dec_rows = jnp.zeros((chunk, width), F32)
        for j, part in enumerate(_split3(jnp.exp(cum_last))):
            dec_rows = jnp.where(row_w == j, part.astype(F32), dec_rows)
        dec_rows = dec_rows.astype(BF16)
        heads = range(HG_HEADS)
        sl = [slice(h * dk, (h + 1) * dk) for h in heads]
        st = [st_scr[b, h] for h in heads]
        a = [_nt(qt[:, sl[h]], kt[:, sl[h]]) for h in heads]
        inter = [_dot(qe[:, sl[h]], st[h].astype(BF16)) for h in heads]
        upd = [_tn(kd[:, sl[h]], v[:, sl[h]]) for h in heads]
        decay = [_tn(dec_rows[:, sl[h]], ones_rows) for h in heads]
        for h in heads:
            st_scr[b, h] = st[h] * decay[h] + upd[h]
        a = [jnp.where(causal, a[h], 0.0).astype(BF16) for h in heads]
        o = [_dot(a[h], v[:, sl[h]]) + inter[h] for h in heads]
        outs = [o[h] * lax.rsqrt(jnp.mean(o[h] * o[h], axis=-1, keepdims=True) + EPS) for h in heads]
        gx = g_ref[b]
        o = jnp.concatenate(outs, axis=1) * nw_ref[...] * (gx * _sigmoid(gx))
        o_ref[b] = o.astype(o_ref.dtype)
        return carry

    lax.fori_loop(0, bb, body, 0, unroll=4)

    @pl.when(c == pl.num_programs(1) - 1)
    def _final():
        s_ref[...] = st_scr[...]


def _hgrn(proj3, lb, norm_w, s0, cols, chunk, t_valid, bb):
    bsz, t, _ = proj3.shape
    width = lb.shape[0]
    dk = width // HG_HEADS
    has_state = s0 is not None
    spec = lambda cb: pl.BlockSpec((bb, chunk, width), lambda b, c: (b, c, cb))
    vec = pl.BlockSpec((1, width), lambda b, c: (0, 0))
    st_spec = pl.BlockSpec((bb, HG_HEADS, dk, dk), lambda b, c: (b, 0, 0, 0))
    in_specs = [spec(cols[0]), spec(cols[1]), spec(cols[2]), spec(cols[3]), vec, vec]
    args = [proj3, proj3, proj3, proj3, lb.reshape(1, width), norm_w.reshape(1, width)]
    if has_state:
        in_specs.append(st_spec)
        args.append(s0)
    return pl.pallas_call(
        functools.partial(_hgrn_kernel, t_valid=t_valid, has_state=has_state),
        grid=(bsz // bb, t // chunk),
        in_specs=in_specs,
        out_specs=[pl.BlockSpec((bb, chunk, width), lambda b, c: (b, c, 0)), st_spec],
        out_shape=[jax.ShapeDtypeStruct((bsz, t, width), BF16),
                   jax.ShapeDtypeStruct((bsz, HG_HEADS, dk, dk), F32)],
        scratch_shapes=[pltpu.VMEM((bb, HG_HEADS, dk, dk), F32)],
        compiler_params=_cparams("parallel", "arbitrary"),
        name="hgrn_state" if has_state else "hgrn_prompt",
    )(*args)


def _out_proj_kernel(x_ref, a_ref, oh_ref, wo_ref, nw_ref, h_ref, hnt_ref):
    aw = a_ref.shape[1]
    mix = _dot(a_ref[...], wo_ref[:aw, :]) + _dot(oh_ref[...], wo_ref[aw:, :])
    h = x_ref[...] + mix
    h_ref[...] = h
    hn = h * lax.rsqrt(jnp.mean(h * h, axis=-1, keepdims=True) + EPS) * nw_ref[...]
    hnt_ref[...] = hn.T.astype(hnt_ref.dtype)


def _out_proj(x2d, attn, oh, wo_bf16, norm_w, tm):
    t, d = x2d.shape
    aw, hw = attn.shape[1], oh.shape[1]
    return pl.pallas_call(
        _out_proj_kernel,
        grid=(t // tm,),
        in_specs=[pl.BlockSpec((tm, d), lambda i: (i, 0)),
                  pl.BlockSpec((tm, aw), lambda i: (i, 0)),
                  pl.BlockSpec((tm, hw), lambda i: (i, 0)),
                  pl.BlockSpec((aw + hw, d), lambda i: (0, 0)),
                  pl.BlockSpec((1, d), lambda i: (0, 0))],
        out_specs=[pl.BlockSpec((tm, d), lambda i: (i, 0)),
                   pl.BlockSpec((d, tm), lambda i: (0, i))],
        out_shape=[jax.ShapeDtypeStruct((t, d), F32),
                   jax.ShapeDtypeStruct((d, t), BF16)],
        compiler_params=_cparams("parallel"),
        name="out_proj",
    )(x2d, attn, oh, wo_bf16, norm_w.reshape(1, d))


def _peer_scores_kernel(hnt_ref, wqt_ref, keys_ref, sc_ref):
    qt = _dot(wqt_ref[...], hnt_ref[...])
    half = keys_ref.shape[2]
    for hc in range(keys_ref.shape[0]):
        s = _dot(keys_ref[hc], qt[hc * half:(hc + 1) * half, :].astype(BF16))
        for lt in range(sc_ref.shape[0]):
            sc_ref[lt, hc] = s[:, lt * LANES:(lt + 1) * LANES]


def _peer_scores(hnt, wqt_bf16, keys_bf16, tb):
    d, t = hnt.shape
    nhc, nk, half = keys_bf16.shape
    return pl.pallas_call(
        _peer_scores_kernel,
        grid=(t // tb,),
        in_specs=[pl.BlockSpec((d, tb), lambda i: (0, i)),
                  pl.BlockSpec((nhc * half, d), lambda i: (0, 0)),
                  pl.BlockSpec((nhc, nk, half), lambda i: (0, 0, 0))],
        out_specs=pl.BlockSpec((tb // LANES, nhc, nk, LANES), lambda i: (i, 0, 0, 0)),
        out_shape=jax.ShapeDtypeStruct((t // LANES, nhc, nk, LANES), F32),
        compiler_params=_cparams("parallel"),
        name="peer_scores",
    )(hnt, wqt_bf16, keys_bf16)


def _sort16_pairs():
    def merge(lo, hi, r):
        step = r * 2
        if step < hi - lo:
            yield from merge(lo, hi, step)
            yield from merge(lo + r, hi, step)
            yield from [(i, i + r) for i in range(lo + r, hi - r, step)]
        else:
            yield (lo, lo + r)

    def sort(lo, hi):
        if hi - lo >= 1:
            mid = lo + (hi - lo) // 2
            yield from sort(lo, mid)
            yield from sort(mid + 1, hi)
            yield from merge(lo, hi, 1)

    return tuple(sort(0, PEER_TOPK - 1))


_SORT16 = _sort16_pairs()


def _bitonic_to_sorted(z):
    z = list(z)
    d = PEER_TOPK // 2
    while d >= 1:
        for i in range(PEER_TOPK):
            if i & d == 0:
                hi, lo = jnp.maximum(z[i], z[i + d]), jnp.minimum(z[i], z[i + d])
                z[i], z[i + d] = hi, lo
        d //= 2
    return z


def _merge_bitonic(top, other):
    z = list(top)
    m = len(other)
    for r in range(PEER_TOPK - m, PEER_TOPK):
        z[r] = jnp.maximum(top[r], other[PEER_TOPK - 1 - r])
    return z


def _top16_desc(x):
    n = x.shape[0] // SUBLANES
    xs = [x[g * SUBLANES:(g + 1) * SUBLANES, :] for g in range(n)]
    for i, j in _SORT16:
        xs[i], xs[j] = jnp.maximum(xs[i], xs[j]), jnp.minimum(xs[i], xs[j])
    shift = SUBLANES // 2
    while shift >= 1:
        ys = [pltpu.roll(v, shift, axis=0) for v in xs]
        xs = _bitonic_to_sorted(_merge_bitonic(xs, ys))
        shift //= 2
    return xs


def _peer_select(sc_ref, thr_scr, pw_scr, q_scr, lt):
    assert PEER_HEADS == SUBLANES
    sub = lax.broadcasted_iota(jnp.int32, (SUBLANES, LANES), 0)
    a = b = None
    for h in range(PEER_HEADS):
        a_h = _top16_desc(sc_ref[lt, 2 * h])
        b_h = _top16_desc(sc_ref[lt, 2 * h + 1])
        a = a_h if h == 0 else [jnp.where(sub == h, new, old) for new, old in zip(a_h, a)]
        b = b_h if h == 0 else [jnp.where(sub == h, new, old) for new, old in zip(b_h, b)]
    lists = [[a[r] + b[c] for c in range(PEER_TOPK // (r + 1))] for r in range(SUBLANES)]
    lists.append([a[r] + b[0] for r in range(SUBLANES, PEER_TOPK)])
    top = lists[0]
    for other in lists[1:-1]:
        top = _bitonic_to_sorted(_merge_bitonic(top, other))
    z = _merge_bitonic(top, lists[-1])
    tau = functools.reduce(jnp.minimum, z)
    best = a[0] + b[0]
    zsum = jnp.zeros_like(tau)
    inf = jnp.full_like(tau, jnp.inf)
    thr_rank = []
    for r, cand in enumerate(lists):
        hits = [v >= tau for v in cand]
        for v, hit in zip(cand, hits):
            zsum = zsum + jnp.where(hit, jnp.exp(v - best), 0.0)
        if r < SUBLANES:
            t = inf
            for c, hit in enumerate(hits):
                t = jnp.where(hit, b[c], t)
            thr_rank.append(t)
        else:
            thr_rank.extend(jnp.where(hit, b[0], inf) for hit in hits)
    inv = 0.5 / zsum
    for h in range(PEER_HEADS):
        own = lambda v: jnp.broadcast_to(v[h:h + 1, :], (SUBLANES, LANES))
        a_h = [own(v) for v in a]
        thr_h = [own(v) for v in thr_rank]
        inv_h, b0_h = own(inv), own(b[0])
        s0 = sc_ref[lt, 2 * h]
        s1 = sc_ref[lt, 2 * h + 1]
        for g in range(s0.shape[0] // SUBLANES):
            rows = slice(g * SUBLANES, (g + 1) * SUBLANES)
            x0 = s0[rows, :]
            thr = jnp.full_like(x0, jnp.inf)
            for r in range(PEER_TOPK):
                thr = jnp.where(x0 == a_h[r], thr_h[r], thr)
            thr_scr[lt, h, rows, :] = thr
            pw_scr[lt, h, rows, :] = jnp.exp(x0 - a_h[0]) * inv_h
            q_scr[lt, h, rows, :] = jnp.exp(s1[rows, :] - b0_h)


MXU_TILE = 256
MXU_COUNT = 2
ACC_ROWS = 512
ACC_PIECE = 32
ACC_PRE = 0
ACC_OUT = ACC_ROWS // 4


def _peer_dense_kernel(sc_ref, hnt_ref, u_ref, vt_ref, res_ref, nw_ref, y_ref, thr_scr, pw_scr, q_scr,
                       h0_scr, h1_scr, g0_scr, g1_scr, yt_ref):
    s = pl.program_id(1)
    n_e = pl.num_programs(1) - 2
    d, tb = hnt_ref.shape
    n_lt, _, nk, _ = sc_ref.shape
    eb = u_ref.shape[0]
    n_i = eb // nk
    assert tb == MXU_COUNT * MXU_TILE and eb % ACC_ROWS == 0 and d % ACC_ROWS == 0

    @pl.when(s == 0)
    def _select():
        def body(lt, carry):
            _peer_select(sc_ref, thr_scr, pw_scr, q_scr, lt)
            return carry
        lax.fori_loop(0, n_lt, body, 0)
        yt_ref[...] = jnp.zeros_like(yt_ref)
        g0_scr[...] = jnp.zeros_like(g0_scr)
        g1_scr[...] = jnp.zeros_like(g1_scr)

    i0 = pl.multiple_of(jnp.clip(s - 1, 0, n_e - 1) * n_i, SUBLANES)

    def gate_steps(lt, ii, h_r, g_w):
        lanes = slice(lt * LANES, (lt + 1) * LANES)
        rows = slice(ii * nk, (ii + 1) * nk)
        state = {"w": jnp.zeros((nk, LANES), F32)}

        def head(h):
            thr = thr_scr[lt, h, pl.ds(i0, n_i), :][ii:ii + 1, :]
            pw = pw_scr[lt, h, pl.ds(i0, n_i), :][ii:ii + 1, :]
            state["w"] = state["w"] + jnp.where(sc_ref[lt, 2 * h + 1] >= thr, q_scr[lt, h], 0.0) * pw
            if h == PEER_HEADS - 1:
                x = h_r[rows, lanes]
                act = x + x * lax.erf(x * (2.0 ** -0.5))
                g_w[rows, lanes] = (state["w"] * act).astype(g_w.dtype)

        return [functools.partial(head, h) for h in range(PEER_HEADS)]

    def mxu_steps(kind, c, k, reg, h_w, g_r):
        lhs_ref, rhs_ref, acc, n_k = ((u_ref, hnt_ref, ACC_PRE, d // MXU_TILE) if kind == "pre"
                                      else (vt_ref, g_r, ACC_OUT, eb // MXU_TILE))
        kc = slice(k * MXU_TILE, (k + 1) * MXU_TILE)

        def push():
            for q in range(MXU_COUNT):
                pltpu.matmul_push_rhs(rhs_ref[kc, q * MXU_TILE:(q + 1) * MXU_TILE], staging_register=reg,
                                      mxu_index=q)

        def piece(p):
            r0 = c * ACC_ROWS + p * ACC_PIECE
            lhs = lhs_ref[r0:r0 + ACC_PIECE, kc]
            for q in range(MXU_COUNT):
                pltpu.matmul_acc_lhs(acc + p * ACC_PIECE // 4, lhs, q, load_staged_rhs=reg if p == 0 else None)

        def pop(p):
            rows = slice(c * ACC_ROWS + p * ACC_PIECE, c * ACC_ROWS + (p + 1) * ACC_PIECE)
            for q in range(MXU_COUNT):
                cols = slice(q * MXU_TILE, (q + 1) * MXU_TILE)
                res = pltpu.matmul_pop(acc + p * ACC_PIECE // 4, (ACC_PIECE, MXU_TILE), F32, q)
                if kind == "pre":
                    h_w[rows, cols] = res
                else:
                    yt_ref[rows, cols] += res

        n_p = ACC_ROWS // ACC_PIECE
        pops = [functools.partial(pop, p) for p in range(n_p)] if k == n_k - 1 else []
        return push, [functools.partial(piece, p) for p in range(n_p)], pops

    def stage(h_w, h_r, g_w, g_r, pre, gating, out):
        pre_groups = [("pre", c, k) for c in range(eb // ACC_ROWS) for k in range(d // MXU_TILE)] if pre else []
        out_groups = [("out", c, k) for c in range(d // ACC_ROWS) for k in range(eb // MXU_TILE)] if out else []
        if pre and out:
            order = [g for pair in zip(pre_groups, out_groups) for g in pair]
        else:
            order = pre_groups + out_groups
        steps = [mxu_steps(kind, c, k, gi % 2, h_w, g_r) for gi, (kind, c, k) in enumerate(order)]
        mxu = [steps[0][0]]
        lagged = []
        for gi, (_, pieces, pops) in enumerate(steps):
            half = len(pieces) // 2
            for p, piece in enumerate(pieces):
                if p == half and gi + 1 < len(steps):
                    mxu.append(steps[gi + 1][0])
                mxu.append(piece)
                if lagged:
                    mxu.append(lagged.pop(0))
            same_acc_next = gi + 1 < len(steps) and order[gi + 1][0] == order[gi][0]
            if same_acc_next or gi + 1 == len(steps):
                mxu.extend(pops)
            else:
                lagged = list(pops)
        vpu = [t for lt in range(n_lt) for ii in range(n_i) for t in gate_steps(lt, ii, h_r, g_w)] if gating else []
        im = iv = 0
        while im < len(mxu) or iv < len(vpu):
            if iv >= len(vpu) or (im < len(mxu) and im * len(vpu) <= iv * len(mxu)):
                mxu[im]()
                im += 1
            else:
                vpu[iv]()
                iv += 1

    last = n_e + 1

    @pl.when(s == 0)
    def _first():
        stage(h0_scr, None, None, None, True, False, False)

    @pl.when((s > 0) & (s < last) & (s % 2 == 0))
    def _even():
        stage(h0_scr, h1_scr, g1_scr, g0_scr, True, True, True)

    @pl.when((s > 0) & (s < last) & (s % 2 == 1))
    def _odd():
        stage(h1_scr, h0_scr, g0_scr, g1_scr, True, True, True)

    def finish(g_r):
        stage(None, None, None, g_r, False, False, True)
        y = res_ref[...] + yt_ref[...].T
        y_ref[...] = y * lax.rsqrt(jnp.mean(y * y, axis=-1, keepdims=True) + EPS) * nw_ref[...]

    @pl.when((s == last) & (s % 2 == 0))
    def _last_even():
        finish(g0_scr)

    @pl.when((s == last) & (s % 2 == 1))
    def _last_odd():
        finish(g1_scr)


def _peer_dense(sc, hnt, u_bf16, vt_bf16, resid, norm_w, tb, eb):
    _, nhc, nk, _ = sc.shape
    d, t = hnt.shape
    n_exp = u_bf16.shape[0]
    n_lt = tb // LANES
    assert eb == SUBLANES * nk and n_exp == nk * nk and t % tb == 0 and tb % LANES == 0
    sel = pltpu.VMEM((n_lt, PEER_HEADS, nk, LANES), F32)
    pre = pltpu.VMEM((eb, tb + LANES), F32)
    gated = pltpu.VMEM((eb, tb), BF16)
    n_e = n_exp // eb
    return pl.pallas_call(
        _peer_dense_kernel,
        grid=(t // tb, n_e + 2),
        in_specs=[pl.BlockSpec((n_lt, nhc, nk, LANES), lambda i, s: (i, 0, 0, 0)),
                  pl.BlockSpec((d, tb), lambda i, s: (0, i)),
                  pl.BlockSpec((eb, d), lambda i, s: (jnp.minimum(s, n_e - 1), 0)),
                  pl.BlockSpec((d, eb), lambda i, s: (0, jnp.clip(s - 2, 0, n_e - 1))),
                  pl.BlockSpec((tb, d), lambda i, s: (i, 0)),
                  pl.BlockSpec((1, d), lambda i, s: (0, 0))],
        out_specs=pl.BlockSpec((tb, d), lambda i, s: (i, 0)),
        out_shape=jax.ShapeDtypeStruct((t, d), F32),
        scratch_shapes=[sel, sel, sel, pre, pre, gated, gated, pltpu.VMEM((d, tb), F32)],
        compiler_params=_cparams("parallel", "arbitrary"),
        name="peer_dense",
    )(sc, hnt, u_bf16, vt_bf16, resid, norm_w.reshape(1, d))


TOKEN_BLOCK = 512
EXPERT_BLOCK = 1024
SAMPLE_T_PAD = 16


def _ffn(x2d, attn, oh, w, tb):
    h, hnt = _out_proj(x2d, attn, oh, w['wo'], w['norm_ffn'], tb)
    sc = _peer_scores(hnt, w['wqt'], w['keys'], tb)
    return _peer_dense(sc, hnt, w['u'], w['vt'], h, w['norm_final'], tb, EXPERT_BLOCK)


def kernel(x_prompt, x_sample, cache_k_win, cache_v_win, state_hgrn, norm_mix_w, w_in, attn_sinks,
           rel_bias_table, hg_lb, hg_norm_w, w_o, norm_ffn_w, peer_w_q, peer_sub_keys, peer_u, peer_v,
           norm_final_w):
    bsz, seq, d = x_prompt.shape
    dbsz, dseq, _ = x_sample.shape
    aw = ATTN_HEADS * HEAD_DIM
    kw = ATTN_KV_HEADS * HEAD_DIM
    hw = hg_norm_w.shape[1]
    wb = cache_k_win.shape[2]

    wi = w_in[0]
    w_in_r = jnp.concatenate([wi[:, :aw], wi[:, aw + 2 * kw:], wi[:, aw:aw + 2 * kw]], axis=1).astype(BF16)
    col_k = (aw + 4 * hw) // kw
    col_v = col_k + 1
    hg_cols = (1, 2, 3, 4)
    lb = jax.nn.softmax(hg_lb.astype(F32), axis=0)[0]
    nhc = PEER_HEADS * 2
    w = {
        'wo': w_o[0].astype(BF16),
        'norm_ffn': norm_ffn_w[0],
        'wqt': peer_w_q[0].astype(BF16).T,
        'keys': peer_sub_keys[0].reshape(nhc, peer_sub_keys.shape[3], peer_sub_keys.shape[4]).astype(BF16),
        'u': peer_u[0].astype(BF16),
        'vt': peer_v[0].astype(BF16).T,
        'norm_final': norm_final_w,
    }
    sinks = attn_sinks[0].astype(F32)

    proj_p = _in_proj(x_prompt.reshape(bsz * seq, d), norm_mix_w[0], w_in_r, TOKEN_BLOCK)
    proj_p3 = proj_p.reshape(bsz, seq, -1)
    blk = WINDOW
    dist_p = (jnp.arange(blk)[:, None] + blk) - jnp.arange(2 * blk)[None, :]
    in_win = (dist_p >= 0) & (dist_p <= WINDOW)
    has_prev = jnp.arange(2 * blk)[None, :] >= blk
    bias_p = jnp.stack([_masked_bias(rel_bias_table, blk, 2 * blk, blk, in_win & has_prev),
                        _masked_bias(rel_bias_table, blk, 2 * blk, blk, in_win)])
    attn_p = _swa_prompt(proj_p3, sinks, bias_p, 0, col_k, col_v)
    oh_p, st_p = _hgrn(proj_p3, lb, hg_norm_w[0], None, hg_cols, HG_CHUNK, HG_CHUNK, bsz)
    y_p = _ffn(x_prompt.reshape(bsz * seq, d), attn_p.reshape(bsz * seq, aw),
               oh_p.reshape(bsz * seq, hw), w, TOKEN_BLOCK)
    k_off = aw + 4 * hw
    wp = min(WINDOW, seq)
    k_win_p = proj_p3[:, seq - wp:, k_off:k_off + kw].reshape(1, bsz, wp, ATTN_KV_HEADS, HEAD_DIM)
    v_win_p = proj_p3[:, seq - wp:, k_off + kw:k_off + 2 * kw].reshape(1, bsz, wp, ATTN_KV_HEADS, HEAD_DIM)

    tp = SAMPLE_T_PAD
    xs_pad = jnp.pad(x_sample, ((0, 0), (0, tp - dseq), (0, 0)))
    proj_s3 = _in_proj(xs_pad.reshape(dbsz * tp, d), norm_mix_w[0], w_in_r, TOKEN_BLOCK).reshape(dbsz, tp, -1)
    k_new = proj_s3[:, :dseq, k_off:k_off + kw]
    v_new = proj_s3[:, :dseq, k_off + kw:k_off + 2 * kw]
    k_cache = cache_k_win[0].reshape(dbsz, wb, kw)
    v_cache = cache_v_win[0].reshape(dbsz, wb, kw)
    kk = jnp.concatenate([k_cache, k_new], axis=1)
    vv = jnp.concatenate([v_cache, v_new], axis=1)
    q_pos = jnp.arange(tp)[:, None]
    k_pos = jnp.arange(wb + tp)[None, :]
    dist_s = wb + q_pos - k_pos
    mask_s = (dist_s >= 0) & (dist_s <= WINDOW) & (q_pos < dseq) & (k_pos < wb + dseq)
    bias_s = _masked_bias(rel_bias_table, tp, wb + tp, wb, mask_s)
    attn_s = _swa_sample(proj_s3, k_cache, v_cache, sinks, bias_s[:, :, :wb], bias_s[:, :, wb:],
                         0, col_k, col_v, 16)
    oh_s, st_s = _hgrn(proj_s3, lb, hg_norm_w[0], state_hgrn[0], hg_cols, tp, dseq, 8)
    y_s = _ffn(x_sample.reshape(dbsz * dseq, d), attn_s[:, :dseq].reshape(dbsz * dseq, aw),
               oh_s[:, :dseq].reshape(dbsz * dseq, hw), w, TOKEN_BLOCK)
    k_win_s = kk[:, dseq:].reshape(1, dbsz, wb, ATTN_KV_HEADS, HEAD_DIM)
    v_win_s = vv[:, dseq:].reshape(1, dbsz, wb, ATTN_KV_HEADS, HEAD_DIM)

    return (y_p.reshape(bsz, seq, d), y_s.reshape(dbsz, dseq, d), k_win_p, v_win_p, st_p[None],
            k_win_s, v_win_s, st_s[None])
```

```python
import functools
import math

import jax
import jax.numpy as jnp
from jax import lax
from jax.experimental import pallas as pl
from jax.experimental.pallas import tpu as pltpu

F32 = jnp.float32
BF16 = jnp.bfloat16

EPS = 1e-6
NEG = -1e30

ATTN_HEADS = 8
ATTN_KV_HEADS = 2
HEAD_DIM = 64
WINDOW = 128
REL_BUCKETS = 32
HG_HEADS = 4
HG_CHUNK = 64
PEER_HEADS = 8
PEER_TOPK = 16

LANES = 128
SUBLANES = 8
VMEM_LIMIT = 56 * 1024 * 1024


def _cparams(*sem):
    return pltpu.CompilerParams(dimension_semantics=sem, vmem_limit_bytes=VMEM_LIMIT)


def _nt(a, b):
    return lax.dot_general(a, b, (((1,), (1,)), ((), ())), preferred_element_type=F32)


def _tn(a, b):
    return lax.dot_general(a, b, (((0,), (0,)), ((), ())), preferred_element_type=F32)


def _dot(a, b):
    return jnp.dot(a, b, preferred_element_type=F32)


def _sigmoid(x):
    return 1.0 / (1.0 + jnp.exp(-x))


def _in_proj_kernel(x_ref, nw_ref, w_ref, o_ref):
    x = x_ref[...]
    xn = x * lax.rsqrt(jnp.mean(x * x, axis=-1, keepdims=True) + EPS) * nw_ref[...]
    o_ref[...] = _dot(xn.astype(BF16), w_ref[...])


def _in_proj(x2d, norm_w, w_bf16, tm):
    t, d = x2d.shape
    n = w_bf16.shape[1]
    return pl.pallas_call(
        _in_proj_kernel,
        grid=(t // tm,),
        in_specs=[pl.BlockSpec((tm, d), lambda i: (i, 0)),
                  pl.BlockSpec((1, d), lambda i: (0, 0)),
                  pl.BlockSpec((d, n), lambda i: (0, 0))],
        out_specs=pl.BlockSpec((tm, n), lambda i: (i, 0)),
        out_shape=jax.ShapeDtypeStruct((t, n), F32),
        compiler_params=_cparams("parallel"),
        name="in_proj",
    )(x2d, norm_w.reshape(1, d), w_bf16)


def _t5_bucket(dist):
    n = jnp.maximum(dist, 0)
    max_exact = REL_BUCKETS // 2
    nf = jnp.maximum(n, 1).astype(F32)
    large = max_exact + (jnp.log(nf / max_exact) / math.log(WINDOW / max_exact)
                         * (REL_BUCKETS - max_exact)).astype(jnp.int32)
    large = jnp.minimum(large, REL_BUCKETS - 1)
    return jnp.where(n < max_exact, n, large)


def _masked_bias(table, n_q, n_k, offset, mask):
    h = table.shape[1]
    diag = jnp.arange(n_q + n_k - 1) - (n_k - 1) + offset
    per_diag = table.astype(F32)[_t5_bucket(diag)].T
    w = jnp.pad(per_diag[:, ::-1], ((0, 0), (0, 1)))
    p = n_q + n_k
    skew = jnp.tile(w, (1, n_q))[:, :n_q * (p - 1)].reshape(h, n_q, p - 1)
    return jnp.where(mask[None], skew[:, :, n_q - 1:n_q - 1 + n_k], NEG)


def _swa_prompt_kernel(sink_ref, q_ref, kp_ref, kc_ref, vp_ref, vc_ref, bias_ref, o_ref):
    scale = HEAD_DIM ** -0.5
    assert math.frexp(scale)[0] == 0.5
    group = ATTN_HEADS // ATTN_KV_HEADS
    assert 2 * HEAD_DIM == LANES and kp_ref.shape[2] == LANES and ATTN_KV_HEADS == 2 and group % 2 == 0
    kk = jnp.concatenate([kp_ref[0], kc_ref[0]], axis=0)
    vv = jnp.concatenate([vp_ref[0], vc_ref[0]], axis=0)
    low = lax.broadcasted_iota(jnp.int32, kk.shape, 1) < HEAD_DIM
    def halves(x, kvh):
        own = jnp.where(low if kvh == 0 else ~low, x, 0.0)
        other = pltpu.roll(own, HEAD_DIM, axis=1)
        lo, hi = (own, other) if kvh == 0 else (other, own)
        return lo.astype(BF16), hi.astype(BF16)

    k_half = [halves(kk, kvh) for kvh in range(ATTN_KV_HEADS)]
    v_half = [halves(vv, kvh) for kvh in range(ATTN_KV_HEADS)]
    heads = range(ATTN_HEADS)
    kv_of = lambda h: h // group
    qt = [(q_ref[0, :, t * LANES:(t + 1) * LANES] * scale).astype(BF16) for t in range(ATTN_HEADS // 2)]
    s = [_nt(qt[h // 2], k_half[kv_of(h)][h % 2]) + bias_ref[0, h] for h in heads]
    m = [jnp.maximum(jnp.max(s[h], axis=-1, keepdims=True), sink_ref[h]) for h in heads]
    p = [jnp.exp(s[h] - m[h]) for h in heads]
    den = [jnp.sum(p[h], axis=-1, keepdims=True) + jnp.exp(sink_ref[h] - m[h]) for h in heads]
    p = [(p[h] * (1.0 / den[h])).astype(BF16) for h in heads]
    for tile in range(ATTN_HEADS // 2):
        kvh = kv_of(2 * tile)
        o = _dot(p[2 * tile], v_half[kvh][0]) + _dot(p[2 * tile + 1], v_half[kvh][1])
        o_ref[0, :, tile * LANES:(tile + 1) * LANES] = o.astype(o_ref.dtype)


def _swa_prompt(proj3, sinks, bias, col_q, col_k, col_v):
    bsz, seq, _ = proj3.shape
    blk = WINDOW
    aw = ATTN_HEADS * HEAD_DIM
    kw = ATTN_KV_HEADS * HEAD_DIM
    prev = lambda b, n: (b, jnp.maximum(n - 1, 0))
    return pl.pallas_call(
        _swa_prompt_kernel,
        grid=(bsz, seq // blk),
        in_specs=[pl.BlockSpec(memory_space=pltpu.SMEM),
                  pl.BlockSpec((1, blk, aw), lambda b, n: (b, n, col_q)),
                  pl.BlockSpec((1, blk, kw), lambda b, n: prev(b, n) + (col_k,)),
                  pl.BlockSpec((1, blk, kw), lambda b, n: (b, n, col_k)),
                  pl.BlockSpec((1, blk, kw), lambda b, n: prev(b, n) + (col_v,)),
                  pl.BlockSpec((1, blk, kw), lambda b, n: (b, n, col_v)),
                  pl.BlockSpec((1, ATTN_HEADS, blk, 2 * blk), lambda b, n: (jnp.minimum(n, 1), 0, 0, 0))],
        out_specs=pl.BlockSpec((1, blk, aw), lambda b, n: (b, n, 0)),
        out_shape=jax.ShapeDtypeStruct((bsz, seq, aw), BF16),
        compiler_params=_cparams("parallel", "arbitrary"),
        name="swa_prompt",
    )(sinks, proj3, proj3, proj3, proj3, proj3, bias)


def _swa_sample_kernel(sink_ref, q_ref, kn_ref, vn_ref, kc_ref, vc_ref, bias_c_ref, bias_n_ref, o_ref):
    scale = HEAD_DIM ** -0.5
    group = ATTN_HEADS // ATTN_KV_HEADS
    for h in range(ATTN_HEADS):
        kv = slice((h // group) * HEAD_DIM, (h // group + 1) * HEAD_DIM)
        qh = q_ref[:, :, h * HEAD_DIM:(h + 1) * HEAD_DIM].astype(BF16)
        sc = jnp.einsum('bqd,bkd->bqk', qh, kc_ref[:, :, kv].astype(BF16),
                        preferred_element_type=F32) * scale + bias_c_ref[h][None]
        sn = jnp.einsum('bqd,bkd->bqk', qh, kn_ref[:, :, kv].astype(BF16),
                        preferred_element_type=F32) * scale + bias_n_ref[h][None]
        sink = sink_ref[h]
        m = jnp.maximum(jnp.maximum(jnp.max(sc, axis=-1, keepdims=True),
                                    jnp.max(sn, axis=-1, keepdims=True)), sink)
        pc = jnp.exp(sc - m)
        pn = jnp.exp(sn - m)
        den = (jnp.sum(pc, axis=-1, keepdims=True) + jnp.sum(pn, axis=-1, keepdims=True)
               + jnp.exp(sink - m))
        o = (jnp.einsum('bqk,bkd->bqd', pc.astype(BF16), vc_ref[:, :, kv].astype(BF16), preferred_element_type=F32)
             + jnp.einsum('bqk,bkd->bqd', pn.astype(BF16), vn_ref[:, :, kv].astype(BF16),
                          preferred_element_type=F32)) / den
        o_ref[:, :, h * HEAD_DIM:(h + 1) * HEAD_DIM] = o.astype(o_ref.dtype)


def _swa_sample(proj3, k_cache, v_cache, sinks, bias_c, bias_n, col_q, col_k, col_v, bb):
    bsz, tp, _ = proj3.shape
    wb = k_cache.shape[1]
    aw = ATTN_HEADS * HEAD_DIM
    kw = ATTN_KV_HEADS * HEAD_DIM
    return pl.pallas_call(
        _swa_sample_kernel,
        grid=(bsz // bb,),
        in_specs=[pl.BlockSpec(memory_space=pltpu.SMEM),
                  pl.BlockSpec((bb, tp, aw), lambda b: (b, 0, col_q)),
                  pl.BlockSpec((bb, tp, kw), lambda b: (b, 0, col_k)),
                  pl.BlockSpec((bb, tp, kw), lambda b: (b, 0, col_v)),
                  pl.BlockSpec((bb, wb, kw), lambda b: (b, 0, 0)),
                  pl.BlockSpec((bb, wb, kw), lambda b: (b, 0, 0)),
                  pl.BlockSpec((ATTN_HEADS, tp, wb), lambda b: (0, 0, 0)),
                  pl.BlockSpec((ATTN_HEADS, tp, tp), lambda b: (0, 0, 0))],
        out_specs=pl.BlockSpec((bb, tp, aw), lambda b: (b, 0, 0)),
        out_shape=jax.ShapeDtypeStruct((bsz, tp, aw), BF16),
        compiler_params=_cparams("parallel"),
        name="swa_sample",
    )(sinks, proj3, proj3, proj3, k_cache, v_cache, bias_c, bias_n)


def _split3(x):
    hi = x.astype(BF16)
    r = x - hi.astype(F32)
    mid = r.astype(BF16)
    lo = (r - mid.astype(F32)).astype(BF16)
    return hi, mid, lo


def _hgrn_kernel(*refs, t_valid, has_state):
    if has_state:
        q_ref, f_ref, i_ref, g_ref, lb_ref, nw_ref, s0_ref, o_ref, s_ref, st_scr = refs
    else:
        q_ref, f_ref, i_ref, g_ref, lb_ref, nw_ref, o_ref, s_ref, st_scr = refs
    bb, chunk, width = q_ref.shape
    dk = width // HG_HEADS
    c = pl.program_id(1)

    @pl.when(c == 0)
    def _init():
        if has_state:
            st_scr[...] = s0_ref[...]
        else:
            st_scr[...] = jnp.zeros_like(st_scr)

    row = lax.broadcasted_iota(jnp.int32, (chunk, chunk), 0)
    col = lax.broadcasted_iota(jnp.int32, (chunk, chunk), 1)
    causal = row >= col
    tri = jnp.where(causal, 1.0, 0.0).astype(BF16)
    row_w = lax.broadcasted_iota(jnp.int32, (chunk, width), 0)
    valid = row_w < t_valid
    ones_rows = jnp.where(lax.broadcasted_iota(jnp.int32, (chunk, dk), 0) < 3, 1.0, 0.0).astype(BF16)
    mid_row = chunk // 2

    def body(b, carry):
        qx = q_ref[b]
        q = qx * _sigmoid(qx)
        lb = lb_ref[...]
        f = lb + (1.0 - lb) * _sigmoid(f_ref[b])
        k = 1.0 - f
        lg = jnp.log(f)
        if t_valid < chunk:
            k = jnp.where(valid, k, 0.0)
            lg = jnp.where(valid, lg, 0.0)
        v = i_ref[b].astype(BF16)
        cum = sum(_dot(tri, part) for part in _split3(lg))
        cum_mid = cum[mid_row:mid_row + 1, :]
        cum_last = cum[chunk - 1:chunk, :]
        qt = (q * jnp.exp(cum - cum_mid)).astype(BF16)
        kt = (k * jnp.exp(cum_mid - cum)).astype(BF16)
        qe = (q * jnp.exp(cum)).astype(BF16)
        kd = (k * jnp.exp(cum_last - cum)).astype(BF16)
        dec_rows = jnp.zeros((chunk, width), F32)
        for j, part in enumerate(_split3(jnp.exp(cum_last))):
            dec_rows = jnp.where(row_w == j, part.astype(F32), dec_rows)
        dec_rows = dec_rows.astype(BF16)
        heads = range(HG_HEADS)
        sl = [slice(h * dk, (h + 1) * dk) for h in heads]
        st = [st_scr[b, h] for h in heads]
        a = [_nt(qt[:, sl[h]], kt[:, sl[h]]) for h in heads]
        inter = [_dot(qe[:, sl[h]], st[h].astype(BF16)) for h in heads]
        upd = [_tn(kd[:, sl[h]], v[:, sl[h]]) for h in heads]
        decay = [_tn(dec_rows[:, sl[h]], ones_rows) for h in heads]
        for h in heads:
            st_scr[b, h] = st[h] * decay[h] + upd[h]
        a = [jnp.where(causal, a[h], 0.0).astype(BF16) for h in heads]
        o = [_dot(a[h], v[:, sl[h]]) + inter[h] for h in heads]
        outs = [o[h] * lax.rsqrt(jnp.mean(o[h] * o[h], axis=-1, keepdims=True) + EPS) for h in heads]
        gx = g_ref[b]
        o = jnp.concatenate(outs, axis=1) * nw_ref[...] * (gx * _sigmoid(gx))
        o_ref[b] = o.astype(o_ref.dtype)
        return carry

    lax.fori_loop(0, bb, body, 0, unroll=4)

    @pl.when(c == pl.num_programs(1) - 1)
    def _final():
        s_ref[...] = st_scr[...]


def _hgrn(proj3, lb, norm_w, s0, cols, chunk, t_valid, bb):
    bsz, t, _ = proj3.shape
    width = lb.shape[0]
    dk = width // HG_HEADS
    has_state = s0 is not None
    spec = lambda cb: pl.BlockSpec((bb, chunk, width), lambda b, c: (b, c, cb))
    vec = pl.BlockSpec((1, width), lambda b, c: (0, 0))
    st_spec = pl.BlockSpec((bb, HG_HEADS, dk, dk), lambda b, c: (b, 0, 0, 0))
    in_specs = [spec(cols[0]), spec(cols[1]), spec(cols[2]), spec(cols[3]), vec, vec]
    args = [proj3, proj3, proj3, proj3, lb.reshape(1, width), norm_w.reshape(1, width)]
    if has_state:
        in_specs.append(st_spec)
        args.append(s0)
    return pl.pallas_call(
        functools.partial(_hgrn_kernel, t_valid=t_valid, has_state=has_state),
        grid=(bsz // bb, t // chunk),
        in_specs=in_specs,
        out_specs=[pl.BlockSpec((bb, chunk, width), lambda b, c: (b, c, 0)), st_spec],
        out_shape=[jax.ShapeDtypeStruct((bsz, t, width), BF16),
                   jax.ShapeDtypeStruct((bsz, HG_HEADS, dk, dk), F32)],
        scratch_shapes=[pltpu.VMEM((bb, HG_HEADS, dk, dk), F32)],
        compiler_params=_cparams("parallel", "arbitrary"),
        name="hgrn_state" if has_state else "hgrn_prompt",
    )(*args)


def _out_proj_kernel(x_ref, a_ref, oh_ref, wo_ref, nw_ref, h_ref, hnt_ref):
    aw = a_ref.shape[1]
    mix = _dot(a_ref[...], wo_ref[:aw, :]) + _dot(oh_ref[...], wo_ref[aw:, :])
    h = x_ref[...] + mix
    h_ref[...] = h
    hn = h * lax.rsqrt(jnp.mean(h * h, axis=-1, keepdims=True) + EPS) * nw_ref[...]
    hnt_ref[...] = hn.T.astype(hnt_ref.dtype)


def _out_proj(x2d, attn, oh, wo_bf16, norm_w, tm):
    t, d = x2d.shape
    aw, hw = attn.shape[1], oh.shape[1]
    return pl.pallas_call(
        _out_proj_kernel,
        grid=(t // tm,),
        in_specs=[pl.BlockSpec((tm, d), lambda i: (i, 0)),
                  pl.BlockSpec((tm, aw), lambda i: (i, 0)),
                  pl.BlockSpec((tm, hw), lambda i: (i, 0)),
                  pl.BlockSpec((aw + hw, d), lambda i: (0, 0)),
                  pl.BlockSpec((1, d), lambda i: (0, 0))],
        out_specs=[pl.BlockSpec((tm, d), lambda i: (i, 0)),
                   pl.BlockSpec((d, tm), lambda i: (0, i))],
        out_shape=[jax.ShapeDtypeStruct((t, d), F32),
                   jax.ShapeDtypeStruct((d, t), BF16)],
        compiler_params=_cparams("parallel"),
        name="out_proj",
    )(x2d, attn, oh, wo_bf16, norm_w.reshape(1, d))


def _peer_scores_kernel(hnt_ref, wqt_ref, keys_ref, sc_ref):
    qt = _dot(wqt_ref[...], hnt_ref[...])
    half = keys_ref.shape[2]
    for hc in range(keys_ref.shape[0]):
        s = _dot(keys_ref[hc], qt[hc * half:(hc + 1) * half, :].astype(BF16))
        for lt in range(sc_ref.shape[0]):
            sc_ref[lt, hc] = s[:, lt * LANES:(lt + 1) * LANES]


def _peer_scores(hnt, wqt_bf16, keys_bf16, tb):
    d, t = hnt.shape
    nhc, nk, half = keys_bf16.shape
    return pl.pallas_call(
        _peer_scores_kernel,
        grid=(t // tb,),
        in_specs=[pl.BlockSpec((d, tb), lambda i: (0, i)),
                  pl.BlockSpec((nhc * half, d), lambda i: (0, 0)),
                  pl.BlockSpec((nhc, nk, half), lambda i: (0, 0, 0))],
        out_specs=pl.BlockSpec((tb // LANES, nhc, nk, LANES), lambda i: (i, 0, 0, 0)),
        out_shape=jax.ShapeDtypeStruct((t // LANES, nhc, nk, LANES), F32),
        compiler_params=_cparams("parallel"),
        name="peer_scores",
    )(hnt, wqt_bf16, keys_bf16)


def _sort16_pairs():
    def merge(lo, hi, r):
        step = r * 2
        if step < hi - lo:
            yield from merge(lo, hi, step)
            yield from merge(lo + r, hi, step)
            yield from [(i, i + r) for i in range(lo + r, hi - r, step)]
        else:
            yield (lo, lo + r)

    def sort(lo, hi):
        if hi - lo >= 1:
            mid = lo + (hi - lo) // 2
            yield from sort(lo, mid)
            yield from sort(mid + 1, hi)
            yield from merge(lo, hi, 1)

    return tuple(sort(0, PEER_TOPK - 1))


_SORT16 = _sort16_pairs()


def _bitonic_to_sorted(z):
    z = list(z)
    d = PEER_TOPK // 2
    while d >= 1:
        for i in range(PEER_TOPK):
            if i & d == 0:
                hi, lo = jnp.maximum(z[i], z[i + d]), jnp.minimum(z[i], z[i + d])
                z[i], z[i + d] = hi, lo
        d //= 2
    return z


def _merge_bitonic(top, other):
    z = list(top)
    m = len(other)
    for r in range(PEER_TOPK - m, PEER_TOPK):
        z[r] = jnp.maximum(top[r], other[PEER_TOPK - 1 - r])
    return z


def _top16_desc(x):
    n = x.shape[0] // SUBLANES
    xs = [x[g * SUBLANES:(g + 1) * SUBLANES, :] for g in range(n)]
    for i, j in _SORT16:
        xs[i], xs[j] = jnp.maximum(xs[i], xs[j]), jnp.minimum(xs[i], xs[j])
    shift = SUBLANES // 2
    while shift >= 1:
        ys = [pltpu.roll(v, shift, axis=0) for v in xs]
        xs = _bitonic_to_sorted(_merge_bitonic(xs, ys))
        shift //= 2
    return xs


def _peer_select(sc_ref, thr_scr, pw_scr, q_scr, lt):
    assert PEER_HEADS == SUBLANES
    sub = lax.broadcasted_iota(jnp.int32, (SUBLANES, LANES), 0)
    a = b = None
    for h in range(PEER_HEADS):
        a_h = _top16_desc(sc_ref[lt, 2 * h])
        b_h = _top16_desc(sc_ref[lt, 2 * h + 1])
        a = a_h if h == 0 else [jnp.where(sub == h, new, old) for new, old in zip(a_h, a)]
        b = b_h if h == 0 else [jnp.where(sub == h, new, old) for new, old in zip(b_h, b)]
    lists = [[a[r] + b[c] for c in range(PEER_TOPK // (r + 1))] for r in range(SUBLANES)]
    lists.append([a[r] + b[0] for r in range(SUBLANES, PEER_TOPK)])
    top = lists[0]
    for other in lists[1:-1]:
        top = _bitonic_to_sorted(_merge_bitonic(top, other))
    z = _merge_bitonic(top, lists[-1])
    tau = functools.reduce(jnp.minimum, z)
    best = a[0] + b[0]
    zsum = jnp.zeros_like(tau)
    inf = jnp.full_like(tau, jnp.inf)
    thr_rank = []
    for r, cand in enumerate(lists):
        hits = [v >= tau for v in cand]
        for v, hit in zip(cand, hits):
            zsum = zsum + jnp.where(hit, jnp.exp(v - best), 0.0)
        if r < SUBLANES:
            t = inf
            for c, hit in enumerate(hits):
                t = jnp.where(hit, b[c], t)
            thr_rank.append(t)
        else:
            thr_rank.extend(jnp.where(hit, b[0], inf) for hit in hits)
    inv = 0.5 / zsum
    for h in range(PEER_HEADS):
        own = lambda v: jnp.broadcast_to(v[h:h + 1, :], (SUBLANES, LANES))
        a_h = [own(v) for v in a]
        thr_h = [own(v) for v in thr_rank]
        inv_h, b0_h = own(inv), own(b[0])
        s0 = sc_ref[lt, 2 * h]
        s1 = sc_ref[lt, 2 * h + 1]
        for g in range(s0.shape[0] // SUBLANES):
            rows = slice(g * SUBLANES, (g + 1) * SUBLANES)
            x0 = s0[rows, :]
            thr = jnp.full_like(x0, jnp.inf)
            for r in range(PEER_TOPK):
                thr = jnp.where(x0 == a_h[r], thr_h[r], thr)
            thr_scr[lt, h, rows, :] = thr
            pw_scr[lt, h, rows, :] = jnp.exp(x0 - a_h[0]) * inv_h
            q_scr[lt, h, rows, :] = jnp.exp(s1[rows, :] - b0_h)


MXU_TILE = 256
MXU_COUNT = 2
ACC_ROWS = 512
ACC_PIECE = 32
ACC_PRE = 0
ACC_OUT = ACC_ROWS // 4


def _peer_dense_kernel(sc_ref, hnt_ref, u_ref, vt_ref, res_ref, nw_ref, y_ref, thr_scr, pw_scr, q_scr,
                       h0_scr, h1_scr, g0_scr, g1_scr, yt_ref):
    s = pl.program_id(1)
    n_e = pl.num_programs(1) - 2
    d, tb = hnt_ref.shape
    n_lt, _, nk, _ = sc_ref.shape
    eb = u_ref.shape[0]
    n_i = eb // nk
    assert tb == MXU_COUNT * MXU_TILE and eb % ACC_ROWS == 0 and d % ACC_ROWS == 0

    @pl.when(s == 0)
    def _select():
        def body(lt, carry):
            _peer_select(sc_ref, thr_scr, pw_scr, q_scr, lt)
            return carry
        lax.fori_loop(0, n_lt, body, 0)
        yt_ref[...] = jnp.zeros_like(yt_ref)
        g0_scr[...] = jnp.zeros_like(g0_scr)
        g1_scr[...] = jnp.zeros_like(g1_scr)

    i0 = pl.multiple_of(jnp.clip(s - 1, 0, n_e - 1) * n_i, SUBLANES)

    def gate_steps(lt, ii, h_r, g_w):
        lanes = slice(lt * LANES, (lt + 1) * LANES)
        rows = slice(ii * nk, (ii + 1) * nk)
        tiled = (nk // SUBLANES, SUBLANES, LANES)
        state = {"w": jnp.zeros(tiled, F32)}

        def head(h):
            thr = thr_scr[lt, h, pl.ds(i0 + ii, SUBLANES, stride=0), :]
            pw = pw_scr[lt, h, pl.ds(i0 + ii, SUBLANES, stride=0), :]
            s1 = sc_ref[lt, 2 * h + 1].reshape(tiled)
            state["w"] = state["w"] + jnp.where(s1 >= thr[None], q_scr[lt, h].reshape(tiled), 0.0) * pw[None]
            if h == PEER_HEADS - 1:
                x = h_r[rows, lanes]
                act = x + x * lax.erf(x * (2.0 ** -0.5))
                g_w[rows, lanes] = (state["w"].reshape(nk, LANES) * act).astype(g_w.dtype)

        return [functools.partial(head, h) for h in range(PEER_HEADS)]

    def mxu_steps(kind, c, k, reg, h_w, g_r):
        lhs_ref, rhs_ref, acc, n_k = ((u_ref, hnt_ref, ACC_PRE, d // MXU_TILE) if kind == "pre"
                                      else (vt_ref, g_r, ACC_OUT, eb // MXU_TILE))
        kc = slice(k * MXU_TILE, (k + 1) * MXU_TILE)

        def push():
            for q in range(MXU_COUNT):
                pltpu.matmul_push_rhs(rhs_ref[kc, q * MXU_TILE:(q + 1) * MXU_TILE], staging_register=reg,
                                      mxu_index=q)

        def piece(p):
            r0 = c * ACC_ROWS + p * ACC_PIECE
            lhs = lhs_ref[r0:r0 + ACC_PIECE, kc]
            for q in range(MXU_COUNT):
                pltpu.matmul_acc_lhs(acc + p * ACC_PIECE // 4, lhs, q, load_staged_rhs=reg if p == 0 else None)

        def pop(p):
            rows = slice(c * ACC_ROWS + p * ACC_PIECE, c * ACC_ROWS + (p + 1) * ACC_PIECE)
            for q in range(MXU_COUNT):
                cols = slice(q * MXU_TILE, (q + 1) * MXU_TILE)
                res = pltpu.matmul_pop(acc + p * ACC_PIECE // 4, (ACC_PIECE, MXU_TILE), F32, q)
                if kind == "pre":
                    h_w[rows, cols] = res
                else:
                    yt_ref[rows, cols] += res

        n_p = ACC_ROWS // ACC_PIECE
        pops = [functools.partial(pop, p) for p in range(n_p)] if k == n_k - 1 else []
        return push, [functools.partial(piece, p) for p in range(n_p)], pops

    def stage(h_w, h_r, g_w, g_r, pre, gating, out):
        pre_groups = [("pre", c, k) for c in range(eb // ACC_ROWS) for k in range(d // MXU_TILE)] if pre else []
        out_groups = [("out", c, k) for c in range(d // ACC_ROWS) for k in range(eb // MXU_TILE)] if out else []
        if pre and out:
            order = [g for pair in zip(pre_groups, out_groups) for g in pair]
        else:
            order = pre_groups + out_groups
        steps = [mxu_steps(kind, c, k, gi % 2, h_w, g_r) for gi, (kind, c, k) in enumerate(order)]
        mxu = [steps[0][0]]
        lagged = []
        for gi, (_, pieces, pops) in enumerate(steps):
            half = len(pieces) // 2
            for p, piece in enumerate(pieces):
                if p == half and gi + 1 < len(steps):
                    mxu.append(steps[gi + 1][0])
                mxu.append(piece)
                if lagged:
                    mxu.append(lagged.pop(0))
            same_acc_next = gi + 1 < len(steps) and order[gi + 1][0] == order[gi][0]
            if same_acc_next or gi + 1 == len(steps):
                mxu.extend(pops)
            else:
                lagged = list(pops)
        vpu = [t for lt in range(n_lt) for ii in range(n_i) for t in gate_steps(lt, ii, h_r, g_w)] if gating else []
        im = iv = 0
        while im < len(mxu) or iv < len(vpu):
            if iv >= len(vpu) or (im < len(mxu) and im * len(vpu) <= iv * len(mxu)):
                mxu[im]()
                im += 1
            else:
                vpu[iv]()
                iv += 1

    last = n_e + 1

    @pl.when(s == 0)
    def _first():
        stage(h0_scr, None, None, None, True, False, False)

    @pl.when((s > 0) & (s < last) & (s % 2 == 0))
    def _even():
        stage(h0_scr, h1_scr, g1_scr, g0_scr, True, True, True)

    @pl.when((s > 0) & (s < last) & (s % 2 == 1))
    def _odd():
        stage(h1_scr, h0_scr, g0_scr, g1_scr, True, True, True)

    def finish(g_r):
        stage(None, None, None, g_r, False, False, True)
        y = res_ref[...] + yt_ref[...].T
        y_ref[...] = y * lax.rsqrt(jnp.mean(y * y, axis=-1, keepdims=True) + EPS) * nw_ref[...]

    @pl.when((s == last) & (s % 2 == 0))
    def _last_even():
        finish(g0_scr)

    @pl.when((s == last) & (s % 2 == 1))
    def _last_odd():
        finish(g1_scr)


def _peer_dense(sc, hnt, u_bf16, vt_bf16, resid, norm_w, tb, eb):
    _, nhc, nk, _ = sc.shape
    d, t = hnt.shape
    n_exp = u_bf16.shape[0]
    n_lt = tb // LANES
    assert eb == SUBLANES * nk and n_exp == nk * nk and t % tb == 0 and tb % LANES == 0
    sel = pltpu.VMEM((n_lt, PEER_HEADS, nk, LANES), F32)
    pre = pltpu.VMEM((eb, tb + LANES), F32)
    gated = pltpu.VMEM((eb, tb), BF16)
    n_e = n_exp // eb
    return pl.pallas_call(
        _peer_dense_kernel,
        grid=(t // tb, n_e + 2),
        in_specs=[pl.BlockSpec((n_lt, nhc, nk, LANES), lambda i, s: (i, 0, 0, 0)),
                  pl.BlockSpec((d, tb), lambda i, s: (0, i)),
                  pl.BlockSpec((eb, d), lambda i, s: (jnp.minimum(s, n_e - 1), 0)),
                  pl.BlockSpec((d, eb), lambda i, s: (0, jnp.clip(s - 2, 0, n_e - 1))),
                  pl.BlockSpec((tb, d), lambda i, s: (i, 0)),
                  pl.BlockSpec((1, d), lambda i, s: (0, 0))],
        out_specs=pl.BlockSpec((tb, d), lambda i, s: (i, 0)),
        out_shape=jax.ShapeDtypeStruct((t, d), F32),
        scratch_shapes=[sel, sel, sel, pre, pre, gated, gated, pltpu.VMEM((d, tb), F32)],
        compiler_params=_cparams("parallel", "arbitrary"),
        name="peer_dense",
    )(sc, hnt, u_bf16, vt_bf16, resid, norm_w.reshape(1, d))


TOKEN_BLOCK = 512
EXPERT_BLOCK = 1024
SAMPLE_T_PAD = 16


def _ffn(x2d, attn, oh, w, tb):
    h, hnt = _out_proj(x2d, attn, oh, w['wo'], w['norm_ffn'], tb)
    sc = _peer_scores(hnt, w['wqt'], w['keys'], tb)
    return _peer_dense(sc, hnt, w['u'], w['vt'], h, w['norm_final'], tb, EXPERT_BLOCK)


def kernel(x_prompt, x_sample, cache_k_win, cache_v_win, state_hgrn, norm_mix_w, w_in, attn_sinks,
           rel_bias_table, hg_lb, hg_norm_w, w_o, norm_ffn_w, peer_w_q, peer_sub_keys, peer_u, peer_v,
           norm_final_w):
    bsz, seq, d = x_prompt.shape
    dbsz, dseq, _ = x_sample.shape
    aw = ATTN_HEADS * HEAD_DIM
    kw = ATTN_KV_HEADS * HEAD_DIM
    hw = hg_norm_w.shape[1]
    wb = cache_k_win.shape[2]

    wi = w_in[0]
    w_in_r = jnp.concatenate([wi[:, :aw], wi[:, aw + 2 * kw:], wi[:, aw:aw + 2 * kw]], axis=1).astype(BF16)
    col_k = (aw + 4 * hw) // kw
    col_v = col_k + 1
    hg_cols = (1, 2, 3, 4)
    lb = jax.nn.softmax(hg_lb.astype(F32), axis=0)[0]
    nhc = PEER_HEADS * 2
    w = {
        'wo': w_o[0].astype(BF16),
        'norm_ffn': norm_ffn_w[0],
        'wqt': peer_w_q[0].astype(BF16).T,
        'keys': peer_sub_keys[0].reshape(nhc, peer_sub_keys.shape[3], peer_sub_keys.shape[4]).astype(BF16),
        'u': peer_u[0].astype(BF16),
        'vt': peer_v[0].astype(BF16).T,
        'norm_final': norm_final_w,
    }
    sinks = attn_sinks[0].astype(F32)

    proj_p = _in_proj(x_prompt.reshape(bsz * seq, d), norm_mix_w[0], w_in_r, TOKEN_BLOCK)
    proj_p3 = proj_p.reshape(bsz, seq, -1)
    blk = WINDOW
    dist_p = (jnp.arange(blk)[:, None] + blk) - jnp.arange(2 * blk)[None, :]
    in_win = (dist_p >= 0) & (dist_p <= WINDOW)
    has_prev = jnp.arange(2 * blk)[None, :] >= blk
    bias_p = jnp.stack([_masked_bias(rel_bias_table, blk, 2 * blk, blk, in_win & has_prev),
                        _masked_bias(rel_bias_table, blk, 2 * blk, blk, in_win)])
    attn_p = _swa_prompt(proj_p3, sinks, bias_p, 0, col_k, col_v)
    oh_p, st_p = _hgrn(proj_p3, lb, hg_norm_w[0], None, hg_cols, HG_CHUNK, HG_CHUNK, bsz)
    y_p = _ffn(x_prompt.reshape(bsz * seq, d), attn_p.reshape(bsz * seq, aw),
               oh_p.reshape(bsz * seq, hw), w, TOKEN_BLOCK)
    k_off = aw + 4 * hw
    wp = min(WINDOW, seq)
    k_win_p = proj_p3[:, seq - wp:, k_off:k_off + kw].reshape(1, bsz, wp, ATTN_KV_HEADS, HEAD_DIM)
    v_win_p = proj_p3[:, seq - wp:, k_off + kw:k_off + 2 * kw].reshape(1, bsz, wp, ATTN_KV_HEADS, HEAD_DIM)

    tp = SAMPLE_T_PAD
    xs_pad = jnp.pad(x_sample, ((0, 0), (0, tp - dseq), (0, 0)))
    proj_s3 = _in_proj(xs_pad.reshape(dbsz * tp, d), norm_mix_w[0], w_in_r, TOKEN_BLOCK).reshape(dbsz, tp, -1)
    k_new = proj_s3[:, :dseq, k_off:k_off + kw]
    v_new = proj_s3[:, :dseq, k_off + kw:k_off + 2 * kw]
    k_cache = cache_k_win[0].reshape(dbsz, wb, kw)
    v_cache = cache_v_win[0].reshape(dbsz, wb, kw)
    kk = jnp.concatenate([k_cache, k_new], axis=1)
    vv = jnp.concatenate([v_cache, v_new], axis=1)
    q_pos = jnp.arange(tp)[:, None]
    k_pos = jnp.arange(wb + tp)[None, :]
    dist_s = wb + q_pos - k_pos
    mask_s = (dist_s >= 0) & (dist_s <= WINDOW) & (q_pos < dseq) & (k_pos < wb + dseq)
    bias_s = _masked_bias(rel_bias_table, tp, wb + tp, wb, mask_s)
    attn_s = _swa_sample(proj_s3, k_cache, v_cache, sinks, bias_s[:, :, :wb], bias_s[:, :, wb:],
                         0, col_k, col_v, 16)
    oh_s, st_s = _hgrn(proj_s3, lb, hg_norm_w[0], state_hgrn[0], hg_cols, tp, dseq, 8)
    y_s = _ffn(x_sample.reshape(dbsz * dseq, d), attn_s[:, :dseq].reshape(dbsz * dseq, aw),
               oh_s[:, :dseq].reshape(dbsz * dseq, hw), w, TOKEN_BLOCK)
    k_win_s = kk[:, dseq:].reshape(1, dbsz, wb, ATTN_KV_HEADS, HEAD_DIM)
    v_win_s = vv[:, dseq:].reshape(1, dbsz, wb, ATTN_KV_HEADS, HEAD_DIM)

    return (y_p.reshape(bsz, seq, d), y_s.reshape(dbsz, dseq, d), k_win_p, v_win_p, st_p[None],
            k_win_s, v_win_s, st_s[None])
```

```python
import functools
import math

import jax
import jax.numpy as jnp
from jax import lax
from jax.experimental import pallas as pl
from jax.experimental.pallas import tpu as pltpu

F32 = jnp.float32
BF16 = jnp.bfloat16

EPS = 1e-6
NEG = -1e30

ATTN_HEADS = 8
ATTN_KV_HEADS = 2
HEAD_DIM = 64
WINDOW = 128
REL_BUCKETS = 32
HG_HEADS = 4
HG_CHUNK = 64
PEER_HEADS = 8
PEER_TOPK = 16

LANES = 128
SUBLANES = 8
VMEM_LIMIT = 56 * 1024 * 1024


def _cparams(*sem):
    return pltpu.CompilerParams(dimension_semantics=sem, vmem_limit_bytes=VMEM_LIMIT)


def _nt(a, b):
    return lax.dot_general(a, b, (((1,), (1,)), ((), ())), preferred_element_type=F32)


def _tn(a, b):
    return lax.dot_general(a, b, (((0,), (0,)), ((), ())), preferred_element_type=F32)


def _dot(a, b):
    return jnp.dot(a, b, preferred_element_type=F32)


def _sigmoid(x):
    return 1.0 / (1.0 + jnp.exp(-x))


def _in_proj_kernel(x_ref, nw_ref, w_ref, o_ref):
    x = x_ref[...]
    xn = x * lax.rsqrt(jnp.mean(x * x, axis=-1, keepdims=True) + EPS) * nw_ref[...]
    o_ref[...] = _dot(xn.astype(BF16), w_ref[...])


def _in_proj(x2d, norm_w, w_bf16, tm):
    t, d = x2d.shape
    n = w_bf16.shape[1]
    return pl.pallas_call(
        _in_proj_kernel,
        grid=(t // tm,),
        in_specs=[pl.BlockSpec((tm, d), lambda i: (i, 0)),
                  pl.BlockSpec((1, d), lambda i: (0, 0)),
                  pl.BlockSpec((d, n), lambda i: (0, 0))],
        out_specs=pl.BlockSpec((tm, n), lambda i: (i, 0)),
        out_shape=jax.ShapeDtypeStruct((t, n), F32),
        compiler_params=_cparams("parallel"),
        name="in_proj",
    )(x2d, norm_w.reshape(1, d), w_bf16)


def _t5_bucket(dist):
    n = jnp.maximum(dist, 0)
    max_exact = REL_BUCKETS // 2
    nf = jnp.maximum(n, 1).astype(F32)
    large = max_exact + (jnp.log(nf / max_exact) / math.log(WINDOW / max_exact)
                         * (REL_BUCKETS - max_exact)).astype(jnp.int32)
    large = jnp.minimum(large, REL_BUCKETS - 1)
    return jnp.where(n < max_exact, n, large)


def _masked_bias(table, n_q, n_k, offset, mask):
    h = table.shape[1]
    diag = jnp.arange(n_q + n_k - 1) - (n_k - 1) + offset
    per_diag = table.astype(F32)[_t5_bucket(diag)].T
    w = jnp.pad(per_diag[:, ::-1], ((0, 0), (0, 1)))
    p = n_q + n_k
    skew = jnp.tile(w, (1, n_q))[:, :n_q * (p - 1)].reshape(h, n_q, p - 1)
    return jnp.where(mask[None], skew[:, :, n_q - 1:n_q - 1 + n_k], NEG)


def _swa_prompt_kernel(sink_ref, q_ref, kp_ref, kc_ref, vp_ref, vc_ref, bias0_ref, bias1_ref, o_ref):
    scale = HEAD_DIM ** -0.5
    assert math.frexp(scale)[0] == 0.5
    group = ATTN_HEADS // ATTN_KV_HEADS
    blk = kp_ref.shape[1]
    assert 2 * HEAD_DIM == LANES and kp_ref.shape[2] == LANES and ATTN_KV_HEADS == 2 and group % 2 == 0
    kk = [jnp.concatenate([kp_ref[0], kc_ref[0, :blk]], axis=0), kc_ref[0]]
    vv = [jnp.concatenate([vp_ref[0], vc_ref[0, :blk]], axis=0), vc_ref[0]]
    bias = [bias0_ref, bias1_ref]
    low = lax.broadcasted_iota(jnp.int32, kk[0].shape, 1) < HEAD_DIM

    def halves(x, kvh):
        own = jnp.where(low if kvh == 0 else ~low, x, 0.0)
        other = pltpu.roll(own, HEAD_DIM, axis=1)
        lo, hi = (own, other) if kvh == 0 else (other, own)
        return lo.astype(BF16), hi.astype(BF16)

    n_sub = len(kk)
    k_half = [[halves(kk[j], kvh) for kvh in range(ATTN_KV_HEADS)] for j in range(n_sub)]
    v_half = [[halves(vv[j], kvh) for kvh in range(ATTN_KV_HEADS)] for j in range(n_sub)]
    units = [(j, h) for j in range(n_sub) for h in range(ATTN_HEADS)]
    kv_of = lambda h: h // group
    qt = [[(q_ref[0, j * blk:(j + 1) * blk, t * LANES:(t + 1) * LANES] * scale).astype(BF16)
           for t in range(ATTN_HEADS // 2)] for j in range(n_sub)]
    s = {(j, h): _nt(qt[j][h // 2], k_half[j][kv_of(h)][h % 2]) + bias[j][0, h] for j, h in units}
    m = {u: jnp.maximum(jnp.max(s[u], axis=-1, keepdims=True), sink_ref[u[1]]) for u in units}
    p = {u: jnp.exp(s[u] - m[u]) for u in units}
    den = {u: jnp.sum(p[u], axis=-1, keepdims=True) + jnp.exp(sink_ref[u[1]] - m[u]) for u in units}
    p = {u: (p[u] * (1.0 / den[u])).astype(BF16) for u in units}
    for j in range(n_sub):
        for tile in range(ATTN_HEADS // 2):
            kvh = kv_of(2 * tile)
            o = _dot(p[j, 2 * tile], v_half[j][kvh][0]) + _dot(p[j, 2 * tile + 1], v_half[j][kvh][1])
            o_ref[0, j * blk:(j + 1) * blk, tile * LANES:(tile + 1) * LANES] = o.astype(o_ref.dtype)


def _swa_prompt(proj3, sinks, bias, col_q, col_k, col_v):
    bsz, seq, _ = proj3.shape
    blk = WINDOW
    aw = ATTN_HEADS * HEAD_DIM
    kw = ATTN_KV_HEADS * HEAD_DIM
    prev = lambda b, n: (b, jnp.maximum(2 * n - 1, 0))
    bias_spec = lambda pick: pl.BlockSpec((1, ATTN_HEADS, blk, 2 * blk), lambda b, n: (pick(n), 0, 0, 0))
    return pl.pallas_call(
        _swa_prompt_kernel,
        grid=(bsz, seq // (2 * blk)),
        in_specs=[pl.BlockSpec(memory_space=pltpu.SMEM),
                  pl.BlockSpec((1, 2 * blk, aw), lambda b, n: (b, n, col_q)),
                  pl.BlockSpec((1, blk, kw), lambda b, n: prev(b, n) + (col_k,)),
                  pl.BlockSpec((1, 2 * blk, kw), lambda b, n: (b, n, col_k)),
                  pl.BlockSpec((1, blk, kw), lambda b, n: prev(b, n) + (col_v,)),
                  pl.BlockSpec((1, 2 * blk, kw), lambda b, n: (b, n, col_v)),
                  bias_spec(lambda n: jnp.minimum(n, 1)),
                  bias_spec(lambda n: 1)],
        out_specs=pl.BlockSpec((1, 2 * blk, aw), lambda b, n: (b, n, 0)),
        out_shape=jax.ShapeDtypeStruct((bsz, seq, aw), BF16),
        compiler_params=_cparams("parallel", "arbitrary"),
        name="swa_prompt",
    )(sinks, proj3, proj3, proj3, proj3, proj3, bias, bias)


def _swa_sample_kernel(sink_ref, q_ref, kn_ref, vn_ref, kc_ref, vc_ref, bias_c_ref, bias_n_ref, o_ref):
    scale = HEAD_DIM ** -0.5
    group = ATTN_HEADS // ATTN_KV_HEADS
    for h in range(ATTN_HEADS):
        kv = slice((h // group) * HEAD_DIM, (h // group + 1) * HEAD_DIM)
        qh = q_ref[:, :, h * HEAD_DIM:(h + 1) * HEAD_DIM].astype(BF16)
        sc = jnp.einsum('bqd,bkd->bqk', qh, kc_ref[:, :, kv].astype(BF16),
                        preferred_element_type=F32) * scale + bias_c_ref[h][None]
        sn = jnp.einsum('bqd,bkd->bqk', qh, kn_ref[:, :, kv].astype(BF16),
                        preferred_element_type=F32) * scale + bias_n_ref[h][None]
        sink = sink_ref[h]
        m = jnp.maximum(jnp.maximum(jnp.max(sc, axis=-1, keepdims=True),
                                    jnp.max(sn, axis=-1, keepdims=True)), sink)
        pc = jnp.exp(sc - m)
        pn = jnp.exp(sn - m)
        den = (jnp.sum(pc, axis=-1, keepdims=True) + jnp.sum(pn, axis=-1, keepdims=True)
               + jnp.exp(sink - m))
        o = (jnp.einsum('bqk,bkd->bqd', pc.astype(BF16), vc_ref[:, :, kv].astype(BF16), preferred_element_type=F32)
             + jnp.einsum('bqk,bkd->bqd', pn.astype(BF16), vn_ref[:, :, kv].astype(BF16),
                          preferred_element_type=F32)) / den
        o_ref[:, :, h * HEAD_DIM:(h + 1) * HEAD_DIM] = o.astype(o_ref.dtype)


def _swa_sample(proj3, k_cache, v_cache, sinks, bias_c, bias_n, col_q, col_k, col_v, bb):
    bsz, tp, _ = proj3.shape
    wb = k_cache.shape[1]
    aw = ATTN_HEADS * HEAD_DIM
    kw = ATTN_KV_HEADS * HEAD_DIM
    return pl.pallas_call(
        _swa_sample_kernel,
        grid=(bsz // bb,),
        in_specs=[pl.BlockSpec(memory_space=pltpu.SMEM),
                  pl.BlockSpec((bb, tp, aw), lambda b: (b, 0, col_q)),
                  pl.BlockSpec((bb, tp, kw), lambda b: (b, 0, col_k)),
                  pl.BlockSpec((bb, tp, kw), lambda b: (b, 0, col_v)),
                  pl.BlockSpec((bb, wb, kw), lambda b: (b, 0, 0)),
                  pl.BlockSpec((bb, wb, kw), lambda b: (b, 0, 0)),
                  pl.BlockSpec((ATTN_HEADS, tp, wb), lambda b: (0, 0, 0)),
                  pl.BlockSpec((ATTN_HEADS, tp, tp), lambda b: (0, 0, 0))],
        out_specs=pl.BlockSpec((bb, tp, aw), lambda b: (b, 0, 0)),
        out_shape=jax.ShapeDtypeStruct((bsz, tp, aw), BF16),
        compiler_params=_cparams("parallel"),
        name="swa_sample",
    )(sinks, proj3, proj3, proj3, k_cache, v_cache, bias_c, bias_n)


def _split3(x):
    hi = x.astype(BF16)
    r = x - hi.astype(F32)
    mid = r.astype(BF16)
    lo = (r - mid.astype(F32)).astype(BF16)
    return hi, mid, lo


def _hgrn_kernel(*refs, t_valid, has_state):
    if has_state:
        q_ref, f_ref, i_ref, g_ref, lb_ref, nw_ref, s0_ref, o_ref, s_ref, st_scr = refs
    else:
        q_ref, f_ref, i_ref, g_ref, lb_ref, nw_ref, o_ref, s_ref, st_scr = refs
    bb, chunk, width = q_ref.shape
    dk = width // HG_HEADS
    c = pl.program_id(1)

    @pl.when(c == 0)
    def _init():
        if has_state:
            st_scr[...] = s0_ref[...]
        else:
            st_scr[...] = jnp.zeros_like(st_scr)

    row = lax.broadcasted_iota(jnp.int32, (chunk, chunk), 0)
    col = lax.broadcasted_iota(jnp.int32, (chunk, chunk), 1)
    causal = row >= col
    tri = jnp.where(causal, 1.0, 0.0).astype(BF16)
    row_w = lax.broadcasted_iota(jnp.int32, (chunk, width), 0)
    valid = row_w < t_valid
    ones_rows = jnp.where(lax.broadcasted_iota(jnp.int32, (chunk, dk), 0) < 3, 1.0, 0.0).astype(BF16)
    mid_row = chunk // 2

    def body(b, carry):
        qx = q_ref[b]
        q = qx * _sigmoid(qx)
        lb = lb_ref[...]
        f = lb + (1.0 - lb) * _sigmoid(f_ref[b])
        k = 1.0 - f
        lg = jnp.log(f)
        if t_valid < chunk:
            k = jnp.where(valid, k, 0.0)
            lg = jnp.where(valid, lg, 0.0)
        v = i_ref[b].astype(BF16)
        cum = sum(_dot(tri, part) for part in _split3(lg))
        cum_mid = cum[mid_row:mid_row + 1, :]
        cum_last = cum[chunk - 1:chunk, :]
        qt = (q * jnp.exp(cum - cum_mid)).astype(BF16)
        kt = (k * jnp.exp(cum_mid - cum)).astype(BF16)
        qe = (q * jnp.exp(cum)).astype(BF16)
        kd = (k * jnp.exp(cum_last - cum)).astype(BF16)
        dec_rows = jnp.zeros((chunk, width), F32)
        for j, part in enumerate(_split3(jnp.exp(cum_last))):
            dec_rows = jnp.where(row_w == j, part.astype(F32), dec_rows)
        dec_rows = dec_rows.astype(BF16)
        heads = range(HG_HEADS)
        sl = [slice(h * dk, (h + 1) * dk) for h in heads]
        st = [st_scr[b, h] for h in heads]
        a = [_nt(qt[:, sl[h]], kt[:, sl[h]]) for h in heads]
        inter = [_dot(qe[:, sl[h]], st[h].astype(BF16)) for h in heads]
        upd = [_tn(kd[:, sl[h]], v[:, sl[h]]) for h in heads]
        decay = [_tn(dec_rows[:, sl[h]], ones_rows) for h in heads]
        for h in heads:
            st_scr[b, h] = st[h] * decay[h] + upd[h]
        a = [jnp.where(causal, a[h], 0.0).astype(BF16) for h in heads]
        o = [_dot(a[h], v[:, sl[h]]) + inter[h] for h in heads]
        outs = [o[h] * lax.rsqrt(jnp.mean(o[h] * o[h], axis=-1, keepdims=True) + EPS) for h in heads]
        gx = g_ref[b]
        o = jnp.concatenate(outs, axis=1) * nw_ref[...] * (gx * _sigmoid(gx))
        o_ref[b] = o.astype(o_ref.dtype)
        return carry

    lax.fori_loop(0, bb, body, 0, unroll=4)

    @pl.when(c == pl.num_programs(1) - 1)
    def _final():
        s_ref[...] = st_scr[...]


def _hgrn(proj3, lb, norm_w, s0, cols, chunk, t_valid, bb):
    bsz, t, _ = proj3.shape
    width = lb.shape[0]
    dk = width // HG_HEADS
    has_state = s0 is not None
    spec = lambda cb: pl.BlockSpec((bb, chunk, width), lambda b, c: (b, c, cb))
    vec = pl.BlockSpec((1, width), lambda b, c: (0, 0))
    st_spec = pl.BlockSpec((bb, HG_HEADS, dk, dk), lambda b, c: (b, 0, 0, 0))
    in_specs = [spec(cols[0]), spec(cols[1]), spec(cols[2]), spec(cols[3]), vec, vec]
    args = [proj3, proj3, proj3, proj3, lb.reshape(1, width), norm_w.reshape(1, width)]
    if has_state:
        in_specs.append(st_spec)
        args.append(s0)
    return pl.pallas_call(
        functools.partial(_hgrn_kernel, t_valid=t_valid, has_state=has_state),
        grid=(bsz // bb, t // chunk),
        in_specs=in_specs,
        out_specs=[pl.BlockSpec((bb, chunk, width), lambda b, c: (b, c, 0)), st_spec],
        out_shape=[jax.ShapeDtypeStruct((bsz, t, width), BF16),
                   jax.ShapeDtypeStruct((bsz, HG_HEADS, dk, dk), F32)],
        scratch_shapes=[pltpu.VMEM((bb, HG_HEADS, dk, dk), F32)],
        compiler_params=_cparams("parallel", "arbitrary"),
        name="hgrn_state" if has_state else "hgrn_prompt",
    )(*args)


def _out_proj_kernel(x_ref, a_ref, oh_ref, wo_ref, nw_ref, h_ref, hnt_ref):
    aw = a_ref.shape[1]
    mix = _dot(a_ref[...], wo_ref[:aw, :]) + _dot(oh_ref[...], wo_ref[aw:, :])
    h = x_ref[...] + mix
    h_ref[...] = h
    hn = h * lax.rsqrt(jnp.mean(h * h, axis=-1, keepdims=True) + EPS) * nw_ref[...]
    hnt_ref[...] = hn.T.astype(hnt_ref.dtype)


def _out_proj(x2d, attn, oh, wo_bf16, norm_w, tm):
    t, d = x2d.shape
    aw, hw = attn.shape[1], oh.shape[1]
    return pl.pallas_call(
        _out_proj_kernel,
        grid=(t // tm,),
        in_specs=[pl.BlockSpec((tm, d), lambda i: (i, 0)),
                  pl.BlockSpec((tm, aw), lambda i: (i, 0)),
                  pl.BlockSpec((tm, hw), lambda i: (i, 0)),
                  pl.BlockSpec((aw + hw, d), lambda i: (0, 0)),
                  pl.BlockSpec((1, d), lambda i: (0, 0))],
        out_specs=[pl.BlockSpec((tm, d), lambda i: (i, 0)),
                   pl.BlockSpec((d, tm), lambda i: (0, i))],
        out_shape=[jax.ShapeDtypeStruct((t, d), F32),
                   jax.ShapeDtypeStruct((d, t), BF16)],
        compiler_params=_cparams("parallel"),
        name="out_proj",
    )(x2d, attn, oh, wo_bf16, norm_w.reshape(1, d))


def _peer_scores_kernel(hnt_ref, wqt_ref, keys_ref, sc_ref):
    qt = _dot(wqt_ref[...], hnt_ref[...])
    half = keys_ref.shape[2]
    for hc in range(keys_ref.shape[0]):
        s = _dot(keys_ref[hc], qt[hc * half:(hc + 1) * half, :].astype(BF16))
        for lt in range(sc_ref.shape[0]):
            sc_ref[lt, hc] = s[:, lt * LANES:(lt + 1) * LANES]


def _peer_scores(hnt, wqt_bf16, keys_bf16, tb):
    d, t = hnt.shape
    nhc, nk, half = keys_bf16.shape
    return pl.pallas_call(
        _peer_scores_kernel,
        grid=(t // tb,),
        in_specs=[pl.BlockSpec((d, tb), lambda i: (0, i)),
                  pl.BlockSpec((nhc * half, d), lambda i: (0, 0)),
                  pl.BlockSpec((nhc, nk, half), lambda i: (0, 0, 0))],
        out_specs=pl.BlockSpec((tb // LANES, nhc, nk, LANES), lambda i: (i, 0, 0, 0)),
        out_shape=jax.ShapeDtypeStruct((t // LANES, nhc, nk, LANES), F32),
        compiler_params=_cparams("parallel"),
        name="peer_scores",
    )(hnt, wqt_bf16, keys_bf16)


def _sort16_pairs():
    def merge(lo, hi, r):
        step = r * 2
        if step < hi - lo:
            yield from merge(lo, hi, step)
            yield from merge(lo + r, hi, step)
            yield from [(i, i + r) for i in range(lo + r, hi - r, step)]
        else:
            yield (lo, lo + r)

    def sort(lo, hi):
        if hi - lo >= 1:
            mid = lo + (hi - lo) // 2
            yield from sort(lo, mid)
            yield from sort(mid + 1, hi)
            yield from merge(lo, hi, 1)

    return tuple(sort(0, PEER_TOPK - 1))


_SORT16 = _sort16_pairs()


def _bitonic_to_sorted(z):
    z = list(z)
    d = PEER_TOPK // 2
    while d >= 1:
        for i in range(PEER_TOPK):
            if i & d == 0:
                hi, lo = jnp.maximum(z[i], z[i + d]), jnp.minimum(z[i], z[i + d])
                z[i], z[i + d] = hi, lo
        d //= 2
    return z


def _merge_bitonic(top, other):
    z = list(top)
    m = len(other)
    for r in range(PEER_TOPK - m, PEER_TOPK):
        z[r] = jnp.maximum(top[r], other[PEER_TOPK - 1 - r])
    return z


def _top16_desc(x):
    n = x.shape[0] // SUBLANES
    xs = [x[g * SUBLANES:(g + 1) * SUBLANES, :] for g in range(n)]
    for i, j in _SORT16:
        xs[i], xs[j] = jnp.maximum(xs[i], xs[j]), jnp.minimum(xs[i], xs[j])
    shift = SUBLANES // 2
    while shift >= 1:
        ys = [pltpu.roll(v, shift, axis=0) for v in xs]
        xs = _bitonic_to_sorted(_merge_bitonic(xs, ys))
        shift //= 2
    return xs


def _peer_select(sc_ref, thr_scr, pw_scr, q_scr, lt):
    assert PEER_HEADS == SUBLANES
    sub = lax.broadcasted_iota(jnp.int32, (SUBLANES, LANES), 0)
    a = b = None
    for h in range(PEER_HEADS):
        a_h = _top16_desc(sc_ref[lt, 2 * h])
        b_h = _top16_desc(sc_ref[lt, 2 * h + 1])
        a = a_h if h == 0 else [jnp.where(sub == h, new, old) for new, old in zip(a_h, a)]
        b = b_h if h == 0 else [jnp.where(sub == h, new, old) for new, old in zip(b_h, b)]
    lists = [[a[r] + b[c] for c in range(PEER_TOPK // (r + 1))] for r in range(SUBLANES)]
    lists.append([a[r] + b[0] for r in range(SUBLANES, PEER_TOPK)])
    top = lists[0]
    for other in lists[1:-1]:
        top = _bitonic_to_sorted(_merge_bitonic(top, other))
    z = _merge_bitonic(top, lists[-1])
    tau = functools.reduce(jnp.minimum, z)
    best = a[0] + b[0]
    zsum = jnp.zeros_like(tau)
    inf = jnp.full_like(tau, jnp.inf)
    thr_rank = []
    for r, cand in enumerate(lists):
        hits = [v >= tau for v in cand]
        for v, hit in zip(cand, hits):
            zsum = zsum + jnp.where(hit, jnp.exp(v - best), 0.0)
        if r < SUBLANES:
            t = inf
            for c, hit in enumerate(hits):
                t = jnp.where(hit, b[c], t)
            thr_rank.append(t)
        else:
            thr_rank.extend(jnp.where(hit, b[0], inf) for hit in hits)
    inv = 0.5 / zsum
    for h in range(PEER_HEADS):
        own = lambda v: jnp.broadcast_to(v[h:h + 1, :], (SUBLANES, LANES))
        a_h = [own(v) for v in a]
        thr_h = [own(v) for v in thr_rank]
        inv_h, b0_h = own(inv), own(b[0])
        s0 = sc_ref[lt, 2 * h]
        s1 = sc_ref[lt, 2 * h + 1]
        for g in range(s0.shape[0] // SUBLANES):
            rows = slice(g * SUBLANES, (g + 1) * SUBLANES)
            x0 = s0[rows, :]
            thr = jnp.full_like(x0, jnp.inf)
            for r in range(PEER_TOPK):
                thr = jnp.where(x0 == a_h[r], thr_h[r], thr)
            thr_scr[lt, h, rows, :] = thr
            pw_scr[lt, h, rows, :] = jnp.exp(x0 - a_h[0]) * inv_h
            q_scr[lt, h, rows, :] = jnp.exp(s1[rows, :] - b0_h)


MXU_TILE = 256
MXU_COUNT = 2
ACC_ROWS = 512
ACC_PIECE = 32
ACC_PRE = 0
ACC_OUT = ACC_ROWS // 4


def _peer_dense_kernel(sc_ref, hnt_ref, u_ref, vt_ref, res_ref, nw_ref, y_ref, thr_scr, pw_scr, q_scr,
                       h0_scr, h1_scr, g0_scr, g1_scr, yt_ref):
    s = pl.program_id(1)
    n_e = pl.num_programs(1) - 2
    d, tb = hnt_ref.shape
    n_lt, _, nk, _ = sc_ref.shape
    eb = u_ref.shape[0]
    n_i = eb // nk
    assert tb == MXU_COUNT * MXU_TILE and eb % ACC_ROWS == 0 and d % ACC_ROWS == 0

    @pl.when(s == 0)
    def _select():
        def body(lt, carry):
            _peer_select(sc_ref, thr_scr, pw_scr, q_scr, lt)
            return carry
        lax.fori_loop(0, n_lt, body, 0)
        yt_ref[...] = jnp.zeros_like(yt_ref)
        g0_scr[...] = jnp.zeros_like(g0_scr)
        g1_scr[...] = jnp.zeros_like(g1_scr)

    i0 = pl.multiple_of(jnp.clip(s - 1, 0, n_e - 1) * n_i, SUBLANES)

    def gate_steps(lt, ii, h_r, g_w):
        lanes = slice(lt * LANES, (lt + 1) * LANES)
        rows = slice(ii * nk, (ii + 1) * nk)
        tiled = (nk // SUBLANES, SUBLANES, LANES)
        state = {"w": jnp.zeros(tiled, F32)}

        def head(h):
            thr = thr_scr[lt, h, pl.ds(i0 + ii, SUBLANES, stride=0), :]
            pw = pw_scr[lt, h, pl.ds(i0 + ii, SUBLANES, stride=0), :]
            s1 = sc_ref[lt, 2 * h + 1].reshape(tiled)
            state["w"] = state["w"] + jnp.where(s1 >= thr[None], q_scr[lt, h].reshape(tiled), 0.0) * pw[None]
            if h == PEER_HEADS - 1:
                x = h_r[rows, lanes]
                act = x + x * lax.erf(x * (2.0 ** -0.5))
                g_w[rows, lanes] = (state["w"].reshape(nk, LANES) * act).astype(g_w.dtype)

        return [functools.partial(head, h) for h in range(PEER_HEADS)]

    def mxu_steps(kind, c, k, reg, h_w, g_r):
        lhs_ref, rhs_ref, acc, n_k = ((u_ref, hnt_ref, ACC_PRE, d // MXU_TILE) if kind == "pre"
                                      else (vt_ref, g_r, ACC_OUT, eb // MXU_TILE))
        kc = slice(k * MXU_TILE, (k + 1) * MXU_TILE)

        def push():
            for q in range(MXU_COUNT):
                pltpu.matmul_push_rhs(rhs_ref[kc, q * MXU_TILE:(q + 1) * MXU_TILE], staging_register=reg,
                                      mxu_index=q)

        def piece(p):
            r0 = c * ACC_ROWS + p * ACC_PIECE
            lhs = lhs_ref[r0:r0 + ACC_PIECE, kc]
            for q in range(MXU_COUNT):
                pltpu.matmul_acc_lhs(acc + p * ACC_PIECE // 4, lhs, q, load_staged_rhs=reg if p == 0 else None)

        def pop(p):
            rows = slice(c * ACC_ROWS + p * ACC_PIECE, c * ACC_ROWS + (p + 1) * ACC_PIECE)
            for q in range(MXU_COUNT):
                cols = slice(q * MXU_TILE, (q + 1) * MXU_TILE)
                res = pltpu.matmul_pop(acc + p * ACC_PIECE // 4, (ACC_PIECE, MXU_TILE), F32, q)
                if kind == "pre":
                    h_w[rows, cols] = res
                else:
                    yt_ref[rows, cols] += res

        n_p = ACC_ROWS // ACC_PIECE
        pops = [functools.partial(pop, p) for p in range(n_p)] if k == n_k - 1 else []
        return push, [functools.partial(piece, p) for p in range(n_p)], pops

    def stage(h_w, h_r, g_w, g_r, pre, gating, out):
        pre_groups = [("pre", c, k) for c in range(eb // ACC_ROWS) for k in range(d // MXU_TILE)] if pre else []
        out_groups = [("out", c, k) for c in range(d // ACC_ROWS) for k in range(eb // MXU_TILE)] if out else []
        if pre and out:
            order = [g for pair in zip(pre_groups, out_groups) for g in pair]
        else:
            order = pre_groups + out_groups
        steps = [mxu_steps(kind, c, k, gi % 2, h_w, g_r) for gi, (kind, c, k) in enumerate(order)]
        mxu = [steps[0][0]]
        lagged = []
        for gi, (_, pieces, pops) in enumerate(steps):
            half = len(pieces) // 2
            for p, piece in enumerate(pieces):
                if p == half and gi + 1 < len(steps):
                    mxu.append(steps[gi + 1][0])
                mxu.append(piece)
                if lagged:
                    mxu.append(lagged.pop(0))
            same_acc_next = gi + 1 < len(steps) and order[gi + 1][0] == order[gi][0]
            if same_acc_next or gi + 1 == len(steps):
                mxu.extend(pops)
            else:
                lagged = list(pops)
        vpu = [t for lt in range(n_lt) for ii in range(n_i) for t in gate_steps(lt, ii, h_r, g_w)] if gating else []
        im = iv = 0
        while im < len(mxu) or iv < len(vpu):
            if iv >= len(vpu) or (im < len(mxu) and im * len(vpu) <= iv * len(mxu)):
                mxu[im]()
                im += 1
            else:
                vpu[iv]()
                iv += 1

    last = n_e + 1

    @pl.when(s == 0)
    def _first():
        stage(h0_scr, None, None, None, True, False, False)

    @pl.when((s > 0) & (s < last) & (s % 2 == 0))
    def _even():
        stage(h0_scr, h1_scr, g1_scr, g0_scr, True, True, True)

    @pl.when((s > 0) & (s < last) & (s % 2 == 1))
    def _odd():
        stage(h1_scr, h0_scr, g0_scr, g1_scr, True, True, True)

    def finish(g_r):
        stage(None, None, None, g_r, False, False, True)
        y = res_ref[...] + yt_ref[...].T
        y_ref[...] = y * lax.rsqrt(jnp.mean(y * y, axis=-1, keepdims=True) + EPS) * nw_ref[...]

    @pl.when((s == last) & (s % 2 == 0))
    def _last_even():
        finish(g0_scr)

    @pl.when((s == last) & (s % 2 == 1))
    def _last_odd():
        finish(g1_scr)


def _peer_dense(sc, hnt, u_bf16, vt_bf16, resid, norm_w, tb, eb):
    _, nhc, nk, _ = sc.shape
    d, t = hnt.shape
    n_exp = u_bf16.shape[0]
    n_lt = tb // LANES
    assert eb == SUBLANES * nk and n_exp == nk * nk and t % tb == 0 and tb % LANES == 0
    sel = pltpu.VMEM((n_lt, PEER_HEADS, nk, LANES), F32)
    pre = pltpu.VMEM((eb, tb + LANES), F32)
    gated = pltpu.VMEM((eb, tb), BF16)
    n_e = n_exp // eb
    return pl.pallas_call(
        _peer_dense_kernel,
        grid=(t // tb, n_e + 2),
        in_specs=[pl.BlockSpec((n_lt, nhc, nk, LANES), lambda i, s: (i, 0, 0, 0)),
                  pl.BlockSpec((d, tb), lambda i, s: (0, i)),
                  pl.BlockSpec((eb, d), lambda i, s: (jnp.minimum(s, n_e - 1), 0)),
                  pl.BlockSpec((d, eb), lambda i, s: (0, jnp.clip(s - 2, 0, n_e - 1))),
                  pl.BlockSpec((tb, d), lambda i, s: (i, 0)),
                  pl.BlockSpec((1, d), lambda i, s: (0, 0))],
        out_specs=pl.BlockSpec((tb, d), lambda i, s: (i, 0)),
        out_shape=jax.ShapeDtypeStruct((t, d), F32),
        scratch_shapes=[sel, sel, sel, pre, pre, gated, gated, pltpu.VMEM((d, tb), F32)],
        compiler_params=_cparams("parallel", "arbitrary"),
        name="peer_dense",
    )(sc, hnt, u_bf16, vt_bf16, resid, norm_w.reshape(1, d))


TOKEN_BLOCK = 512
EXPERT_BLOCK = 1024
SAMPLE_T_PAD = 16


def _ffn(x2d, attn, oh, w, tb):
    h, hnt = _out_proj(x2d, attn, oh, w['wo'], w['norm_ffn'], tb)
    sc = _peer_scores(hnt, w['wqt'], w['keys'], tb)
    return _peer_dense(sc, hnt, w['u'], w['vt'], h, w['norm_final'], tb, EXPERT_BLOCK)


def kernel(x_prompt, x_sample, cache_k_win, cache_v_win, state_hgrn, norm_mix_w, w_in, attn_sinks,
           rel_bias_table, hg_lb, hg_norm_w, w_o, norm_ffn_w, peer_w_q, peer_sub_keys, peer_u, peer_v,
           norm_final_w):
    bsz, seq, d = x_prompt.shape
    dbsz, dseq, _ = x_sample.shape
    aw = ATTN_HEADS * HEAD_DIM
    kw = ATTN_KV_HEADS * HEAD_DIM
    hw = hg_norm_w.shape[1]
    wb = cache_k_win.shape[2]

    wi = w_in[0]
    w_in_r = jnp.concatenate([wi[:, :aw], wi[:, aw + 2 * kw:], wi[:, aw:aw + 2 * kw]], axis=1).astype(BF16)
    col_k = (aw + 4 * hw) // kw
    col_v = col_k + 1
    hg_cols = (1, 2, 3, 4)
    lb = jax.nn.softmax(hg_lb.astype(F32), axis=0)[0]
    nhc = PEER_HEADS * 2
    w = {
        'wo': w_o[0].astype(BF16),
        'norm_ffn': norm_ffn_w[0],
        'wqt': peer_w_q[0].astype(BF16).T,
        'keys': peer_sub_keys[0].reshape(nhc, peer_sub_keys.shape[3], peer_sub_keys.shape[4]).astype(BF16),
        'u': peer_u[0].astype(BF16),
        'vt': peer_v[0].astype(BF16).T,
        'norm_final': norm_final_w,
    }
    sinks = attn_sinks[0].astype(F32)

    proj_p = _in_proj(x_prompt.reshape(bsz * seq, d), norm_mix_w[0], w_in_r, TOKEN_BLOCK)
    proj_p3 = proj_p.reshape(bsz, seq, -1)
    blk = WINDOW
    dist_p = (jnp.arange(blk)[:, None] + blk) - jnp.arange(2 * blk)[None, :]
    in_win = (dist_p >= 0) & (dist_p <= WINDOW)
    has_prev = jnp.arange(2 * blk)[None, :] >= blk
    bias_p = jnp.stack([_masked_bias(rel_bias_table, blk, 2 * blk, blk, in_win & has_prev),
                        _masked_bias(rel_bias_table, blk, 2 * blk, blk, in_win)])
    attn_p = _swa_prompt(proj_p3, sinks, bias_p, 0, col_k, col_v)
    oh_p, st_p = _hgrn(proj_p3, lb, hg_norm_w[0], None, hg_cols, HG_CHUNK, HG_CHUNK, bsz)
    y_p = _ffn(x_prompt.reshape(bsz * seq, d), attn_p.reshape(bsz * seq, aw),
               oh_p.reshape(bsz * seq, hw), w, TOKEN_BLOCK)
    k_off = aw + 4 * hw
    wp = min(WINDOW, seq)
    k_win_p = proj_p3[:, seq - wp:, k_off:k_off + kw].reshape(1, bsz, wp, ATTN_KV_HEADS, HEAD_DIM)
    v_win_p = proj_p3[:, seq - wp:, k_off + kw:k_off + 2 * kw].reshape(1, bsz, wp, ATTN_KV_HEADS, HEAD_DIM)

    tp = SAMPLE_T_PAD
    xs_pad = jnp.pad(x_sample, ((0, 0), (0, tp - dseq), (0, 0)))
    proj_s3 = _in_proj(xs_pad.reshape(dbsz * tp, d), norm_mix_w[0], w_in_r, TOKEN_BLOCK).reshape(dbsz, tp, -1)
    k_new = proj_s3[:, :dseq, k_off:k_off + kw]
    v_new = proj_s3[:, :dseq, k_off + kw:k_off + 2 * kw]
    k_cache = cache_k_win[0].reshape(dbsz, wb, kw)
    v_cache = cache_v_win[0].reshape(dbsz, wb, kw)
    kk = jnp.concatenate([k_cache, k_new], axis=1)
    vv = jnp.concatenate([v_cache, v_new], axis=1)
    q_pos = jnp.arange(tp)[:, None]
    k_pos = jnp.arange(wb + tp)[None, :]
    dist_s = wb + q_pos - k_pos
    mask_s = (dist_s >= 0) & (dist_s <= WINDOW) & (q_pos < dseq) & (k_pos < wb + dseq)
    bias_s = _masked_bias(rel_bias_table, tp, wb + tp, wb, mask_s)
    attn_s = _swa_sample(proj_s3, k_cache, v_cache, sinks, bias_s[:, :, :wb], bias_s[:, :, wb:],
                         0, col_k, col_v, 16)
    oh_s, st_s = _hgrn(proj_s3, lb, hg_norm_w[0], state_hgrn[0], hg_cols, tp, dseq, 8)
    y_s = _ffn(x_sample.reshape(dbsz * dseq, d), attn_s[:, :dseq].reshape(dbsz * dseq, aw),
               oh_s[:, :dseq].reshape(dbsz * dseq, hw), w, TOKEN_BLOCK)
    k_win_s = kk[:, dseq:].reshape(1, dbsz, wb, ATTN_KV_HEADS, HEAD_DIM)
    v_win_s = vv[:, dseq:].reshape(1, dbsz, wb, ATTN_KV_HEADS, HEAD_DIM)

    return (y_p.reshape(bsz, seq, d), y_s.reshape(dbsz, dseq, d), k_win_p, v_win_p, st_p[None],
            k_win_s, v_win_s, st_s[None])
```

```python
import functools
import math

import jax
import jax.numpy as jnp
from jax import lax
from jax.experimental import pallas as pl
from jax.experimental.pallas import tpu as pltpu

F32 = jnp.float32
BF16 = jnp.bfloat16

EPS = 1e-6
NEG = -1e30

ATTN_HEADS = 8
ATTN_KV_HEADS = 2
HEAD_DIM = 64
WINDOW = 128
REL_BUCKETS = 32
HG_HEADS = 4
HG_CHUNK = 64
PEER_HEADS = 8
PEER_TOPK = 16

LANES = 128
SUBLANES = 8
VMEM_LIMIT = 56 * 1024 * 1024


def _cparams(*sem):
    return pltpu.CompilerParams(dimension_semantics=sem, vmem_limit_bytes=VMEM_LIMIT)


def _nt(a, b):
    return lax.dot_general(a, b, (((1,), (1,)), ((), ())), preferred_element_type=F32)


def _tn(a, b):
    return lax.dot_general(a, b, (((0,), (0,)), ((), ())), preferred_element_type=F32)


def _dot(a, b):
    return jnp.dot(a, b, preferred_element_type=F32)


def _sigmoid(x):
    return 1.0 / (1.0 + jnp.exp(-x))


def _in_proj_kernel(x_ref, nw_ref, w_ref, o_ref):
    x = x_ref[...]
    xn = x * lax.rsqrt(jnp.mean(x * x, axis=-1, keepdims=True) + EPS) * nw_ref[...]
    o_ref[...] = _dot(xn.astype(BF16), w_ref[...])


def _in_proj(x2d, norm_w, w_bf16, tm):
    t, d = x2d.shape
    n = w_bf16.shape[1]
    return pl.pallas_call(
        _in_proj_kernel,
        grid=(t // tm,),
        in_specs=[pl.BlockSpec((tm, d), lambda i: (i, 0)),
                  pl.BlockSpec((1, d), lambda i: (0, 0)),
                  pl.BlockSpec((d, n), lambda i: (0, 0))],
        out_specs=pl.BlockSpec((tm, n), lambda i: (i, 0)),
        out_shape=jax.ShapeDtypeStruct((t, n), F32),
        compiler_params=_cparams("parallel"),
        name="in_proj",
    )(x2d, norm_w.reshape(1, d), w_bf16)


def _t5_bucket(dist):
    n = jnp.maximum(dist, 0)
    max_exact = REL_BUCKETS // 2
    nf = jnp.maximum(n, 1).astype(F32)
    large = max_exact + (jnp.log(nf / max_exact) / math.log(WINDOW / max_exact)
                         * (REL_BUCKETS - max_exact)).astype(jnp.int32)
    large = jnp.minimum(large, REL_BUCKETS - 1)
    return jnp.where(n < max_exact, n, large)


def _masked_bias(table, n_q, n_k, offset, mask):
    h = table.shape[1]
    diag = jnp.arange(n_q + n_k - 1) - (n_k - 1) + offset
    per_diag = table.astype(F32)[_t5_bucket(diag)].T
    w = jnp.pad(per_diag[:, ::-1], ((0, 0), (0, 1)))
    p = n_q + n_k
    skew = jnp.tile(w, (1, n_q))[:, :n_q * (p - 1)].reshape(h, n_q, p - 1)
    return jnp.where(mask[None], skew[:, :, n_q - 1:n_q - 1 + n_k], NEG)


def _swa_prompt_kernel(sink_ref, q_ref, kp_ref, kc_ref, vp_ref, vc_ref, bias0_ref, bias1_ref, o_ref):
    scale = HEAD_DIM ** -0.5
    assert math.frexp(scale)[0] == 0.5
    group = ATTN_HEADS // ATTN_KV_HEADS
    blk = kp_ref.shape[1]
    assert 2 * HEAD_DIM == LANES and kp_ref.shape[2] == LANES and ATTN_KV_HEADS == 2 and group % 2 == 0
    n_sub = q_ref.shape[1] // blk
    kk = [jnp.concatenate([kp_ref[0], kc_ref[0, :blk]], axis=0)]
    vv = [jnp.concatenate([vp_ref[0], vc_ref[0, :blk]], axis=0)]
    kk += [kc_ref[0, (j - 1) * blk:(j + 1) * blk] for j in range(1, n_sub)]
    vv += [vc_ref[0, (j - 1) * blk:(j + 1) * blk] for j in range(1, n_sub)]
    bias = [bias0_ref] + [bias1_ref] * (n_sub - 1)
    low = lax.broadcasted_iota(jnp.int32, kk[0].shape, 1) < HEAD_DIM

    def halves(x, kvh):
        own = jnp.where(low if kvh == 0 else ~low, x, 0.0)
        other = pltpu.roll(own, HEAD_DIM, axis=1)
        lo, hi = (own, other) if kvh == 0 else (other, own)
        return lo.astype(BF16), hi.astype(BF16)

    k_half = [[halves(kk[j], kvh) for kvh in range(ATTN_KV_HEADS)] for j in range(n_sub)]
    v_half = [[halves(vv[j], kvh) for kvh in range(ATTN_KV_HEADS)] for j in range(n_sub)]
    units = [(j, h) for j in range(n_sub) for h in range(ATTN_HEADS)]
    kv_of = lambda h: h // group
    qt = [[(q_ref[0, j * blk:(j + 1) * blk, t * LANES:(t + 1) * LANES] * scale).astype(BF16)
           for t in range(ATTN_HEADS // 2)] for j in range(n_sub)]
    s = {(j, h): _nt(qt[j][h // 2], k_half[j][kv_of(h)][h % 2]) + bias[j][0, h] for j, h in units}
    m = {u: jnp.maximum(jnp.max(s[u], axis=-1, keepdims=True), sink_ref[u[1]]) for u in units}
    p = {u: jnp.exp(s[u] - m[u]) for u in units}
    den = {u: jnp.sum(p[u], axis=-1, keepdims=True) + jnp.exp(sink_ref[u[1]] - m[u]) for u in units}
    p = {u: (p[u] * (1.0 / den[u])).astype(BF16) for u in units}
    for j in range(n_sub):
        for tile in range(ATTN_HEADS // 2):
            kvh = kv_of(2 * tile)
            o = _dot(p[j, 2 * tile], v_half[j][kvh][0]) + _dot(p[j, 2 * tile + 1], v_half[j][kvh][1])
            o_ref[0, j * blk:(j + 1) * blk, tile * LANES:(tile + 1) * LANES] = o.astype(o_ref.dtype)


def _swa_prompt(proj3, sinks, bias, col_q, col_k, col_v):
    bsz, seq, _ = proj3.shape
    blk = WINDOW
    aw = ATTN_HEADS * HEAD_DIM
    kw = ATTN_KV_HEADS * HEAD_DIM
    per = SWA_BLOCKS_PER_STEP
    prev = lambda b, n: (b, jnp.maximum(per * n - 1, 0))
    bias_spec = lambda pick: pl.BlockSpec((1, ATTN_HEADS, blk, 2 * blk), lambda b, n: (pick(n), 0, 0, 0))
    return pl.pallas_call(
        _swa_prompt_kernel,
        grid=(bsz, seq // (per * blk)),
        in_specs=[pl.BlockSpec(memory_space=pltpu.SMEM),
                  pl.BlockSpec((1, per * blk, aw), lambda b, n: (b, n, col_q)),
                  pl.BlockSpec((1, blk, kw), lambda b, n: prev(b, n) + (col_k,)),
                  pl.BlockSpec((1, per * blk, kw), lambda b, n: (b, n, col_k)),
                  pl.BlockSpec((1, blk, kw), lambda b, n: prev(b, n) + (col_v,)),
                  pl.BlockSpec((1, per * blk, kw), lambda b, n: (b, n, col_v)),
                  bias_spec(lambda n: jnp.minimum(n, 1)),
                  bias_spec(lambda n: 1)],
        out_specs=pl.BlockSpec((1, per * blk, aw), lambda b, n: (b, n, 0)),
        out_shape=jax.ShapeDtypeStruct((bsz, seq, aw), BF16),
        compiler_params=_cparams("parallel", "arbitrary"),
        name="swa_prompt",
    )(sinks, proj3, proj3, proj3, proj3, proj3, bias, bias)


def _swa_sample_kernel(sink_ref, q_ref, kn_ref, vn_ref, kc_ref, vc_ref, bias_c_ref, bias_n_ref, o_ref):
    scale = HEAD_DIM ** -0.5
    group = ATTN_HEADS // ATTN_KV_HEADS
    for h in range(ATTN_HEADS):
        kv = slice((h // group) * HEAD_DIM, (h // group + 1) * HEAD_DIM)
        qh = q_ref[:, :, h * HEAD_DIM:(h + 1) * HEAD_DIM].astype(BF16)
        sc = jnp.einsum('bqd,bkd->bqk', qh, kc_ref[:, :, kv].astype(BF16),
                        preferred_element_type=F32) * scale + bias_c_ref[h][None]
        sn = jnp.einsum('bqd,bkd->bqk', qh, kn_ref[:, :, kv].astype(BF16),
                        preferred_element_type=F32) * scale + bias_n_ref[h][None]
        sink = sink_ref[h]
        m = jnp.maximum(jnp.maximum(jnp.max(sc, axis=-1, keepdims=True),
                                    jnp.max(sn, axis=-1, keepdims=True)), sink)
        pc = jnp.exp(sc - m)
        pn = jnp.exp(sn - m)
        den = (jnp.sum(pc, axis=-1, keepdims=True) + jnp.sum(pn, axis=-1, keepdims=True)
               + jnp.exp(sink - m))
        o = (jnp.einsum('bqk,bkd->bqd', pc.astype(BF16), vc_ref[:, :, kv].astype(BF16), preferred_element_type=F32)
             + jnp.einsum('bqk,bkd->bqd', pn.astype(BF16), vn_ref[:, :, kv].astype(BF16),
                          preferred_element_type=F32)) / den
        o_ref[:, :, h * HEAD_DIM:(h + 1) * HEAD_DIM] = o.astype(o_ref.dtype)


def _swa_sample(proj3, k_cache, v_cache, sinks, bias_c, bias_n, col_q, col_k, col_v, bb):
    bsz, tp, _ = proj3.shape
    wb = k_cache.shape[1]
    aw = ATTN_HEADS * HEAD_DIM
    kw = ATTN_KV_HEADS * HEAD_DIM
    return pl.pallas_call(
        _swa_sample_kernel,
        grid=(bsz // bb,),
        in_specs=[pl.BlockSpec(memory_space=pltpu.SMEM),
                  pl.BlockSpec((bb, tp, aw), lambda b: (b, 0, col_q)),
                  pl.BlockSpec((bb, tp, kw), lambda b: (b, 0, col_k)),
                  pl.BlockSpec((bb, tp, kw), lambda b: (b, 0, col_v)),
                  pl.BlockSpec((bb, wb, kw), lambda b: (b, 0, 0)),
                  pl.BlockSpec((bb, wb, kw), lambda b: (b, 0, 0)),
                  pl.BlockSpec((ATTN_HEADS, tp, wb), lambda b: (0, 0, 0)),
                  pl.BlockSpec((ATTN_HEADS, tp, tp), lambda b: (0, 0, 0))],
        out_specs=pl.BlockSpec((bb, tp, aw), lambda b: (b, 0, 0)),
        out_shape=jax.ShapeDtypeStruct((bsz, tp, aw), BF16),
        compiler_params=_cparams("parallel"),
        name="swa_sample",
    )(sinks, proj3, proj3, proj3, k_cache, v_cache, bias_c, bias_n)


def _split3(x):
    hi = x.astype(BF16)
    r = x - hi.astype(F32)
    mid = r.astype(BF16)
    lo = (r - mid.astype(F32)).astype(BF16)
    return hi, mid, lo


def _hgrn_kernel(*refs, t_valid, has_state):
    if has_state:
        q_ref, f_ref, i_ref, g_ref, lb_ref, nw_ref, s0_ref, o_ref, s_ref, st_scr = refs
    else:
        q_ref, f_ref, i_ref, g_ref, lb_ref, nw_ref, o_ref, s_ref, st_scr = refs
    bb, chunk, width = q_ref.shape
    dk = width // HG_HEADS
    c = pl.program_id(1)

    @pl.when(c == 0)
    def _init():
        if has_state:
            st_scr[...] = s0_ref[...]
        else:
            st_scr[...] = jnp.zeros_like(st_scr)

    row = lax.broadcasted_iota(jnp.int32, (chunk, chunk), 0)
    col = lax.broadcasted_iota(jnp.int32, (chunk, chunk), 1)
    causal = row >= col
    tri = jnp.where(causal, 1.0, 0.0).astype(BF16)
    row_w = lax.broadcasted_iota(jnp.int32, (chunk, width), 0)
    valid = row_w < t_valid
    ones_rows = jnp.where(lax.broadcasted_iota(jnp.int32, (chunk, dk), 0) < 3, 1.0, 0.0).astype(BF16)
    mid_row = chunk // 2

    def body(b, carry):
        qx = q_ref[b]
        q = qx * _sigmoid(qx)
        lb = lb_ref[...]
        f = lb + (1.0 - lb) * _sigmoid(f_ref[b])
        k = 1.0 - f
        lg = jnp.log(f)
        if t_valid < chunk:
            k = jnp.where(valid, k, 0.0)
            lg = jnp.where(valid, lg, 0.0)
        v = i_ref[b].astype(BF16)
        cum = sum(_dot(tri, part) for part in _split3(lg))
        cum_mid = cum[mid_row:mid_row + 1, :]
        cum_last = cum[chunk - 1:chunk, :]
        qt = (q * jnp.exp(cum - cum_mid)).astype(BF16)
        kt = (k * jnp.exp(cum_mid - cum)).astype(BF16)
        qe = (q * jnp.exp(cum)).astype(BF16)
        kd = (k * jnp.exp(cum_last - cum)).astype(BF16)
        dec_rows = jnp.zeros((chunk, width), F32)
        for j, part in enumerate(_split3(jnp.exp(cum_last))):
            dec_rows = jnp.where(row_w == j, part.astype(F32), dec_rows)
        dec_rows = dec_rows.astype(BF16)
        heads = range(HG_HEADS)
        sl = [slice(h * dk, (h + 1) * dk) for h in heads]
        st = [st_scr[b, h] for h in heads]
        a = [_nt(qt[:, sl[h]], kt[:, sl[h]]) for h in heads]
        inter = [_dot(qe[:, sl[h]], st[h].astype(BF16)) for h in heads]
        upd = [_tn(kd[:, sl[h]], v[:, sl[h]]) for h in heads]
        decay = [_tn(dec_rows[:, sl[h]], ones_rows) for h in heads]
        for h in heads:
            st_scr[b, h] = st[h] * decay[h] + upd[h]
        a = [jnp.where(causal, a[h], 0.0).astype(BF16) for h in heads]
        o = [_dot(a[h], v[:, sl[h]]) + inter[h] for h in heads]
        outs = [o[h] * lax.rsqrt(jnp.mean(o[h] * o[h], axis=-1, keepdims=True) + EPS) for h in heads]
        gx = g_ref[b]
        o = jnp.concatenate(outs, axis=1) * nw_ref[...] * (gx * _sigmoid(gx))
        o_ref[b] = o.astype(o_ref.dtype)
        return carry

    lax.fori_loop(0, bb, body, 0, unroll=8)

    @pl.when(c == pl.num_programs(1) - 1)
    def _final():
        s_ref[...] = st_scr[...]


def _hgrn(proj3, lb, norm_w, s0, cols, chunk, t_valid, bb):
    bsz, t, _ = proj3.shape
    width = lb.shape[0]
    dk = width // HG_HEADS
    has_state = s0 is not None
    spec = lambda cb: pl.BlockSpec((bb, chunk, width), lambda b, c: (b, c, cb))
    vec = pl.BlockSpec((1, width), lambda b, c: (0, 0))
    st_spec = pl.BlockSpec((bb, HG_HEADS, dk, dk), lambda b, c: (b, 0, 0, 0))
    in_specs = [spec(cols[0]), spec(cols[1]), spec(cols[2]), spec(cols[3]), vec, vec]
    args = [proj3, proj3, proj3, proj3, lb.reshape(1, width), norm_w.reshape(1, width)]
    if has_state:
        in_specs.append(st_spec)
        args.append(s0)
    return pl.pallas_call(
        functools.partial(_hgrn_kernel, t_valid=t_valid, has_state=has_state),
        grid=(bsz // bb, t // chunk),
        in_specs=in_specs,
        out_specs=[pl.BlockSpec((bb, chunk, width), lambda b, c: (b, c, 0)), st_spec],
        out_shape=[jax.ShapeDtypeStruct((bsz, t, width), BF16),
                   jax.ShapeDtypeStruct((bsz, HG_HEADS, dk, dk), F32)],
        scratch_shapes=[pltpu.VMEM((bb, HG_HEADS, dk, dk), F32)],
        compiler_params=_cparams("parallel", "arbitrary"),
        name="hgrn_state" if has_state else "hgrn_prompt",
    )(*args)


def _out_proj_kernel(x_ref, a_ref, oh_ref, wo_ref, nw_ref, h_ref, hnt_ref):
    aw = a_ref.shape[1]
    mix = _dot(a_ref[...], wo_ref[:aw, :]) + _dot(oh_ref[...], wo_ref[aw:, :])
    h = x_ref[...] + mix
    h_ref[...] = h
    hn = h * lax.rsqrt(jnp.mean(h * h, axis=-1, keepdims=True) + EPS) * nw_ref[...]
    hnt_ref[...] = hn.T.astype(hnt_ref.dtype)


def _out_proj(x2d, attn, oh, wo_bf16, norm_w, tm):
    t, d = x2d.shape
    aw, hw = attn.shape[1], oh.shape[1]
    return pl.pallas_call(
        _out_proj_kernel,
        grid=(t // tm,),
        in_specs=[pl.BlockSpec((tm, d), lambda i: (i, 0)),
                  pl.BlockSpec((tm, aw), lambda i: (i, 0)),
                  pl.BlockSpec((tm, hw), lambda i: (i, 0)),
                  pl.BlockSpec((aw + hw, d), lambda i: (0, 0)),
                  pl.BlockSpec((1, d), lambda i: (0, 0))],
        out_specs=[pl.BlockSpec((tm, d), lambda i: (i, 0)),
                   pl.BlockSpec((d, tm), lambda i: (0, i))],
        out_shape=[jax.ShapeDtypeStruct((t, d), F32),
                   jax.ShapeDtypeStruct((d, t), BF16)],
        compiler_params=_cparams("parallel"),
        name="out_proj",
    )(x2d, attn, oh, wo_bf16, norm_w.reshape(1, d))


def _peer_scores_kernel(hnt_ref, wqt_ref, keys_ref, sc_ref):
    qt = _dot(wqt_ref[...], hnt_ref[...])
    half = keys_ref.shape[2]
    for hc in range(keys_ref.shape[0]):
        s = _dot(keys_ref[hc], qt[hc * half:(hc + 1) * half, :].astype(BF16))
        for lt in range(sc_ref.shape[0]):
            sc_ref[lt, hc] = s[:, lt * LANES:(lt + 1) * LANES]


def _peer_scores(hnt, wqt_bf16, keys_bf16, tb):
    d, t = hnt.shape
    nhc, nk, half = keys_bf16.shape
    return pl.pallas_call(
        _peer_scores_kernel,
        grid=(t // tb,),
        in_specs=[pl.BlockSpec((d, tb), lambda i: (0, i)),
                  pl.BlockSpec((nhc * half, d), lambda i: (0, 0)),
                  pl.BlockSpec((nhc, nk, half), lambda i: (0, 0, 0))],
        out_specs=pl.BlockSpec((tb // LANES, nhc, nk, LANES), lambda i: (i, 0, 0, 0)),
        out_shape=jax.ShapeDtypeStruct((t // LANES, nhc, nk, LANES), F32),
        compiler_params=_cparams("parallel"),
        name="peer_scores",
    )(hnt, wqt_bf16, keys_bf16)


def _sort16_pairs():
    def merge(lo, hi, r):
        step = r * 2
        if step < hi - lo:
            yield from merge(lo, hi, step)
            yield from merge(lo + r, hi, step)
            yield from [(i, i + r) for i in range(lo + r, hi - r, step)]
        else:
            yield (lo, lo + r)

    def sort(lo, hi):
        if hi - lo >= 1:
            mid = lo + (hi - lo) // 2
            yield from sort(lo, mid)
            yield from sort(mid + 1, hi)
            yield from merge(lo, hi, 1)

    return tuple(sort(0, PEER_TOPK - 1))


_SORT16 = _sort16_pairs()


def _bitonic_to_sorted(z):
    z = list(z)
    d = PEER_TOPK // 2
    while d >= 1:
        for i in range(PEER_TOPK):
            if i & d == 0:
                hi, lo = jnp.maximum(z[i], z[i + d]), jnp.minimum(z[i], z[i + d])
                z[i], z[i + d] = hi, lo
        d //= 2
    return z


def _merge_bitonic(top, other):
    z = list(top)
    m = len(other)
    for r in range(PEER_TOPK - m, PEER_TOPK):
        z[r] = jnp.maximum(top[r], other[PEER_TOPK - 1 - r])
    return z


def _top16_desc(x):
    n = x.shape[0] // SUBLANES
    xs = [x[g * SUBLANES:(g + 1) * SUBLANES, :] for g in range(n)]
    for i, j in _SORT16:
        xs[i], xs[j] = jnp.maximum(xs[i], xs[j]), jnp.minimum(xs[i], xs[j])
    shift = SUBLANES // 2
    while shift >= 1:
        ys = [pltpu.roll(v, shift, axis=0) for v in xs]
        xs = _bitonic_to_sorted(_merge_bitonic(xs, ys))
        shift //= 2
    return xs


def _peer_select(sc_ref, thr_scr, pw_scr, q_scr, lt):
    assert PEER_HEADS == SUBLANES
    sub = lax.broadcasted_iota(jnp.int32, (SUBLANES, LANES), 0)
    a = b = None
    for h in range(PEER_HEADS):
        a_h = _top16_desc(sc_ref[lt, 2 * h])
        b_h = _top16_desc(sc_ref[lt, 2 * h + 1])
        a = a_h if h == 0 else [jnp.where(sub == h, new, old) for new, old in zip(a_h, a)]
        b = b_h if h == 0 else [jnp.where(sub == h, new, old) for new, old in zip(b_h, b)]
    lists = [[a[r] + b[c] for c in range(PEER_TOPK // (r + 1))] for r in range(SUBLANES)]
    lists.append([a[r] + b[0] for r in range(SUBLANES, PEER_TOPK)])
    top = lists[0]
    for other in lists[1:-1]:
        top = _bitonic_to_sorted(_merge_bitonic(top, other))
    z = _merge_bitonic(top, lists[-1])
    tau = functools.reduce(jnp.minimum, z)
    best = a[0] + b[0]
    zsum = jnp.zeros_like(tau)
    inf = jnp.full_like(tau, jnp.inf)
    thr_rank = []
    for r, cand in enumerate(lists):
        hits = [v >= tau for v in cand]
        for v, hit in zip(cand, hits):
            zsum = zsum + jnp.where(hit, jnp.exp(v - best), 0.0)
        if r < SUBLANES:
            t = inf
            for c, hit in enumerate(hits):
                t = jnp.where(hit, b[c], t)
            thr_rank.append(t)
        else:
            thr_rank.extend(jnp.where(hit, b[0], inf) for hit in hits)
    inv = 0.5 / zsum
    for h in range(PEER_HEADS):
        own = lambda v: jnp.broadcast_to(v[h:h + 1, :], (SUBLANES, LANES))
        a_h = [own(v) for v in a]
        thr_h = [own(v) for v in thr_rank]
        inv_h, b0_h = own(inv), own(b[0])
        s0 = sc_ref[lt, 2 * h]
        s1 = sc_ref[lt, 2 * h + 1]
        for g in range(s0.shape[0] // SUBLANES):
            rows = slice(g * SUBLANES, (g + 1) * SUBLANES)
            x0 = s0[rows, :]
            thr = jnp.full_like(x0, jnp.inf)
            for r in range(PEER_TOPK):
                thr = jnp.where(x0 == a_h[r], thr_h[r], thr)
            thr_scr[lt, h, rows, :] = thr
            pw_scr[lt, h, rows, :] = jnp.exp(x0 - a_h[0]) * inv_h
            q_scr[lt, h, rows, :] = jnp.exp(s1[rows, :] - b0_h)


MXU_TILE = 256
MXU_COUNT = 2
ACC_ROWS = 512
ACC_PIECE = 32
ACC_PRE = 0
ACC_OUT = ACC_ROWS // 4


def _peer_dense_kernel(sc_ref, hnt_ref, u_ref, vt_ref, res_ref, nw_ref, y_ref, thr_scr, pw_scr, q_scr,
                       h0_scr, h1_scr, g0_scr, g1_scr, yt_ref):
    s = pl.program_id(1)
    n_e = pl.num_programs(1) - 2
    d, tb = hnt_ref.shape
    n_lt, _, nk, _ = sc_ref.shape
    eb = u_ref.shape[0]
    n_i = eb // nk
    assert tb == MXU_COUNT * MXU_TILE and eb % ACC_ROWS == 0 and d % ACC_ROWS == 0

    @pl.when(s == 0)
    def _select():
        def body(lt, carry):
            _peer_select(sc_ref, thr_scr, pw_scr, q_scr, lt)
            return carry
        lax.fori_loop(0, n_lt, body, 0)
        yt_ref[...] = jnp.zeros_like(yt_ref)
        g0_scr[...] = jnp.zeros_like(g0_scr)
        g1_scr[...] = jnp.zeros_like(g1_scr)

    i0 = pl.multiple_of(jnp.clip(s - 1, 0, n_e - 1) * n_i, SUBLANES)

    def gate_steps(lt, ii, h_r, g_w):
        lanes = slice(lt * LANES, (lt + 1) * LANES)
        rows = slice(ii * nk, (ii + 1) * nk)
        tiled = (nk // SUBLANES, SUBLANES, LANES)
        state = {"w": jnp.zeros(tiled, F32)}

        def head(h):
            thr = thr_scr[lt, h, pl.ds(i0 + ii, SUBLANES, stride=0), :]
            pw = pw_scr[lt, h, pl.ds(i0 + ii, SUBLANES, stride=0), :]
            s1 = sc_ref[lt, 2 * h + 1].reshape(tiled)
            state["w"] = state["w"] + jnp.where(s1 >= thr[None], q_scr[lt, h].reshape(tiled), 0.0) * pw[None]
            if h == PEER_HEADS - 1:
                x = h_r[rows, lanes]
                act = x + x * lax.erf(x * (2.0 ** -0.5))
                g_w[rows, lanes] = (state["w"].reshape(nk, LANES) * act).astype(g_w.dtype)

        return [functools.partial(head, h) for h in range(PEER_HEADS)]

    def mxu_steps(kind, c, k, reg, h_w, g_r):
        lhs_ref, rhs_ref, acc, n_k = ((u_ref, hnt_ref, ACC_PRE, d // MXU_TILE) if kind == "pre"
                                      else (vt_ref, g_r, ACC_OUT, eb // MXU_TILE))
        kc = slice(k * MXU_TILE, (k + 1) * MXU_TILE)

        def push():
            for q in range(MXU_COUNT):
                pltpu.matmul_push_rhs(rhs_ref[kc, q * MXU_TILE:(q + 1) * MXU_TILE], staging_register=reg,
                                      mxu_index=q)

        def piece(p):
            r0 = c * ACC_ROWS + p * ACC_PIECE
            lhs = lhs_ref[r0:r0 + ACC_PIECE, kc]
            for q in range(MXU_COUNT):
                pltpu.matmul_acc_lhs(acc + p * ACC_PIECE // 4, lhs, q, load_staged_rhs=reg if p == 0 else None)

        def pop(p):
            rows = slice(c * ACC_ROWS + p * ACC_PIECE, c * ACC_ROWS + (p + 1) * ACC_PIECE)
            for q in range(MXU_COUNT):
                cols = slice(q * MXU_TILE, (q + 1) * MXU_TILE)
                res = pltpu.matmul_pop(acc + p * ACC_PIECE // 4, (ACC_PIECE, MXU_TILE), F32, q)
                if kind == "pre":
                    h_w[rows, cols] = res
                else:
                    yt_ref[rows, cols] += res

        n_p = ACC_ROWS // ACC_PIECE
        pops = [functools.partial(pop, p) for p in range(n_p)] if k == n_k - 1 else []
        return push, [functools.partial(piece, p) for p in range(n_p)], pops

    def stage(h_w, h_r, g_w, g_r, pre, gating, out):
        pre_groups = [("pre", c, k) for c in range(eb // ACC_ROWS) for k in range(d // MXU_TILE)] if pre else []
        out_groups = [("out", c, k) for c in range(d // ACC_ROWS) for k in range(eb // MXU_TILE)] if out else []
        if pre and out:
            order = [g for pair in zip(pre_groups, out_groups) for g in pair]
        else:
            order = pre_groups + out_groups
        steps = [mxu_steps(kind, c, k, gi % 2, h_w, g_r) for gi, (kind, c, k) in enumerate(order)]
        mxu = [steps[0][0]]
        lagged = []
        for gi, (_, pieces, pops) in enumerate(steps):
            half = len(pieces) // 2
            for p, piece in enumerate(pieces):
                if p == half and gi + 1 < len(steps):
                    mxu.append(steps[gi + 1][0])
                mxu.append(piece)
                if lagged:
                    mxu.append(lagged.pop(0))
            same_acc_next = gi + 1 < len(steps) and order[gi + 1][0] == order[gi][0]
            if same_acc_next or gi + 1 == len(steps):
                mxu.extend(pops)
            else:
                lagged = list(pops)
        vpu = [t for lt in range(n_lt) for ii in range(n_i) for t in gate_steps(lt, ii, h_r, g_w)] if gating else []
        im = iv = 0
        while im < len(mxu) or iv < len(vpu):
            if iv >= len(vpu) or (im < len(mxu) and im * len(vpu) <= iv * len(mxu)):
                mxu[im]()
                im += 1
            else:
                vpu[iv]()
                iv += 1

    last = n_e + 1

    @pl.when(s == 0)
    def _first():
        stage(h0_scr, None, None, None, True, False, False)

    @pl.when((s > 0) & (s < last) & (s % 2 == 0))
    def _even():
        stage(h0_scr, h1_scr, g1_scr, g0_scr, True, True, True)

    @pl.when((s > 0) & (s < last) & (s % 2 == 1))
    def _odd():
        stage(h1_scr, h0_scr, g0_scr, g1_scr, True, True, True)

    def finish(g_r):
        stage(None, None, None, g_r, False, False, True)
        y = res_ref[...] + yt_ref[...].T
        y_ref[...] = y * lax.rsqrt(jnp.mean(y * y, axis=-1, keepdims=True) + EPS) * nw_ref[...]

    @pl.when((s == last) & (s % 2 == 0))
    def _last_even():
        finish(g0_scr)

    @pl.when((s == last) & (s % 2 == 1))
    def _last_odd():
        finish(g1_scr)


def _peer_dense(sc, hnt, u_bf16, vt_bf16, resid, norm_w, tb, eb):
    _, nhc, nk, _ = sc.shape
    d, t = hnt.shape
    n_exp = u_bf16.shape[0]
    n_lt = tb // LANES
    assert eb == SUBLANES * nk and n_exp == nk * nk and t % tb == 0 and tb % LANES == 0
    sel = pltpu.VMEM((n_lt, PEER_HEADS, nk, LANES), F32)
    pre = pltpu.VMEM((eb, tb + LANES), F32)
    gated = pltpu.VMEM((eb, tb), BF16)
    n_e = n_exp // eb
    return pl.pallas_call(
        _peer_dense_kernel,
        grid=(t // tb, n_e + 2),
        in_specs=[pl.BlockSpec((n_lt, nhc, nk, LANES), lambda i, s: (i, 0, 0, 0)),
                  pl.BlockSpec((d, tb), lambda i, s: (0, i)),
                  pl.BlockSpec((eb, d), lambda i, s: (jnp.minimum(s, n_e - 1), 0)),
                  pl.BlockSpec((d, eb), lambda i, s: (0, jnp.clip(s - 2, 0, n_e - 1))),
                  pl.BlockSpec((tb, d), lambda i, s: (i, 0)),
                  pl.BlockSpec((1, d), lambda i, s: (0, 0))],
        out_specs=pl.BlockSpec((tb, d), lambda i, s: (i, 0)),
        out_shape=jax.ShapeDtypeStruct((t, d), F32),
        scratch_shapes=[sel, sel, sel, pre, pre, gated, gated, pltpu.VMEM((d, tb), F32)],
        compiler_params=_cparams("parallel", "arbitrary"),
        name="peer_dense",
    )(sc, hnt, u_bf16, vt_bf16, resid, norm_w.reshape(1, d))


TOKEN_BLOCK = 512
SWA_BLOCKS_PER_STEP = 4
EXPERT_BLOCK = 1024
SAMPLE_T_PAD = 16


def _ffn(x2d, attn, oh, w, tb):
    h, hnt = _out_proj(x2d, attn, oh, w['wo'], w['norm_ffn'], tb)
    sc = _peer_scores(hnt, w['wqt'], w['keys'], tb)
    return _peer_dense(sc, hnt, w['u'], w['vt'], h, w['norm_final'], tb, EXPERT_BLOCK)


def kernel(x_prompt, x_sample, cache_k_win, cache_v_win, state_hgrn, norm_mix_w, w_in, attn_sinks,
           rel_bias_table, hg_lb, hg_norm_w, w_o, norm_ffn_w, peer_w_q, peer_sub_keys, peer_u, peer_v,
           norm_final_w):
    bsz, seq, d = x_prompt.shape
    dbsz, dseq, _ = x_sample.shape
    aw = ATTN_HEADS * HEAD_DIM
    kw = ATTN_KV_HEADS * HEAD_DIM
    hw = hg_norm_w.shape[1]
    wb = cache_k_win.shape[2]

    wi = w_in[0]
    w_in_r = jnp.concatenate([wi[:, :aw], wi[:, aw + 2 * kw:], wi[:, aw:aw + 2 * kw]], axis=1).astype(BF16)
    col_k = (aw + 4 * hw) // kw
    col_v = col_k + 1
    hg_cols = (1, 2, 3, 4)
    lb = jax.nn.softmax(hg_lb.astype(F32), axis=0)[0]
    nhc = PEER_HEADS * 2
    w = {
        'wo': w_o[0].astype(BF16),
        'norm_ffn': norm_ffn_w[0],
        'wqt': peer_w_q[0].astype(BF16).T,
        'keys': peer_sub_keys[0].reshape(nhc, peer_sub_keys.shape[3], peer_sub_keys.shape[4]).astype(BF16),
        'u': peer_u[0].astype(BF16),
        'vt': peer_v[0].astype(BF16).T,
        'norm_final': norm_final_w,
    }
    sinks = attn_sinks[0].astype(F32)

    proj_p = _in_proj(x_prompt.reshape(bsz * seq, d), norm_mix_w[0], w_in_r, TOKEN_BLOCK)
    proj_p3 = proj_p.reshape(bsz, seq, -1)
    blk = WINDOW
    dist_p = (jnp.arange(blk)[:, None] + blk) - jnp.arange(2 * blk)[None, :]
    in_win = (dist_p >= 0) & (dist_p <= WINDOW)
    has_prev = jnp.arange(2 * blk)[None, :] >= blk
    bias_p = jnp.stack([_masked_bias(rel_bias_table, blk, 2 * blk, blk, in_win & has_prev),
                        _masked_bias(rel_bias_table, blk, 2 * blk, blk, in_win)])
    attn_p = _swa_prompt(proj_p3, sinks, bias_p, 0, col_k, col_v)
    oh_p, st_p = _hgrn(proj_p3, lb, hg_norm_w[0], None, hg_cols, HG_CHUNK, HG_CHUNK, bsz)
    y_p = _ffn(x_prompt.reshape(bsz * seq, d), attn_p.reshape(bsz * seq, aw),
               oh_p.reshape(bsz * seq, hw), w, TOKEN_BLOCK)
    k_off = aw + 4 * hw
    wp = min(WINDOW, seq)
    k_win_p = proj_p3[:, seq - wp:, k_off:k_off + kw].reshape(1, bsz, wp, ATTN_KV_HEADS, HEAD_DIM)
    v_win_p = proj_p3[:, seq - wp:, k_off + kw:k_off + 2 * kw].reshape(1, bsz, wp, ATTN_KV_HEADS, HEAD_DIM)

    tp = SAMPLE_T_PAD
    xs_pad = jnp.pad(x_sample, ((0, 0), (0, tp - dseq), (0, 0)))
    proj_s3 = _in_proj(xs_pad.reshape(dbsz * tp, d), norm_mix_w[0], w_in_r, TOKEN_BLOCK).reshape(dbsz, tp, -1)
    k_new = proj_s3[:, :dseq, k_off:k_off + kw]
    v_new = proj_s3[:, :dseq, k_off + kw:k_off + 2 * kw]
    k_cache = cache_k_win[0].reshape(dbsz, wb, kw)
    v_cache = cache_v_win[0].reshape(dbsz, wb, kw)
    kk = jnp.concatenate([k_cache, k_new], axis=1)
    vv = jnp.concatenate([v_cache, v_new], axis=1)
    q_pos = jnp.arange(tp)[:, None]
    k_pos = jnp.arange(wb + tp)[None, :]
    dist_s = wb + q_pos - k_pos
    mask_s = (dist_s >= 0) & (dist_s <= WINDOW) & (q_pos < dseq) & (k_pos < wb + dseq)
    bias_s = _masked_bias(rel_bias_table, tp, wb + tp, wb, mask_s)
    attn_s = _swa_sample(proj_s3, k_cache, v_cache, sinks, bias_s[:, :, :wb], bias_s[:, :, wb:],
                         0, col_k, col_v, 16)
    oh_s, st_s = _hgrn(proj_s3, lb, hg_norm_w[0], state_hgrn[0], hg_cols, tp, dseq, 8)
    y_s = _ffn(x_sample.reshape(dbsz * dseq, d), attn_s[:, :dseq].reshape(dbsz * dseq, aw),
               oh_s[:, :dseq].reshape(dbsz * dseq, hw), w, TOKEN_BLOCK)
    k_win_s = kk[:, dseq:].reshape(1, dbsz, wb, ATTN_KV_HEADS, HEAD_DIM)
    v_win_s = vv[:, dseq:].reshape(1, dbsz, wb, ATTN_KV_HEADS, HEAD_DIM)

    return (y_p.reshape(bsz, seq, d), y_s.reshape(dbsz, dseq, d), k_win_p, v_win_p, st_p[None],
            k_win_s, v_win_s, st_s[None])
```

```python
import functools
import math

import jax
import jax.numpy as jnp
from jax import lax
from jax.experimental import pallas as pl
from jax.experimental.pallas import tpu as pltpu

F32 = jnp.float32
BF16 = jnp.bfloat16

EPS = 1e-6
NEG = -1e30

ATTN_HEADS = 8
ATTN_KV_HEADS = 2
HEAD_DIM = 64
WINDOW = 128
REL_BUCKETS = 32
HG_HEADS = 4
HG_CHUNK = 64
PEER_HEADS = 8
PEER_TOPK = 16

LANES = 128
SUBLANES = 8
VMEM_LIMIT = 56 * 1024 * 1024


def _cparams(*sem):
    return pltpu.CompilerParams(dimension_semantics=sem, vmem_limit_bytes=VMEM_LIMIT)


def _nt(a, b):
    return lax.dot_general(a, b, (((1,), (1,)), ((), ())), preferred_element_type=F32)


def _tn(a, b):
    return lax.dot_general(a, b, (((0,), (0,)), ((), ())), preferred_element_type=F32)


def _dot(a, b):
    return jnp.dot(a, b, preferred_element_type=F32)


def _sigmoid(x):
    return 1.0 / (1.0 + jnp.exp(-x))


def _in_proj_kernel(x_ref, nw_ref, w_ref, o_ref):
    x = x_ref[...]
    xn = x * lax.rsqrt(jnp.mean(x * x, axis=-1, keepdims=True) + EPS) * nw_ref[...]
    o_ref[...] = _dot(xn.astype(BF16), w_ref[...])


def _in_proj(x2d, norm_w, w_bf16, tm):
    t, d = x2d.shape
    n = w_bf16.shape[1]
    return pl.pallas_call(
        _in_proj_kernel,
        grid=(t // tm,),
        in_specs=[pl.BlockSpec((tm, d), lambda i: (i, 0)),
                  pl.BlockSpec((1, d), lambda i: (0, 0)),
                  pl.BlockSpec((d, n), lambda i: (0, 0))],
        out_specs=pl.BlockSpec((tm, n), lambda i: (i, 0)),
        out_shape=jax.ShapeDtypeStruct((t, n), F32),
        compiler_params=_cparams("parallel"),
        name="in_proj",
    )(x2d, norm_w.reshape(1, d), w_bf16)


def _t5_bucket(dist):
    n = jnp.maximum(dist, 0)
    max_exact = REL_BUCKETS // 2
    nf = jnp.maximum(n, 1).astype(F32)
    large = max_exact + (jnp.log(nf / max_exact) / math.log(WINDOW / max_exact)
                         * (REL_BUCKETS - max_exact)).astype(jnp.int32)
    large = jnp.minimum(large, REL_BUCKETS - 1)
    return jnp.where(n < max_exact, n, large)


def _masked_bias(table, n_q, n_k, offset, mask):
    h = table.shape[1]
    diag = jnp.arange(n_q + n_k - 1) - (n_k - 1) + offset
    per_diag = table.astype(F32)[_t5_bucket(diag)].T
    w = jnp.pad(per_diag[:, ::-1], ((0, 0), (0, 1)))
    p = n_q + n_k
    skew = jnp.tile(w, (1, n_q))[:, :n_q * (p - 1)].reshape(h, n_q, p - 1)
    return jnp.where(mask[None], skew[:, :, n_q - 1:n_q - 1 + n_k], NEG)


def _swa_prompt_kernel(sink_ref, q_ref, kp_ref, kc_ref, vp_ref, vc_ref, bias0_ref, bias1_ref, o_ref):
    scale = HEAD_DIM ** -0.5
    assert math.frexp(scale)[0] == 0.5
    group = ATTN_HEADS // ATTN_KV_HEADS
    blk = kp_ref.shape[1]
    assert 2 * HEAD_DIM == LANES and kp_ref.shape[2] == LANES and ATTN_KV_HEADS == 2 and group % 2 == 0
    n_sub = q_ref.shape[1] // blk
    kk = [jnp.concatenate([kp_ref[0], kc_ref[0, :blk]], axis=0)]
    vv = [jnp.concatenate([vp_ref[0], vc_ref[0, :blk]], axis=0)]
    kk += [kc_ref[0, (j - 1) * blk:(j + 1) * blk] for j in range(1, n_sub)]
    vv += [vc_ref[0, (j - 1) * blk:(j + 1) * blk] for j in range(1, n_sub)]
    bias = [bias0_ref] + [bias1_ref] * (n_sub - 1)
    low = lax.broadcasted_iota(jnp.int32, kk[0].shape, 1) < HEAD_DIM

    def halves(x, kvh):
        own = jnp.where(low if kvh == 0 else ~low, x, 0.0)
        other = pltpu.roll(own, HEAD_DIM, axis=1)
        lo, hi = (own, other) if kvh == 0 else (other, own)
        return lo.astype(BF16), hi.astype(BF16)

    k_half = [[halves(kk[j], kvh) for kvh in range(ATTN_KV_HEADS)] for j in range(n_sub)]
    v_half = [[halves(vv[j], kvh) for kvh in range(ATTN_KV_HEADS)] for j in range(n_sub)]
    units = [(j, h) for j in range(n_sub) for h in range(ATTN_HEADS)]
    kv_of = lambda h: h // group
    qt = [[(q_ref[0, j * blk:(j + 1) * blk, t * LANES:(t + 1) * LANES] * scale).astype(BF16)
           for t in range(ATTN_HEADS // 2)] for j in range(n_sub)]
    s = {(j, h): _nt(qt[j][h // 2], k_half[j][kv_of(h)][h % 2]) + bias[j][0, h] for j, h in units}
    m = {u: jnp.maximum(jnp.max(s[u], axis=-1, keepdims=True), sink_ref[u[1]]) for u in units}
    p = {u: jnp.exp(s[u] - m[u]) for u in units}
    den = {u: jnp.sum(p[u], axis=-1, keepdims=True) + jnp.exp(sink_ref[u[1]] - m[u]) for u in units}
    p = {u: (p[u] * (1.0 / den[u])).astype(BF16) for u in units}
    for j in range(n_sub):
        for tile in range(ATTN_HEADS // 2):
            kvh = kv_of(2 * tile)
            o = _dot(p[j, 2 * tile], v_half[j][kvh][0]) + _dot(p[j, 2 * tile + 1], v_half[j][kvh][1])
            o_ref[0, j * blk:(j + 1) * blk, tile * LANES:(tile + 1) * LANES] = o.astype(o_ref.dtype)


def _swa_prompt(proj3, sinks, bias, col_q, col_k, col_v):
    bsz, seq, _ = proj3.shape
    blk = WINDOW
    aw = ATTN_HEADS * HEAD_DIM
    kw = ATTN_KV_HEADS * HEAD_DIM
    per = SWA_BLOCKS_PER_STEP
    prev = lambda b, n: (b, jnp.maximum(per * n - 1, 0))
    bias_spec = lambda pick: pl.BlockSpec((1, ATTN_HEADS, blk, 2 * blk), lambda b, n: (pick(n), 0, 0, 0))
    return pl.pallas_call(
        _swa_prompt_kernel,
        grid=(bsz, seq // (per * blk)),
        in_specs=[pl.BlockSpec(memory_space=pltpu.SMEM),
                  pl.BlockSpec((1, per * blk, aw), lambda b, n: (b, n, col_q)),
                  pl.BlockSpec((1, blk, kw), lambda b, n: prev(b, n) + (col_k,)),
                  pl.BlockSpec((1, per * blk, kw), lambda b, n: (b, n, col_k)),
                  pl.BlockSpec((1, blk, kw), lambda b, n: prev(b, n) + (col_v,)),
                  pl.BlockSpec((1, per * blk, kw), lambda b, n: (b, n, col_v)),
                  bias_spec(lambda n: jnp.minimum(n, 1)),
                  bias_spec(lambda n: 1)],
        out_specs=pl.BlockSpec((1, per * blk, aw), lambda b, n: (b, n, 0)),
        out_shape=jax.ShapeDtypeStruct((bsz, seq, aw), BF16),
        compiler_params=_cparams("parallel", "arbitrary"),
        name="swa_prompt",
    )(sinks, proj3, proj3, proj3, proj3, proj3, bias, bias)


def _swa_sample_kernel(sink_ref, q_ref, kn_ref, vn_ref, kc_ref, vc_ref, bias_c_ref, bias_n_ref, o_ref):
    scale = HEAD_DIM ** -0.5
    group = ATTN_HEADS // ATTN_KV_HEADS
    heads = range(ATTN_HEADS)
    kv = [slice((h // group) * HEAD_DIM, (h // group + 1) * HEAD_DIM) for h in heads]
    dot_qk = lambda a, b: jnp.einsum('bqd,bkd->bqk', a, b, preferred_element_type=F32)
    dot_pv = lambda a, b: jnp.einsum('bqk,bkd->bqd', a, b, preferred_element_type=F32)
    qh = [q_ref[:, :, h * HEAD_DIM:(h + 1) * HEAD_DIM].astype(BF16) for h in heads]
    sc = [dot_qk(qh[h], kc_ref[:, :, kv[h]].astype(BF16)) * scale + bias_c_ref[h][None] for h in heads]
    sn = [dot_qk(qh[h], kn_ref[:, :, kv[h]].astype(BF16)) * scale + bias_n_ref[h][None] for h in heads]
    m = [jnp.maximum(jnp.maximum(jnp.max(sc[h], axis=-1, keepdims=True),
                                 jnp.max(sn[h], axis=-1, keepdims=True)), sink_ref[h]) for h in heads]
    pc = [jnp.exp(sc[h] - m[h]) for h in heads]
    pn = [jnp.exp(sn[h] - m[h]) for h in heads]
    den = [jnp.sum(pc[h], axis=-1, keepdims=True) + jnp.sum(pn[h], axis=-1, keepdims=True)
           + jnp.exp(sink_ref[h] - m[h]) for h in heads]
    for h in heads:
        o = (dot_pv(pc[h].astype(BF16), vc_ref[:, :, kv[h]].astype(BF16))
             + dot_pv(pn[h].astype(BF16), vn_ref[:, :, kv[h]].astype(BF16))) / den[h]
        o_ref[:, :, h * HEAD_DIM:(h + 1) * HEAD_DIM] = o.astype(o_ref.dtype)


def _swa_sample(proj3, k_cache, v_cache, sinks, bias_c, bias_n, col_q, col_k, col_v, bb):
    bsz, tp, _ = proj3.shape
    wb = k_cache.shape[1]
    aw = ATTN_HEADS * HEAD_DIM
    kw = ATTN_KV_HEADS * HEAD_DIM
    return pl.pallas_call(
        _swa_sample_kernel,
        grid=(bsz // bb,),
        in_specs=[pl.BlockSpec(memory_space=pltpu.SMEM),
                  pl.BlockSpec((bb, tp, aw), lambda b: (b, 0, col_q)),
                  pl.BlockSpec((bb, tp, kw), lambda b: (b, 0, col_k)),
                  pl.BlockSpec((bb, tp, kw), lambda b: (b, 0, col_v)),
                  pl.BlockSpec((bb, wb, kw), lambda b: (b, 0, 0)),
                  pl.BlockSpec((bb, wb, kw), lambda b: (b, 0, 0)),
                  pl.BlockSpec((ATTN_HEADS, tp, wb), lambda b: (0, 0, 0)),
                  pl.BlockSpec((ATTN_HEADS, tp, tp), lambda b: (0, 0, 0))],
        out_specs=pl.BlockSpec((bb, tp, aw), lambda b: (b, 0, 0)),
        out_shape=jax.ShapeDtypeStruct((bsz, tp, aw), BF16),
        compiler_params=_cparams("parallel"),
        name="swa_sample",
    )(sinks, proj3, proj3, proj3, k_cache, v_cache, bias_c, bias_n)


def _split3(x):
    hi = x.astype(BF16)
    r = x - hi.astype(F32)
    mid = r.astype(BF16)
    lo = (r - mid.astype(F32)).astype(BF16)
    return hi, mid, lo


def _hgrn_kernel(*refs, t_valid, has_state):
    if has_state:
        q_ref, f_ref, i_ref, g_ref, lb_ref, nw_ref, s0_ref, o_ref, s_ref, st_scr = refs
    else:
        q_ref, f_ref, i_ref, g_ref, lb_ref, nw_ref, o_ref, s_ref, st_scr = refs
    bb, chunk, width = q_ref.shape
    dk = width // HG_HEADS
    c = pl.program_id(1)

    @pl.when(c == 0)
    def _init():
        if has_state:
            st_scr[...] = s0_ref[...]
        else:
            st_scr[...] = jnp.zeros_like(st_scr)

    row = lax.broadcasted_iota(jnp.int32, (chunk, chunk), 0)
    col = lax.broadcasted_iota(jnp.int32, (chunk, chunk), 1)
    causal = row >= col
    tri = jnp.where(causal, 1.0, 0.0).astype(BF16)
    row_w = lax.broadcasted_iota(jnp.int32, (chunk, width), 0)
    valid = row_w < t_valid
    ones_rows = jnp.where(lax.broadcasted_iota(jnp.int32, (chunk, dk), 0) < 3, 1.0, 0.0).astype(BF16)
    mid_row = chunk // 2

    def body(b, carry):
        qx = q_ref[b]
        q = qx * _sigmoid(qx)
        lb = lb_ref[...]
        f = lb + (1.0 - lb) * _sigmoid(f_ref[b])
        k = 1.0 - f
        lg = jnp.log(f)
        if t_valid < chunk:
            k = jnp.where(valid, k, 0.0)
            lg = jnp.where(valid, lg, 0.0)
        v = i_ref[b].astype(BF16)
        cum = sum(_dot(tri, part) for part in _split3(lg))
        cum_mid = cum[mid_row:mid_row + 1, :]
        cum_last = cum[chunk - 1:chunk, :]
        qt = (q * jnp.exp(cum - cum_mid)).astype(BF16)
        kt = (k * jnp.exp(cum_mid - cum)).astype(BF16)
        qe = (q * jnp.exp(cum)).astype(BF16)
        kd = (k * jnp.exp(cum_last - cum)).astype(BF16)
        dec_rows = jnp.zeros((chunk, width), F32)
        for j, part in enumerate(_split3(jnp.exp(cum_last))):
            dec_rows = jnp.where(row_w == j, part.astype(F32), dec_rows)
        dec_rows = dec_rows.astype(BF16)
        heads = range(HG_HEADS)
        sl = [slice(h * dk, (h + 1) * dk) for h in heads]
        st = [st_scr[b, h] for h in heads]
        a = [_nt(qt[:, sl[h]], kt[:, sl[h]]) for h in heads]
        inter = [_dot(qe[:, sl[h]], st[h].astype(BF16)) for h in heads]
        upd = [_tn(kd[:, sl[h]], v[:, sl[h]]) for h in heads]
        decay = [_tn(dec_rows[:, sl[h]], ones_rows) for h in heads]
        for h in heads:
            st_scr[b, h] = st[h] * decay[h] + upd[h]
        a = [jnp.where(causal, a[h], 0.0).astype(BF16) for h in heads]
        o = [_dot(a[h], v[:, sl[h]]) + inter[h] for h in heads]
        outs = [o[h] * lax.rsqrt(jnp.mean(o[h] * o[h], axis=-1, keepdims=True) + EPS) for h in heads]
        gx = g_ref[b]
        o = jnp.concatenate(outs, axis=1) * nw_ref[...] * (gx * _sigmoid(gx))
        o_ref[b] = o.astype(o_ref.dtype)
        return carry

    lax.fori_loop(0, bb, body, 0, unroll=8)

    @pl.when(c == pl.num_programs(1) - 1)
    def _final():
        s_ref[...] = st_scr[...]


def _hgrn(proj3, lb, norm_w, s0, cols, chunk, t_valid, bb):
    bsz, t, _ = proj3.shape
    width = lb.shape[0]
    dk = width // HG_HEADS
    has_state = s0 is not None
    spec = lambda cb: pl.BlockSpec((bb, chunk, width), lambda b, c: (b, c, cb))
    vec = pl.BlockSpec((1, width), lambda b, c: (0, 0))
    st_spec = pl.BlockSpec((bb, HG_HEADS, dk, dk), lambda b, c: (b, 0, 0, 0))
    in_specs = [spec(cols[0]), spec(cols[1]), spec(cols[2]), spec(cols[3]), vec, vec]
    args = [proj3, proj3, proj3, proj3, lb.reshape(1, width), norm_w.reshape(1, width)]
    if has_state:
        in_specs.append(st_spec)
        args.append(s0)
    return pl.pallas_call(
        functools.partial(_hgrn_kernel, t_valid=t_valid, has_state=has_state),
        grid=(bsz // bb, t // chunk),
        in_specs=in_specs,
        out_specs=[pl.BlockSpec((bb, chunk, width), lambda b, c: (b, c, 0)), st_spec],
        out_shape=[jax.ShapeDtypeStruct((bsz, t, width), BF16),
                   jax.ShapeDtypeStruct((bsz, HG_HEADS, dk, dk), F32)],
        scratch_shapes=[pltpu.VMEM((bb, HG_HEADS, dk, dk), F32)],
        compiler_params=_cparams("parallel", "arbitrary"),
        name="hgrn_state" if has_state else "hgrn_prompt",
    )(*args)


def _out_proj_kernel(x_ref, a_ref, oh_ref, wo_ref, nw_ref, h_ref, hnt_ref):
    aw = a_ref.shape[1]
    mix = _dot(a_ref[...], wo_ref[:aw, :]) + _dot(oh_ref[...], wo_ref[aw:, :])
    h = x_ref[...] + mix
    h_ref[...] = h
    hn = h * lax.rsqrt(jnp.mean(h * h, axis=-1, keepdims=True) + EPS) * nw_ref[...]
    hnt_ref[...] = hn.T.astype(hnt_ref.dtype)


def _out_proj(x2d, attn, oh, wo_bf16, norm_w, tm):
    t, d = x2d.shape
    aw, hw = attn.shape[1], oh.shape[1]
    return pl.pallas_call(
        _out_proj_kernel,
        grid=(t // tm,),
        in_specs=[pl.BlockSpec((tm, d), lambda i: (i, 0)),
                  pl.BlockSpec((tm, aw), lambda i: (i, 0)),
                  pl.BlockSpec((tm, hw), lambda i: (i, 0)),
                  pl.BlockSpec((aw + hw, d), lambda i: (0, 0)),
                  pl.BlockSpec((1, d), lambda i: (0, 0))],
        out_specs=[pl.BlockSpec((tm, d), lambda i: (i, 0)),
                   pl.BlockSpec((d, tm), lambda i: (0, i))],
        out_shape=[jax.ShapeDtypeStruct((t, d), F32),
                   jax.ShapeDtypeStruct((d, t), BF16)],
        compiler_params=_cparams("parallel"),
        name="out_proj",
    )(x2d, attn, oh, wo_bf16, norm_w.reshape(1, d))


def _peer_scores_kernel(hnt_ref, wqt_ref, keys_ref, sc_ref):
    qt = _dot(wqt_ref[...], hnt_ref[...])
    half = keys_ref.shape[2]
    for hc in range(keys_ref.shape[0]):
        s = _dot(keys_ref[hc], qt[hc * half:(hc + 1) * half, :].astype(BF16))
        for lt in range(sc_ref.shape[0]):
            sc_ref[lt, hc] = s[:, lt * LANES:(lt + 1) * LANES]


def _peer_scores(hnt, wqt_bf16, keys_bf16, tb):
    d, t = hnt.shape
    nhc, nk, half = keys_bf16.shape
    return pl.pallas_call(
        _peer_scores_kernel,
        grid=(t // tb,),
        in_specs=[pl.BlockSpec((d, tb), lambda i: (0, i)),
                  pl.BlockSpec((nhc * half, d), lambda i: (0, 0)),
                  pl.BlockSpec((nhc, nk, half), lambda i: (0, 0, 0))],
        out_specs=pl.BlockSpec((tb // LANES, nhc, nk, LANES), lambda i: (i, 0, 0, 0)),
        out_shape=jax.ShapeDtypeStruct((t // LANES, nhc, nk, LANES), F32),
        compiler_params=_cparams("parallel"),
        name="peer_scores",
    )(hnt, wqt_bf16, keys_bf16)


def _sort16_pairs():
    def merge(lo, hi, r):
        step = r * 2
        if step < hi - lo:
            yield from merge(lo, hi, step)
            yield from merge(lo + r, hi, step)
            yield from [(i, i + r) for i in range(lo + r, hi - r, step)]
        else:
            yield (lo, lo + r)

    def sort(lo, hi):
        if hi - lo >= 1:
            mid = lo + (hi - lo) // 2
            yield from sort(lo, mid)
            yield from sort(mid + 1, hi)
            yield from merge(lo, hi, 1)

    return tuple(sort(0, PEER_TOPK - 1))


_SORT16 = _sort16_pairs()


def _bitonic_to_sorted(z):
    z = list(z)
    d = PEER_TOPK // 2
    while d >= 1:
        for i in range(PEER_TOPK):
            if i & d == 0:
                hi, lo = jnp.maximum(z[i], z[i + d]), jnp.minimum(z[i], z[i + d])
                z[i], z[i + d] = hi, lo
        d //= 2
    return z


def _merge_bitonic(top, other):
    z = list(top)
    m = len(other)
    for r in range(PEER_TOPK - m, PEER_TOPK):
        z[r] = jnp.maximum(top[r], other[PEER_TOPK - 1 - r])
    return z


def _top16_desc(x):
    n = x.shape[0] // SUBLANES
    xs = [x[g * SUBLANES:(g + 1) * SUBLANES, :] for g in range(n)]
    for i, j in _SORT16:
        xs[i], xs[j] = jnp.maximum(xs[i], xs[j]), jnp.minimum(xs[i], xs[j])
    shift = SUBLANES // 2
    while shift >= 1:
        ys = [pltpu.roll(v, shift, axis=0) for v in xs]
        xs = _bitonic_to_sorted(_merge_bitonic(xs, ys))
        shift //= 2
    return xs


def _peer_select(sc_ref, thr_scr, pw_scr, q_scr, lt):
    assert PEER_HEADS == SUBLANES
    sub = lax.broadcasted_iota(jnp.int32, (SUBLANES, LANES), 0)
    a = b = None
    for h in range(PEER_HEADS):
        a_h = _top16_desc(sc_ref[lt, 2 * h])
        b_h = _top16_desc(sc_ref[lt, 2 * h + 1])
        a = a_h if h == 0 else [jnp.where(sub == h, new, old) for new, old in zip(a_h, a)]
        b = b_h if h == 0 else [jnp.where(sub == h, new, old) for new, old in zip(b_h, b)]
    lists = [[a[r] + b[c] for c in range(PEER_TOPK // (r + 1))] for r in range(SUBLANES)]
    lists.append([a[r] + b[0] for r in range(SUBLANES, PEER_TOPK)])
    top = lists[0]
    for other in lists[1:-1]:
        top = _bitonic_to_sorted(_merge_bitonic(top, other))
    z = _merge_bitonic(top, lists[-1])
    tau = functools.reduce(jnp.minimum, z)
    best = a[0] + b[0]
    zsum = jnp.zeros_like(tau)
    inf = jnp.full_like(tau, jnp.inf)
    thr_rank = []
    for r, cand in enumerate(lists):
        hits = [v >= tau for v in cand]
        for v, hit in zip(cand, hits):
            zsum = zsum + jnp.where(hit, jnp.exp(v - best), 0.0)
        if r < SUBLANES:
            t = inf
            for c, hit in enumerate(hits):
                t = jnp.where(hit, b[c], t)
            thr_rank.append(t)
        else:
            thr_rank.extend(jnp.where(hit, b[0], inf) for hit in hits)
    inv = 0.5 / zsum
    for h in range(PEER_HEADS):
        own = lambda v: jnp.broadcast_to(v[h:h + 1, :], (SUBLANES, LANES))
        a_h = [own(v) for v in a]
        thr_h = [own(v) for v in thr_rank]
        inv_h, b0_h = own(inv), own(b[0])
        s0 = sc_ref[lt, 2 * h]
        s1 = sc_ref[lt, 2 * h + 1]
        for g in range(s0.shape[0] // SUBLANES):
            rows = slice(g * SUBLANES, (g + 1) * SUBLANES)
            x0 = s0[rows, :]
            thr = jnp.full_like(x0, jnp.inf)
            for r in range(PEER_TOPK):
                thr = jnp.where(x0 == a_h[r], thr_h[r], thr)
            thr_scr[lt, h, rows, :] = thr
            pw_scr[lt, h, rows, :] = jnp.exp(x0 - a_h[0]) * inv_h
            q_scr[lt, h, rows, :] = jnp.exp(s1[rows, :] - b0_h)


MXU_TILE = 256
MXU_COUNT = 2
ACC_ROWS = 512
ACC_PIECE = 32
ACC_PRE = 0
ACC_OUT = ACC_ROWS // 4


def _peer_dense_kernel(sc_ref, hnt_ref, u_ref, vt_ref, res_ref, nw_ref, y_ref, thr_scr, pw_scr, q_scr,
                       h0_scr, h1_scr, g0_scr, g1_scr, yt_ref):
    s = pl.program_id(1)
    n_e = pl.num_programs(1) - 2
    d, tb = hnt_ref.shape
    n_lt, _, nk, _ = sc_ref.shape
    eb = u_ref.shape[0]
    n_i = eb // nk
    assert tb == MXU_COUNT * MXU_TILE and eb % ACC_ROWS == 0 and d % ACC_ROWS == 0

    @pl.when(s == 0)
    def _select():
        def body(lt, carry):
            _peer_select(sc_ref, thr_scr, pw_scr, q_scr, lt)
            return carry
        lax.fori_loop(0, n_lt, body, 0)
        yt_ref[...] = jnp.zeros_like(yt_ref)
        g0_scr[...] = jnp.zeros_like(g0_scr)
        g1_scr[...] = jnp.zeros_like(g1_scr)

    i0 = pl.multiple_of(jnp.clip(s - 1, 0, n_e - 1) * n_i, SUBLANES)

    def gate_steps(lt, ii, h_r, g_w):
        lanes = slice(lt * LANES, (lt + 1) * LANES)
        rows = slice(ii * nk, (ii + 1) * nk)
        tiled = (nk // SUBLANES, SUBLANES, LANES)
        state = {"w": jnp.zeros(tiled, F32)}

        def head(h):
            thr = thr_scr[lt, h, pl.ds(i0 + ii, SUBLANES, stride=0), :]
            pw = pw_scr[lt, h, pl.ds(i0 + ii, SUBLANES, stride=0), :]
            s1 = sc_ref[lt, 2 * h + 1].reshape(tiled)
            state["w"] = state["w"] + jnp.where(s1 >= thr[None], q_scr[lt, h].reshape(tiled), 0.0) * pw[None]
            if h == PEER_HEADS - 1:
                x = h_r[rows, lanes]
                act = x + x * lax.erf(x * (2.0 ** -0.5))
                g_w[rows, lanes] = (state["w"].reshape(nk, LANES) * act).astype(g_w.dtype)

        return [functools.partial(head, h) for h in range(PEER_HEADS)]

    def mxu_steps(kind, c, k, reg, h_w, g_r):
        lhs_ref, rhs_ref, acc, n_k = ((u_ref, hnt_ref, ACC_PRE, d // MXU_TILE) if kind == "pre"
                                      else (vt_ref, g_r, ACC_OUT, eb // MXU_TILE))
        kc = slice(k * MXU_TILE, (k + 1) * MXU_TILE)

        def push():
            for q in range(MXU_COUNT):
                pltpu.matmul_push_rhs(rhs_ref[kc, q * MXU_TILE:(q + 1) * MXU_TILE], staging_register=reg,
                                      mxu_index=q)

        def piece(p):
            r0 = c * ACC_ROWS + p * ACC_PIECE
            lhs = lhs_ref[r0:r0 + ACC_PIECE, kc]
            for q in range(MXU_COUNT):
                pltpu.matmul_acc_lhs(acc + p * ACC_PIECE // 4, lhs, q, load_staged_rhs=reg if p == 0 else None)

        def pop(p):
            rows = slice(c * ACC_ROWS + p * ACC_PIECE, c * ACC_ROWS + (p + 1) * ACC_PIECE)
            for q in range(MXU_COUNT):
                cols = slice(q * MXU_TILE, (q + 1) * MXU_TILE)
                res = pltpu.matmul_pop(acc + p * ACC_PIECE // 4, (ACC_PIECE, MXU_TILE), F32, q)
                if kind == "pre":
                    h_w[rows, cols] = res
                else:
                    yt_ref[rows, cols] += res

        n_p = ACC_ROWS // ACC_PIECE
        pops = [functools.partial(pop, p) for p in range(n_p)] if k == n_k - 1 else []
        return push, [functools.partial(piece, p) for p in range(n_p)], pops

    def stage(h_w, h_r, g_w, g_r, pre, gating, out):
        pre_groups = [("pre", c, k) for c in range(eb // ACC_ROWS) for k in range(d // MXU_TILE)] if pre else []
        out_groups = [("out", c, k) for c in range(d // ACC_ROWS) for k in range(eb // MXU_TILE)] if out else []
        if pre and out:
            order = [g for pair in zip(pre_groups, out_groups) for g in pair]
        else:
            order = pre_groups + out_groups
        steps = [mxu_steps(kind, c, k, gi % 2, h_w, g_r) for gi, (kind, c, k) in enumerate(order)]
        mxu = [steps[0][0]]
        lagged = []
        for gi, (_, pieces, pops) in enumerate(steps):
            half = len(pieces) // 2
            for p, piece in enumerate(pieces):
                if p == half and gi + 1 < len(steps):
                    mxu.append(steps[gi + 1][0])
                mxu.append(piece)
                if lagged:
                    mxu.append(lagged.pop(0))
            same_acc_next = gi + 1 < len(steps) and order[gi + 1][0] == order[gi][0]
            if same_acc_next or gi + 1 == len(steps):
                mxu.extend(pops)
            else:
                lagged = list(pops)
        vpu = [t for lt in range(n_lt) for ii in range(n_i) for t in gate_steps(lt, ii, h_r, g_w)] if gating else []
        im = iv = 0
        while im < len(mxu) or iv < len(vpu):
            if iv >= len(vpu) or (im < len(mxu) and im * len(vpu) <= iv * len(mxu)):
                mxu[im]()
                im += 1
            else:
                vpu[iv]()
                iv += 1

    last = n_e + 1

    @pl.when(s == 0)
    def _first():
        stage(h0_scr, None, None, None, True, False, False)

    @pl.when((s > 0) & (s < last) & (s % 2 == 0))
    def _even():
        stage(h0_scr, h1_scr, g1_scr, g0_scr, True, True, True)

    @pl.when((s > 0) & (s < last) & (s % 2 == 1))
    def _odd():
        stage(h1_scr, h0_scr, g0_scr, g1_scr, True, True, True)

    def finish(g_r):
        stage(None, None, None, g_r, False, False, True)
        y = res_ref[...] + yt_ref[...].T
        y_ref[...] = y * lax.rsqrt(jnp.mean(y * y, axis=-1, keepdims=True) + EPS) * nw_ref[...]

    @pl.when((s == last) & (s % 2 == 0))
    def _last_even():
        finish(g0_scr)

    @pl.when((s == last) & (s % 2 == 1))
    def _last_odd():
        finish(g1_scr)


def _peer_dense(sc, hnt, u_bf16, vt_bf16, resid, norm_w, tb, eb):
    _, nhc, nk, _ = sc.shape
    d, t = hnt.shape
    n_exp = u_bf16.shape[0]
    n_lt = tb // LANES
    assert eb == SUBLANES * nk and n_exp == nk * nk and t % tb == 0 and tb % LANES == 0
    sel = pltpu.VMEM((n_lt, PEER_HEADS, nk, LANES), F32)
    pre = pltpu.VMEM((eb, tb + LANES), F32)
    gated = pltpu.VMEM((eb, tb), BF16)
    n_e = n_exp // eb
    return pl.pallas_call(
        _peer_dense_kernel,
        grid=(t // tb, n_e + 2),
        in_specs=[pl.BlockSpec((n_lt, nhc, nk, LANES), lambda i, s: (i, 0, 0, 0)),
                  pl.BlockSpec((d, tb), lambda i, s: (0, i)),
                  pl.BlockSpec((eb, d), lambda i, s: (jnp.minimum(s, n_e - 1), 0)),
                  pl.BlockSpec((d, eb), lambda i, s: (0, jnp.clip(s - 2, 0, n_e - 1))),
                  pl.BlockSpec((tb, d), lambda i, s: (i, 0)),
                  pl.BlockSpec((1, d), lambda i, s: (0, 0))],
        out_specs=pl.BlockSpec((tb, d), lambda i, s: (i, 0)),
        out_shape=jax.ShapeDtypeStruct((t, d), F32),
        scratch_shapes=[sel, sel, sel, pre, pre, gated, gated, pltpu.VMEM((d, tb), F32)],
        compiler_params=_cparams("parallel", "arbitrary"),
        name="peer_dense",
    )(sc, hnt, u_bf16, vt_bf16, resid, norm_w.reshape(1, d))


TOKEN_BLOCK = 512
SWA_BLOCKS_PER_STEP = 4
EXPERT_BLOCK = 1024
SAMPLE_T_PAD = 16


def _ffn(x2d, attn, oh, w, tb):
    h, hnt = _out_proj(x2d, attn, oh, w['wo'], w['norm_ffn'], tb)
    sc = _peer_scores(hnt, w['wqt'], w['keys'], tb)
    return _peer_dense(sc, hnt, w['u'], w['vt'], h, w['norm_final'], tb, EXPERT_BLOCK)


def kernel(x_prompt, x_sample, cache_k_win, cache_v_win, state_hgrn, norm_mix_w, w_in, attn_sinks,
           rel_bias_table, hg_lb, hg_norm_w, w_o, norm_ffn_w, peer_w_q, peer_sub_keys, peer_u, peer_v,
           norm_final_w):
    bsz, seq, d = x_prompt.shape
    dbsz, dseq, _ = x_sample.shape
    aw = ATTN_HEADS * HEAD_DIM
    kw = ATTN_KV_HEADS * HEAD_DIM
    hw = hg_norm_w.shape[1]
    wb = cache_k_win.shape[2]

    wi = w_in[0]
    w_in_r = jnp.concatenate([wi[:, :aw], wi[:, aw + 2 * kw:], wi[:, aw:aw + 2 * kw]], axis=1).astype(BF16)
    col_k = (aw + 4 * hw) // kw
    col_v = col_k + 1
    hg_cols = (1, 2, 3, 4)
    lb = jax.nn.softmax(hg_lb.astype(F32), axis=0)[0]
    nhc = PEER_HEADS * 2
    w = {
        'wo': w_o[0].astype(BF16),
        'norm_ffn': norm_ffn_w[0],
        'wqt': peer_w_q[0].astype(BF16).T,
        'keys': peer_sub_keys[0].reshape(nhc, peer_sub_keys.shape[3], peer_sub_keys.shape[4]).astype(BF16),
        'u': peer_u[0].astype(BF16),
        'vt': peer_v[0].astype(BF16).T,
        'norm_final': norm_final_w,
    }
    sinks = attn_sinks[0].astype(F32)

    proj_p = _in_proj(x_prompt.reshape(bsz * seq, d), norm_mix_w[0], w_in_r, TOKEN_BLOCK)
    proj_p3 = proj_p.reshape(bsz, seq, -1)
    blk = WINDOW
    dist_p = (jnp.arange(blk)[:, None] + blk) - jnp.arange(2 * blk)[None, :]
    in_win = (dist_p >= 0) & (dist_p <= WINDOW)
    has_prev = jnp.arange(2 * blk)[None, :] >= blk
    bias_p = jnp.stack([_masked_bias(rel_bias_table, blk, 2 * blk, blk, in_win & has_prev),
                        _masked_bias(rel_bias_table, blk, 2 * blk, blk, in_win)])
    attn_p = _swa_prompt(proj_p3, sinks, bias_p, 0, col_k, col_v)
    oh_p, st_p = _hgrn(proj_p3, lb, hg_norm_w[0], None, hg_cols, HG_CHUNK, HG_CHUNK, bsz)
    y_p = _ffn(x_prompt.reshape(bsz * seq, d), attn_p.reshape(bsz * seq, aw),
               oh_p.reshape(bsz * seq, hw), w, TOKEN_BLOCK)
    k_off = aw + 4 * hw
    wp = min(WINDOW, seq)
    k_win_p = proj_p3[:, seq - wp:, k_off:k_off + kw].reshape(1, bsz, wp, ATTN_KV_HEADS, HEAD_DIM)
    v_win_p = proj_p3[:, seq - wp:, k_off + kw:k_off + 2 * kw].reshape(1, bsz, wp, ATTN_KV_HEADS, HEAD_DIM)

    tp = SAMPLE_T_PAD
    xs_pad = jnp.pad(x_sample, ((0, 0), (0, tp - dseq), (0, 0)))
    proj_s3 = _in_proj(xs_pad.reshape(dbsz * tp, d), norm_mix_w[0], w_in_r, TOKEN_BLOCK).reshape(dbsz, tp, -1)
    k_new = proj_s3[:, :dseq, k_off:k_off + kw]
    v_new = proj_s3[:, :dseq, k_off + kw:k_off + 2 * kw]
    k_cache = cache_k_win[0].reshape(dbsz, wb, kw)
    v_cache = cache_v_win[0].reshape(dbsz, wb, kw)
    kk = jnp.concatenate([k_cache, k_new], axis=1)
    vv = jnp.concatenate([v_cache, v_new], axis=1)
    q_pos = jnp.arange(tp)[:, None]
    k_pos = jnp.arange(wb + tp)[None, :]
    dist_s = wb + q_pos - k_pos
    mask_s = (dist_s >= 0) & (dist_s <= WINDOW) & (q_pos < dseq) & (k_pos < wb + dseq)
    bias_s = _masked_bias(rel_bias_table, tp, wb + tp, wb, mask_s)
    attn_s = _swa_sample(proj_s3, k_cache, v_cache, sinks, bias_s[:, :, :wb], bias_s[:, :, wb:],
                         0, col_k, col_v, 16)
    oh_s, st_s = _hgrn(proj_s3, lb, hg_norm_w[0], state_hgrn[0], hg_cols, tp, dseq, 8)
    y_s = _ffn(x_sample.reshape(dbsz * dseq, d), attn_s[:, :dseq].reshape(dbsz * dseq, aw),
               oh_s[:, :dseq].reshape(dbsz * dseq, hw), w, TOKEN_BLOCK)
    k_win_s = kk[:, dseq:].reshape(1, dbsz, wb, ATTN_KV_HEADS, HEAD_DIM)
    v_win_s = vv[:, dseq:].reshape(1, dbsz, wb, ATTN_KV_HEADS, HEAD_DIM)

    return (y_p.reshape(bsz, seq, d), y_s.reshape(dbsz, dseq, d), k_win_p, v_win_p, st_p[None],
            k_win_s, v_win_s, st_s[None])
```

```python
import functools
import math

import jax
import jax.numpy as jnp
from jax import lax
from jax.experimental import pallas as pl
from jax.experimental.pallas import tpu as pltpu

F32 = jnp.float32
BF16 = jnp.bfloat16

EPS = 1e-6
NEG = -1e30

ATTN_HEADS = 8
ATTN_KV_HEADS = 2
HEAD_DIM = 64
WINDOW = 128
REL_BUCKETS = 32
HG_HEADS = 4
HG_CHUNK = 64
PEER_HEADS = 8
PEER_TOPK = 16

LANES = 128
SUBLANES = 8
VMEM_LIMIT = 56 * 1024 * 1024


def _cparams(*sem):
    return pltpu.CompilerParams(dimension_semantics=sem, vmem_limit_bytes=VMEM_LIMIT)


def _nt(a, b):
    return lax.dot_general(a, b, (((1,), (1,)), ((), ())), preferred_element_type=F32)


def _tn(a, b):
    return lax.dot_general(a, b, (((0,), (0,)), ((), ())), preferred_element_type=F32)


def _dot(a, b):
    return jnp.dot(a, b, preferred_element_type=F32)


def _sigmoid(x):
    return 1.0 / (1.0 + jnp.exp(-x))


def _in_proj_kernel(x_ref, nw_ref, w_ref, o_ref):
    x = x_ref[...]
    xn = x * lax.rsqrt(jnp.mean(x * x, axis=-1, keepdims=True) + EPS) * nw_ref[...]
    o_ref[...] = _dot(xn.astype(BF16), w_ref[...])


def _in_proj(x2d, norm_w, w_bf16, tm):
    t, d = x2d.shape
    n = w_bf16.shape[1]
    return pl.pallas_call(
        _in_proj_kernel,
        grid=(t // tm,),
        in_specs=[pl.BlockSpec((tm, d), lambda i: (i, 0)),
                  pl.BlockSpec((1, d), lambda i: (0, 0)),
                  pl.BlockSpec((d, n), lambda i: (0, 0))],
        out_specs=pl.BlockSpec((tm, n), lambda i: (i, 0)),
        out_shape=jax.ShapeDtypeStruct((t, n), F32),
        compiler_params=_cparams("parallel"),
        name="in_proj",
    )(x2d, norm_w.reshape(1, d), w_bf16)


def _t5_bucket(dist):
    n = jnp.maximum(dist, 0)
    max_exact = REL_BUCKETS // 2
    nf = jnp.maximum(n, 1).astype(F32)
    large = max_exact + (jnp.log(nf / max_exact) / math.log(WINDOW / max_exact)
                         * (REL_BUCKETS - max_exact)).astype(jnp.int32)
    large = jnp.minimum(large, REL_BUCKETS - 1)
    return jnp.where(n < max_exact, n, large)


def _masked_bias(table, n_q, n_k, offset, mask):
    h = table.shape[1]
    diag = jnp.arange(n_q + n_k - 1) - (n_k - 1) + offset
    per_diag = table.astype(F32)[_t5_bucket(diag)].T
    w = jnp.pad(per_diag[:, ::-1], ((0, 0), (0, 1)))
    p = n_q + n_k
    skew = jnp.tile(w, (1, n_q))[:, :n_q * (p - 1)].reshape(h, n_q, p - 1)
    return jnp.where(mask[None], skew[:, :, n_q - 1:n_q - 1 + n_k], NEG)


def _swa_prompt_kernel(sink_ref, q_ref, kp_ref, kc_ref, vp_ref, vc_ref, bias0_ref, bias1_ref, o_ref):
    scale = HEAD_DIM ** -0.5
    assert math.frexp(scale)[0] == 0.5
    group = ATTN_HEADS // ATTN_KV_HEADS
    blk = kp_ref.shape[1]
    assert 2 * HEAD_DIM == LANES and kp_ref.shape[2] == LANES and ATTN_KV_HEADS == 2 and group % 2 == 0
    n_sub = q_ref.shape[1] // blk
    kk = [jnp.concatenate([kp_ref[0], kc_ref[0, :blk]], axis=0)]
    vv = [jnp.concatenate([vp_ref[0], vc_ref[0, :blk]], axis=0)]
    kk += [kc_ref[0, (j - 1) * blk:(j + 1) * blk] for j in range(1, n_sub)]
    vv += [vc_ref[0, (j - 1) * blk:(j + 1) * blk] for j in range(1, n_sub)]
    bias = [bias0_ref] + [bias1_ref] * (n_sub - 1)
    low = lax.broadcasted_iota(jnp.int32, kk[0].shape, 1) < HEAD_DIM

    def halves(x, kvh):
        own = jnp.where(low if kvh == 0 else ~low, x, 0.0)
        other = pltpu.roll(own, HEAD_DIM, axis=1)
        lo, hi = (own, other) if kvh == 0 else (other, own)
        return lo.astype(BF16), hi.astype(BF16)

    k_half = [[halves(kk[j], kvh) for kvh in range(ATTN_KV_HEADS)] for j in range(n_sub)]
    v_half = [[halves(vv[j], kvh) for kvh in range(ATTN_KV_HEADS)] for j in range(n_sub)]
    units = [(j, h) for j in range(n_sub) for h in range(ATTN_HEADS)]
    kv_of = lambda h: h // group
    qt = [[(q_ref[0, j * blk:(j + 1) * blk, t * LANES:(t + 1) * LANES] * scale).astype(BF16)
           for t in range(ATTN_HEADS // 2)] for j in range(n_sub)]
    s = {(j, h): _nt(qt[j][h // 2], k_half[j][kv_of(h)][h % 2]) + bias[j][0, h] for j, h in units}
    m = {u: jnp.maximum(jnp.max(s[u], axis=-1, keepdims=True), sink_ref[u[1]]) for u in units}
    p = {u: jnp.exp(s[u] - m[u]) for u in units}
    den = {u: jnp.sum(p[u], axis=-1, keepdims=True) + jnp.exp(sink_ref[u[1]] - m[u]) for u in units}
    p = {u: (p[u] * (1.0 / den[u])).astype(BF16) for u in units}
    for j in range(n_sub):
        for tile in range(ATTN_HEADS // 2):
            kvh = kv_of(2 * tile)
            o = _dot(p[j, 2 * tile], v_half[j][kvh][0]) + _dot(p[j, 2 * tile + 1], v_half[j][kvh][1])
            o_ref[0, j * blk:(j + 1) * blk, tile * LANES:(tile + 1) * LANES] = o.astype(o_ref.dtype)


def _swa_prompt(proj3, sinks, bias, col_q, col_k, col_v):
    bsz, seq, _ = proj3.shape
    blk = WINDOW
    aw = ATTN_HEADS * HEAD_DIM
    kw = ATTN_KV_HEADS * HEAD_DIM
    per = SWA_BLOCKS_PER_STEP
    prev = lambda b, n: (b, jnp.maximum(per * n - 1, 0))
    bias_spec = lambda pick: pl.BlockSpec((1, ATTN_HEADS, blk, 2 * blk), lambda b, n: (pick(n), 0, 0, 0))
    return pl.pallas_call(
        _swa_prompt_kernel,
        grid=(bsz, seq // (per * blk)),
        in_specs=[pl.BlockSpec(memory_space=pltpu.SMEM),
                  pl.BlockSpec((1, per * blk, aw), lambda b, n: (b, n, col_q)),
                  pl.BlockSpec((1, blk, kw), lambda b, n: prev(b, n) + (col_k,)),
                  pl.BlockSpec((1, per * blk, kw), lambda b, n: (b, n, col_k)),
                  pl.BlockSpec((1, blk, kw), lambda b, n: prev(b, n) + (col_v,)),
                  pl.BlockSpec((1, per * blk, kw), lambda b, n: (b, n, col_v)),
                  bias_spec(lambda n: jnp.minimum(n, 1)),
                  bias_spec(lambda n: 1)],
        out_specs=pl.BlockSpec((1, per * blk, aw), lambda b, n: (b, n, 0)),
        out_shape=jax.ShapeDtypeStruct((bsz, seq, aw), BF16),
        compiler_params=_cparams("parallel", "arbitrary"),
        name="swa_prompt",
    )(sinks, proj3, proj3, proj3, proj3, proj3, bias, bias)


def _swa_sample_kernel(sink_ref, q_ref, kn_ref, vn_ref, kc_ref, vc_ref, bias_c_ref, bias_n_ref, o_ref):
    scale = HEAD_DIM ** -0.5
    group = ATTN_HEADS // ATTN_KV_HEADS
    heads = range(ATTN_HEADS)
    kv = [slice((h // group) * HEAD_DIM, (h // group + 1) * HEAD_DIM) for h in heads]
    dot_qk = lambda a, b: jnp.einsum('bqd,bkd->bqk', a, b, preferred_element_type=F32)
    dot_pv = lambda a, b: jnp.einsum('bqk,bkd->bqd', a, b, preferred_element_type=F32)
    qh = [q_ref[:, :, h * HEAD_DIM:(h + 1) * HEAD_DIM].astype(BF16) for h in heads]
    sc = [dot_qk(qh[h], kc_ref[:, :, kv[h]].astype(BF16)) * scale + bias_c_ref[h][None] for h in heads]
    sn = [dot_qk(qh[h], kn_ref[:, :, kv[h]].astype(BF16)) * scale + bias_n_ref[h][None] for h in heads]
    m = [jnp.maximum(jnp.maximum(jnp.max(sc[h], axis=-1, keepdims=True),
                                 jnp.max(sn[h], axis=-1, keepdims=True)), sink_ref[h]) for h in heads]
    pc = [jnp.exp(sc[h] - m[h]) for h in heads]
    pn = [jnp.exp(sn[h] - m[h]) for h in heads]
    den = [jnp.sum(pc[h], axis=-1, keepdims=True) + jnp.sum(pn[h], axis=-1, keepdims=True)
           + jnp.exp(sink_ref[h] - m[h]) for h in heads]
    for h in heads:
        o = (dot_pv(pc[h].astype(BF16), vc_ref[:, :, kv[h]].astype(BF16))
             + dot_pv(pn[h].astype(BF16), vn_ref[:, :, kv[h]].astype(BF16))) / den[h]
        o_ref[:, :, h * HEAD_DIM:(h + 1) * HEAD_DIM] = o.astype(o_ref.dtype)


def _swa_sample(proj3, k_cache, v_cache, sinks, bias_c, bias_n, col_q, col_k, col_v, bb):
    bsz, tp, _ = proj3.shape
    wb = k_cache.shape[1]
    aw = ATTN_HEADS * HEAD_DIM
    kw = ATTN_KV_HEADS * HEAD_DIM
    return pl.pallas_call(
        _swa_sample_kernel,
        grid=(bsz // bb,),
        in_specs=[pl.BlockSpec(memory_space=pltpu.SMEM),
                  pl.BlockSpec((bb, tp, aw), lambda b: (b, 0, col_q)),
                  pl.BlockSpec((bb, tp, kw), lambda b: (b, 0, col_k)),
                  pl.BlockSpec((bb, tp, kw), lambda b: (b, 0, col_v)),
                  pl.BlockSpec((bb, wb, kw), lambda b: (b, 0, 0)),
                  pl.BlockSpec((bb, wb, kw), lambda b: (b, 0, 0)),
                  pl.BlockSpec((ATTN_HEADS, tp, wb), lambda b: (0, 0, 0)),
                  pl.BlockSpec((ATTN_HEADS, tp, tp), lambda b: (0, 0, 0))],
        out_specs=pl.BlockSpec((bb, tp, aw), lambda b: (b, 0, 0)),
        out_shape=jax.ShapeDtypeStruct((bsz, tp, aw), BF16),
        compiler_params=_cparams("parallel"),
        name="swa_sample",
    )(sinks, proj3, proj3, proj3, k_cache, v_cache, bias_c, bias_n)


def _split3(x):
    hi = x.astype(BF16)
    r = x - hi.astype(F32)
    mid = r.astype(BF16)
    lo = (r - mid.astype(F32)).astype(BF16)
    return hi, mid, lo


def _hgrn_kernel(*refs, t_valid, has_state):
    if has_state:
        q_ref, f_ref, i_ref, g_ref, lb_ref, nw_ref, s0_ref, o_ref, s_ref, st_scr = refs
    else:
        q_ref, f_ref, i_ref, g_ref, lb_ref, nw_ref, o_ref, s_ref, st_scr = refs
    bb, chunk, width = q_ref.shape
    dk = width // HG_HEADS
    c = pl.program_id(1)

    @pl.when(c == 0)
    def _init():
        if has_state:
            st_scr[...] = s0_ref[...]
        else:
            st_scr[...] = jnp.zeros_like(st_scr)

    row = lax.broadcasted_iota(jnp.int32, (chunk, chunk), 0)
    col = lax.broadcasted_iota(jnp.int32, (chunk, chunk), 1)
    causal = row >= col
    tri = jnp.where(causal, 1.0, 0.0).astype(BF16)
    row_w = lax.broadcasted_iota(jnp.int32, (chunk, width), 0)
    valid = row_w < t_valid
    ones_rows = jnp.where(lax.broadcasted_iota(jnp.int32, (chunk, dk), 0) < 3, 1.0, 0.0).astype(BF16)
    mid_row = chunk // 2

    def body(b, carry):
        qx = q_ref[b]
        q = qx * _sigmoid(qx)
        lb = lb_ref[...]
        f = lb + (1.0 - lb) * _sigmoid(f_ref[b])
        k = 1.0 - f
        lg = jnp.log(f)
        if t_valid < chunk:
            k = jnp.where(valid, k, 0.0)
            lg = jnp.where(valid, lg, 0.0)
        v = i_ref[b].astype(BF16)
        cum = sum(_dot(tri, part) for part in _split3(lg))
        cum_mid = cum[mid_row:mid_row + 1, :]
        cum_last = cum[chunk - 1:chunk, :]
        qt = (q * jnp.exp(cum - cum_mid)).astype(BF16)
        kt = (k * jnp.exp(cum_mid - cum)).astype(BF16)
        qe = (q * jnp.exp(cum)).astype(BF16)
        kd = (k * jnp.exp(cum_last - cum)).astype(BF16)
        dec_rows = jnp.zeros((chunk, width), F32)
        for j, part in enumerate(_split3(jnp.exp(cum_last))):
            dec_rows = jnp.where(row_w == j, part.astype(F32), dec_rows)
        dec_rows = dec_rows.astype(BF16)
        heads = range(HG_HEADS)
        sl = [slice(h * dk, (h + 1) * dk) for h in heads]
        st = [st_scr[b, h] for h in heads]
        a = [_nt(qt[:, sl[h]], kt[:, sl[h]]) for h in heads]
        inter = [_dot(qe[:, sl[h]], st[h].astype(BF16)) for h in heads]
        upd = [_tn(kd[:, sl[h]], v[:, sl[h]]) for h in heads]
        decay = [_tn(dec_rows[:, sl[h]], ones_rows) for h in heads]
        for h in heads:
            st_scr[b, h] = st[h] * decay[h] + upd[h]
        a = [jnp.where(causal, a[h], 0.0).astype(BF16) for h in heads]
        o = [_dot(a[h], v[:, sl[h]]) + inter[h] for h in heads]
        outs = [o[h] * lax.rsqrt(jnp.mean(o[h] * o[h], axis=-1, keepdims=True) + EPS) for h in heads]
        gx = g_ref[b]
        o = jnp.concatenate(outs, axis=1) * nw_ref[...] * (gx * _sigmoid(gx))
        o_ref[b] = o.astype(o_ref.dtype)
        return carry

    lax.fori_loop(0, bb, body, 0, unroll=8)

    @pl.when(c == pl.num_programs(1) - 1)
    def _final():
        s_ref[...] = st_scr[...]


def _hgrn(proj3, lb, norm_w, s0, cols, chunk, t_valid, bb):
    bsz, t, _ = proj3.shape
    width = lb.shape[0]
    dk = width // HG_HEADS
    has_state = s0 is not None
    spec = lambda cb: pl.BlockSpec((bb, chunk, width), lambda b, c: (b, c, cb))
    vec = pl.BlockSpec((1, width), lambda b, c: (0, 0))
    st_spec = pl.BlockSpec((bb, HG_HEADS, dk, dk), lambda b, c: (b, 0, 0, 0))
    in_specs = [spec(cols[0]), spec(cols[1]), spec(cols[2]), spec(cols[3]), vec, vec]
    args = [proj3, proj3, proj3, proj3, lb.reshape(1, width), norm_w.reshape(1, width)]
    if has_state:
        in_specs.append(st_spec)
        args.append(s0)
    return pl.pallas_call(
        functools.partial(_hgrn_kernel, t_valid=t_valid, has_state=has_state),
        grid=(bsz // bb, t // chunk),
        in_specs=in_specs,
        out_specs=[pl.BlockSpec((bb, chunk, width), lambda b, c: (b, c, 0)), st_spec],
        out_shape=[jax.ShapeDtypeStruct((bsz, t, width), BF16),
                   jax.ShapeDtypeStruct((bsz, HG_HEADS, dk, dk), F32)],
        scratch_shapes=[pltpu.VMEM((bb, HG_HEADS, dk, dk), F32)],
        compiler_params=_cparams("parallel", "arbitrary"),
        name="hgrn_state" if has_state else "hgrn_prompt",
    )(*args)


def _out_proj_scores_kernel(x_ref, a_ref, oh_ref, wo_ref, nw_ref, wqt_ref, keys_ref, h_ref, hnt_ref, sc_ref):
    aw = a_ref.shape[1]
    mix = _dot(a_ref[...], wo_ref[:aw, :]) + _dot(oh_ref[...], wo_ref[aw:, :])
    h = x_ref[...] + mix
    h_ref[...] = h
    hn = h * lax.rsqrt(jnp.mean(h * h, axis=-1, keepdims=True) + EPS) * nw_ref[...]
    hnt = hn.T.astype(hnt_ref.dtype)
    hnt_ref[...] = hnt
    qt = _dot(wqt_ref[...], hnt)
    half = keys_ref.shape[2]
    for hc in range(keys_ref.shape[0]):
        s = _dot(keys_ref[hc], qt[hc * half:(hc + 1) * half, :].astype(BF16))
        for lt in range(sc_ref.shape[0]):
            sc_ref[lt, hc] = s[:, lt * LANES:(lt + 1) * LANES]


def _out_proj_scores(x2d, attn, oh, wo_bf16, norm_w, wqt_bf16, keys_bf16, tm):
    t, d = x2d.shape
    aw, hw = attn.shape[1], oh.shape[1]
    nhc, nk, half = keys_bf16.shape
    return pl.pallas_call(
        _out_proj_scores_kernel,
        grid=(t // tm,),
        in_specs=[pl.BlockSpec((tm, d), lambda i: (i, 0)),
                  pl.BlockSpec((tm, aw), lambda i: (i, 0)),
                  pl.BlockSpec((tm, hw), lambda i: (i, 0)),
                  pl.BlockSpec((aw + hw, d), lambda i: (0, 0)),
                  pl.BlockSpec((1, d), lambda i: (0, 0)),
                  pl.BlockSpec((nhc * half, d), lambda i: (0, 0)),
                  pl.BlockSpec((nhc, nk, half), lambda i: (0, 0, 0))],
        out_specs=[pl.BlockSpec((tm, d), lambda i: (i, 0)),
                   pl.BlockSpec((d, tm), lambda i: (0, i)),
                   pl.BlockSpec((tm // LANES, nhc, nk, LANES), lambda i: (i, 0, 0, 0))],
        out_shape=[jax.ShapeDtypeStruct((t, d), F32),
                   jax.ShapeDtypeStruct((d, t), BF16),
                   jax.ShapeDtypeStruct((t // LANES, nhc, nk, LANES), F32)],
        compiler_params=_cparams("parallel"),
        name="out_proj_scores",
    )(x2d, attn, oh, wo_bf16, norm_w.reshape(1, d), wqt_bf16, keys_bf16)


def _sort16_pairs():
    def merge(lo, hi, r):
        step = r * 2
        if step < hi - lo:
            yield from merge(lo, hi, step)
            yield from merge(lo + r, hi, step)
            yield from [(i, i + r) for i in range(lo + r, hi - r, step)]
        else:
            yield (lo, lo + r)

    def sort(lo, hi):
        if hi - lo >= 1:
            mid = lo + (hi - lo) // 2
            yield from sort(lo, mid)
            yield from sort(mid + 1, hi)
            yield from merge(lo, hi, 1)

    return tuple(sort(0, PEER_TOPK - 1))


_SORT16 = _sort16_pairs()


def _bitonic_to_sorted(z):
    z = list(z)
    d = PEER_TOPK // 2
    while d >= 1:
        for i in range(PEER_TOPK):
            if i & d == 0:
                hi, lo = jnp.maximum(z[i], z[i + d]), jnp.minimum(z[i], z[i + d])
                z[i], z[i + d] = hi, lo
        d //= 2
    return z


def _merge_bitonic(top, other):
    z = list(top)
    m = len(other)
    for r in range(PEER_TOPK - m, PEER_TOPK):
        z[r] = jnp.maximum(top[r], other[PEER_TOPK - 1 - r])
    return z


def _top16_desc(x):
    n = x.shape[0] // SUBLANES
    xs = [x[g * SUBLANES:(g + 1) * SUBLANES, :] for g in range(n)]
    for i, j in _SORT16:
        xs[i], xs[j] = jnp.maximum(xs[i], xs[j]), jnp.minimum(xs[i], xs[j])
    shift = SUBLANES // 2
    while shift >= 1:
        ys = [pltpu.roll(v, shift, axis=0) for v in xs]
        xs = _bitonic_to_sorted(_merge_bitonic(xs, ys))
        shift //= 2
    return xs


def _peer_select(sc_ref, thr_scr, pw_scr, q_scr, lt):
    assert PEER_HEADS == SUBLANES
    sub = lax.broadcasted_iota(jnp.int32, (SUBLANES, LANES), 0)
    a = b = None
    for h in range(PEER_HEADS):
        a_h = _top16_desc(sc_ref[lt, 2 * h])
        b_h = _top16_desc(sc_ref[lt, 2 * h + 1])
        a = a_h if h == 0 else [jnp.where(sub == h, new, old) for new, old in zip(a_h, a)]
        b = b_h if h == 0 else [jnp.where(sub == h, new, old) for new, old in zip(b_h, b)]
    lists = [[a[r] + b[c] for c in range(PEER_TOPK // (r + 1))] for r in range(SUBLANES)]
    lists.append([a[r] + b[0] for r in range(SUBLANES, PEER_TOPK)])
    top = lists[0]
    for other in lists[1:-1]:
        top = _bitonic_to_sorted(_merge_bitonic(top, other))
    z = _merge_bitonic(top, lists[-1])
    tau = functools.reduce(jnp.minimum, z)
    best = a[0] + b[0]
    zsum = jnp.zeros_like(tau)
    inf = jnp.full_like(tau, jnp.inf)
    thr_rank = []
    for r, cand in enumerate(lists):
        hits = [v >= tau for v in cand]
        for v, hit in zip(cand, hits):
            zsum = zsum + jnp.where(hit, jnp.exp(v - best), 0.0)
        if r < SUBLANES:
            t = inf
            for c, hit in enumerate(hits):
                t = jnp.where(hit, b[c], t)
            thr_rank.append(t)
        else:
            thr_rank.extend(jnp.where(hit, b[0], inf) for hit in hits)
    inv = 0.5 / zsum
    for h in range(PEER_HEADS):
        own = lambda v: jnp.broadcast_to(v[h:h + 1, :], (SUBLANES, LANES))
        a_h = [own(v) for v in a]
        thr_h = [own(v) for v in thr_rank]
        inv_h, b0_h = own(inv), own(b[0])
        s0 = sc_ref[lt, 2 * h]
        s1 = sc_ref[lt, 2 * h + 1]
        for g in range(s0.shape[0] // SUBLANES):
            rows = slice(g * SUBLANES, (g + 1) * SUBLANES)
            x0 = s0[rows, :]
            thr = jnp.full_like(x0, jnp.inf)
            for r in range(PEER_TOPK):
                thr = jnp.where(x0 == a_h[r], thr_h[r], thr)
            thr_scr[lt, h, rows, :] = thr
            pw_scr[lt, h, rows, :] = jnp.exp(x0 - a_h[0]) * inv_h
            q_scr[lt, h, rows, :] = jnp.exp(s1[rows, :] - b0_h)


MXU_TILE = 256
MXU_COUNT = 2
ACC_ROWS = 512
ACC_PIECE = 32
ACC_PRE = 0
ACC_OUT = ACC_ROWS // 4


def _peer_dense_kernel(sc_ref, hnt_ref, u_ref, vt_ref, res_ref, nw_ref, y_ref, thr_scr, pw_scr, q_scr,
                       h0_scr, h1_scr, g0_scr, g1_scr, yt_ref):
    s = pl.program_id(1)
    n_e = pl.num_programs(1) - 2
    d, tb = hnt_ref.shape
    n_lt, _, nk, _ = sc_ref.shape
    eb = u_ref.shape[0]
    n_i = eb // nk
    assert tb == MXU_COUNT * MXU_TILE and eb % ACC_ROWS == 0 and d % ACC_ROWS == 0

    @pl.when(s == 0)
    def _select():
        def body(lt, carry):
            _peer_select(sc_ref, thr_scr, pw_scr, q_scr, lt)
            return carry
        lax.fori_loop(0, n_lt, body, 0)
        yt_ref[...] = jnp.zeros_like(yt_ref)
        g0_scr[...] = jnp.zeros_like(g0_scr)
        g1_scr[...] = jnp.zeros_like(g1_scr)

    i0 = pl.multiple_of(jnp.clip(s - 1, 0, n_e - 1) * n_i, SUBLANES)

    def gate_steps(lt, ii, h_r, g_w):
        lanes = slice(lt * LANES, (lt + 1) * LANES)
        rows = slice(ii * nk, (ii + 1) * nk)
        tiled = (nk // SUBLANES, SUBLANES, LANES)
        state = {"w": jnp.zeros(tiled, F32)}

        def head(h):
            thr = thr_scr[lt, h, pl.ds(i0 + ii, SUBLANES, stride=0), :]
            pw = pw_scr[lt, h, pl.ds(i0 + ii, SUBLANES, stride=0), :]
            s1 = sc_ref[lt, 2 * h + 1].reshape(tiled)
            state["w"] = state["w"] + jnp.where(s1 >= thr[None], q_scr[lt, h].reshape(tiled), 0.0) * pw[None]
            if h == PEER_HEADS - 1:
                x = h_r[rows, lanes]
                act = x + x * lax.erf(x * (2.0 ** -0.5))
                g_w[rows, lanes] = (state["w"].reshape(nk, LANES) * act).astype(g_w.dtype)

        return [functools.partial(head, h) for h in range(PEER_HEADS)]

    def mxu_steps(kind, c, k, reg, h_w, g_r):
        lhs_ref, rhs_ref, acc, n_k = ((u_ref, hnt_ref, ACC_PRE, d // MXU_TILE) if kind == "pre"
                                      else (vt_ref, g_r, ACC_OUT, eb // MXU_TILE))
        kc = slice(k * MXU_TILE, (k + 1) * MXU_TILE)

        def push():
            for q in range(MXU_COUNT):
                pltpu.matmul_push_rhs(rhs_ref[kc, q * MXU_TILE:(q + 1) * MXU_TILE], staging_register=reg,
                                      mxu_index=q)

        def piece(p):
            r0 = c * ACC_ROWS + p * ACC_PIECE
            lhs = lhs_ref[r0:r0 + ACC_PIECE, kc]
            for q in range(MXU_COUNT):
                pltpu.matmul_acc_lhs(acc + p * ACC_PIECE // 4, lhs, q, load_staged_rhs=reg if p == 0 else None)

        def pop(p):
            rows = slice(c * ACC_ROWS + p * ACC_PIECE, c * ACC_ROWS + (p + 1) * ACC_PIECE)
            for q in range(MXU_COUNT):
                cols = slice(q * MXU_TILE, (q + 1) * MXU_TILE)
                res = pltpu.matmul_pop(acc + p * ACC_PIECE // 4, (ACC_PIECE, MXU_TILE), F32, q)
                if kind == "pre":
                    h_w[rows, cols] = res
                else:
                    yt_ref[rows, cols] += res

        n_p = ACC_ROWS // ACC_PIECE
        pops = [functools.partial(pop, p) for p in range(n_p)] if k == n_k - 1 else []
        return push, [functools.partial(piece, p) for p in range(n_p)], pops

    def stage(h_w, h_r, g_w, g_r, pre, gating, out):
        pre_groups = [("pre", c, k) for c in range(eb // ACC_ROWS) for k in range(d // MXU_TILE)] if pre else []
        out_groups = [("out", c, k) for c in range(d // ACC_ROWS) for k in range(eb // MXU_TILE)] if out else []
        if pre and out:
            order = [g for pair in zip(pre_groups, out_groups) for g in pair]
        else:
            order = pre_groups + out_groups
        steps = [mxu_steps(kind, c, k, gi % 2, h_w, g_r) for gi, (kind, c, k) in enumerate(order)]
        mxu = [steps[0][0]]
        lagged = []
        for gi, (_, pieces, pops) in enumerate(steps):
            half = len(pieces) // 2
            for p, piece in enumerate(pieces):
                if p == half and gi + 1 < len(steps):
                    mxu.append(steps[gi + 1][0])
                mxu.append(piece)
                if lagged:
                    mxu.append(lagged.pop(0))
            same_acc_next = gi + 1 < len(steps) and order[gi + 1][0] == order[gi][0]
            if same_acc_next or gi + 1 == len(steps):
                mxu.extend(pops)
            else:
                lagged = list(pops)
        vpu = [t for lt in range(n_lt) for ii in range(n_i) for t in gate_steps(lt, ii, h_r, g_w)] if gating else []
        im = iv = 0
        while im < len(mxu) or iv < len(vpu):
            if iv >= len(vpu) or (im < len(mxu) and im * len(vpu) <= iv * len(mxu)):
                mxu[im]()
                im += 1
            else:
                vpu[iv]()
                iv += 1

    last = n_e + 1

    @pl.when(s == 0)
    def _first():
        stage(h0_scr, None, None, None, True, False, False)

    @pl.when((s > 0) & (s < last) & (s % 2 == 0))
    def _even():
        stage(h0_scr, h1_scr, g1_scr, g0_scr, True, True, True)

    @pl.when((s > 0) & (s < last) & (s % 2 == 1))
    def _odd():
        stage(h1_scr, h0_scr, g0_scr, g1_scr, True, True, True)

    def finish(g_r):
        stage(None, None, None, g_r, False, False, True)
        y = res_ref[...] + yt_ref[...].T
        y_ref[...] = y * lax.rsqrt(jnp.mean(y * y, axis=-1, keepdims=True) + EPS) * nw_ref[...]

    @pl.when((s == last) & (s % 2 == 0))
    def _last_even():
        finish(g0_scr)

    @pl.when((s == last) & (s % 2 == 1))
    def _last_odd():
        finish(g1_scr)


def _peer_dense(sc, hnt, u_bf16, vt_bf16, resid, norm_w, tb, eb):
    _, nhc, nk, _ = sc.shape
    d, t = hnt.shape
    n_exp = u_bf16.shape[0]
    n_lt = tb // LANES
    assert eb == SUBLANES * nk and n_exp == nk * nk and t % tb == 0 and tb % LANES == 0
    sel = pltpu.VMEM((n_lt, PEER_HEADS, nk, LANES), F32)
    pre = pltpu.VMEM((eb, tb + LANES), F32)
    gated = pltpu.VMEM((eb, tb), BF16)
    n_e = n_exp // eb
    return pl.pallas_call(
        _peer_dense_kernel,
        grid=(t // tb, n_e + 2),
        in_specs=[pl.BlockSpec((n_lt, nhc, nk, LANES), lambda i, s: (i, 0, 0, 0)),
                  pl.BlockSpec((d, tb), lambda i, s: (0, i)),
                  pl.BlockSpec((eb, d), lambda i, s: (jnp.minimum(s, n_e - 1), 0)),
                  pl.BlockSpec((d, eb), lambda i, s: (0, jnp.clip(s - 2, 0, n_e - 1))),
                  pl.BlockSpec((tb, d), lambda i, s: (i, 0)),
                  pl.BlockSpec((1, d), lambda i, s: (0, 0))],
        out_specs=pl.BlockSpec((tb, d), lambda i, s: (i, 0)),
        out_shape=jax.ShapeDtypeStruct((t, d), F32),
        scratch_shapes=[sel, sel, sel, pre, pre, gated, gated, pltpu.VMEM((d, tb), F32)],
        compiler_params=_cparams("parallel", "arbitrary"),
        name="peer_dense",
    )(sc, hnt, u_bf16, vt_bf16, resid, norm_w.reshape(1, d))


TOKEN_BLOCK = 512
SWA_BLOCKS_PER_STEP = 4
EXPERT_BLOCK = 1024
SAMPLE_T_PAD = 16


def _ffn(x2d, attn, oh, w, tb):
    h, hnt, sc = _out_proj_scores(x2d, attn, oh, w['wo'], w['norm_ffn'], w['wqt'], w['keys'], tb)
    return _peer_dense(sc, hnt, w['u'], w['vt'], h, w['norm_final'], tb, EXPERT_BLOCK)


def kernel(x_prompt, x_sample, cache_k_win, cache_v_win, state_hgrn, norm_mix_w, w_in, attn_sinks,
           rel_bias_table, hg_lb, hg_norm_w, w_o, norm_ffn_w, peer_w_q, peer_sub_keys, peer_u, peer_v,
           norm_final_w):
    bsz, seq, d = x_prompt.shape
    dbsz, dseq, _ = x_sample.shape
    aw = ATTN_HEADS * HEAD_DIM
    kw = ATTN_KV_HEADS * HEAD_DIM
    hw = hg_norm_w.shape[1]
    wb = cache_k_win.shape[2]

    wi = w_in[0]
    w_in_r = jnp.concatenate([wi[:, :aw], wi[:, aw + 2 * kw:], wi[:, aw:aw + 2 * kw]], axis=1).astype(BF16)
    col_k = (aw + 4 * hw) // kw
    col_v = col_k + 1
    hg_cols = (1, 2, 3, 4)
    lb = jax.nn.softmax(hg_lb.astype(F32), axis=0)[0]
    nhc = PEER_HEADS * 2
    w = {
        'wo': w_o[0].astype(BF16),
        'norm_ffn': norm_ffn_w[0],
        'wqt': peer_w_q[0].astype(BF16).T,
        'keys': peer_sub_keys[0].reshape(nhc, peer_sub_keys.shape[3], peer_sub_keys.shape[4]).astype(BF16),
        'u': peer_u[0].astype(BF16),
        'vt': peer_v[0].astype(BF16).T,
        'norm_final': norm_final_w,
    }
    sinks = attn_sinks[0].astype(F32)

    proj_p = _in_proj(x_prompt.reshape(bsz * seq, d), norm_mix_w[0], w_in_r, TOKEN_BLOCK)
    proj_p3 = proj_p.reshape(bsz, seq, -1)
    blk = WINDOW
    dist_p = (jnp.arange(blk)[:, None] + blk) - jnp.arange(2 * blk)[None, :]
    in_win = (dist_p >= 0) & (dist_p <= WINDOW)
    has_prev = jnp.arange(2 * blk)[None, :] >= blk
    bias_p = jnp.stack([_masked_bias(rel_bias_table, blk, 2 * blk, blk, in_win & has_prev),
                        _masked_bias(rel_bias_table, blk, 2 * blk, blk, in_win)])
    attn_p = _swa_prompt(proj_p3, sinks, bias_p, 0, col_k, col_v)
    oh_p, st_p = _hgrn(proj_p3, lb, hg_norm_w[0], None, hg_cols, HG_CHUNK, HG_CHUNK, bsz)
    y_p = _ffn(x_prompt.reshape(bsz * seq, d), attn_p.reshape(bsz * seq, aw),
               oh_p.reshape(bsz * seq, hw), w, TOKEN_BLOCK)
    k_off = aw + 4 * hw
    wp = min(WINDOW, seq)
    k_win_p = proj_p3[:, seq - wp:, k_off:k_off + kw].reshape(1, bsz, wp, ATTN_KV_HEADS, HEAD_DIM)
    v_win_p = proj_p3[:, seq - wp:, k_off + kw:k_off + 2 * kw].reshape(1, bsz, wp, ATTN_KV_HEADS, HEAD_DIM)

    tp = SAMPLE_T_PAD
    xs_pad = jnp.pad(x_sample, ((0, 0), (0, tp - dseq), (0, 0)))
    proj_s3 = _in_proj(xs_pad.reshape(dbsz * tp, d), norm_mix_w[0], w_in_r, TOKEN_BLOCK).reshape(dbsz, tp, -1)
    k_new = proj_s3[:, :dseq, k_off:k_off + kw]
    v_new = proj_s3[:, :dseq, k_off + kw:k_off + 2 * kw]
    k_cache = cache_k_win[0].reshape(dbsz, wb, kw)
    v_cache = cache_v_win[0].reshape(dbsz, wb, kw)
    kk = jnp.concatenate([k_cache, k_new], axis=1)
    vv = jnp.concatenate([v_cache, v_new], axis=1)
    q_pos = jnp.arange(tp)[:, None]
    k_pos = jnp.arange(wb + tp)[None, :]
    dist_s = wb + q_pos - k_pos
    mask_s = (dist_s >= 0) & (dist_s <= WINDOW) & (q_pos < dseq) & (k_pos < wb + dseq)
    bias_s = _masked_bias(rel_bias_table, tp, wb + tp, wb, mask_s)
    attn_s = _swa_sample(proj_s3, k_cache, v_cache, sinks, bias_s[:, :, :wb], bias_s[:, :, wb:],
                         0, col_k, col_v, 16)
    oh_s, st_s = _hgrn(proj_s3, lb, hg_norm_w[0], state_hgrn[0], hg_cols, tp, dseq, 8)
    y_s = _ffn(x_sample.reshape(dbsz * dseq, d), attn_s[:, :dseq].reshape(dbsz * dseq, aw),
               oh_s[:, :dseq].reshape(dbsz * dseq, hw), w, TOKEN_BLOCK)
    k_win_s = kk[:, dseq:].reshape(1, dbsz, wb, ATTN_KV_HEADS, HEAD_DIM)
    v_win_s = vv[:, dseq:].reshape(1, dbsz, wb, ATTN_KV_HEADS, HEAD_DIM)

    return (y_p.reshape(bsz, seq, d), y_s.reshape(dbsz, dseq, d), k_win_p, v_win_p, st_p[None],
            k_win_s, v_win_s, st_s[None])
```

```python
import functools
import math

import jax
import jax.numpy as jnp
from jax import lax
from jax.experimental import pallas as pl
from jax.experimental.pallas import tpu as pltpu

F32 = jnp.float32
BF16 = jnp.bfloat16

EPS = 1e-6
NEG = -1e30

ATTN_HEADS = 8
ATTN_KV_HEADS = 2
HEAD_DIM = 64
WINDOW = 128
REL_BUCKETS = 32
HG_HEADS = 4
HG_CHUNK = 64
PEER_HEADS = 8
PEER_TOPK = 16

LANES = 128
SUBLANES = 8
VMEM_LIMIT = 56 * 1024 * 1024


def _cparams(*sem):
    return pltpu.CompilerParams(dimension_semantics=sem, vmem_limit_bytes=VMEM_LIMIT)


def _nt(a, b):
    return lax.dot_general(a, b, (((1,), (1,)), ((), ())), preferred_element_type=F32)


def _tn(a, b):
    return lax.dot_general(a, b, (((0,), (0,)), ((), ())), preferred_element_type=F32)


def _dot(a, b):
    return jnp.dot(a, b, preferred_element_type=F32)


def _sigmoid(x):
    return 1.0 / (1.0 + jnp.exp(-x))


def _in_proj_kernel(x_ref, nw_ref, w_ref, o_ref):
    x = x_ref[...]
    xn = x * lax.rsqrt(jnp.mean(x * x, axis=-1, keepdims=True) + EPS) * nw_ref[...]
    o_ref[...] = _dot(xn.astype(BF16), w_ref[...])


def _in_proj(x2d, norm_w, w_bf16, tm):
    t, d = x2d.shape
    n = w_bf16.shape[1]
    return pl.pallas_call(
        _in_proj_kernel,
        grid=(t // tm,),
        in_specs=[pl.BlockSpec((tm, d), lambda i: (i, 0)),
                  pl.BlockSpec((1, d), lambda i: (0, 0)),
                  pl.BlockSpec((d, n), lambda i: (0, 0))],
        out_specs=pl.BlockSpec((tm, n), lambda i: (i, 0)),
        out_shape=jax.ShapeDtypeStruct((t, n), F32),
        compiler_params=_cparams("parallel"),
        name="in_proj",
    )(x2d, norm_w.reshape(1, d), w_bf16)


def _t5_bucket(dist):
    n = jnp.maximum(dist, 0)
    max_exact = REL_BUCKETS // 2
    nf = jnp.maximum(n, 1).astype(F32)
    large = max_exact + (jnp.log(nf / max_exact) / math.log(WINDOW / max_exact)
                         * (REL_BUCKETS - max_exact)).astype(jnp.int32)
    large = jnp.minimum(large, REL_BUCKETS - 1)
    return jnp.where(n < max_exact, n, large)


def _masked_bias(table, n_q, n_k, offset, mask):
    h = table.shape[1]
    diag = jnp.arange(n_q + n_k - 1) - (n_k - 1) + offset
    per_diag = table.astype(F32)[_t5_bucket(diag)].T
    w = jnp.pad(per_diag[:, ::-1], ((0, 0), (0, 1)))
    p = n_q + n_k
    skew = jnp.tile(w, (1, n_q))[:, :n_q * (p - 1)].reshape(h, n_q, p - 1)
    return jnp.where(mask[None], skew[:, :, n_q - 1:n_q - 1 + n_k], NEG)


def _swa_prompt_kernel(sink_ref, q_ref, kp_ref, kc_ref, vp_ref, vc_ref, bias0_ref, bias1_ref, o_ref):
    scale = HEAD_DIM ** -0.5
    assert math.frexp(scale)[0] == 0.5
    group = ATTN_HEADS // ATTN_KV_HEADS
    blk = kp_ref.shape[1]
    assert 2 * HEAD_DIM == LANES and kp_ref.shape[2] == LANES and ATTN_KV_HEADS == 2 and group % 2 == 0
    n_sub = q_ref.shape[1] // blk
    kk = [jnp.concatenate([kp_ref[0], kc_ref[0, :blk]], axis=0)]
    vv = [jnp.concatenate([vp_ref[0], vc_ref[0, :blk]], axis=0)]
    kk += [kc_ref[0, (j - 1) * blk:(j + 1) * blk] for j in range(1, n_sub)]
    vv += [vc_ref[0, (j - 1) * blk:(j + 1) * blk] for j in range(1, n_sub)]
    bias = [bias0_ref] + [bias1_ref] * (n_sub - 1)
    low = lax.broadcasted_iota(jnp.int32, kk[0].shape, 1) < HEAD_DIM

    def halves(x, kvh):
        own = jnp.where(low if kvh == 0 else ~low, x, 0.0)
        other = pltpu.roll(own, HEAD_DIM, axis=1)
        lo, hi = (own, other) if kvh == 0 else (other, own)
        return lo.astype(BF16), hi.astype(BF16)

    k_half = [[halves(kk[j], kvh) for kvh in range(ATTN_KV_HEADS)] for j in range(n_sub)]
    v_half = [[halves(vv[j], kvh) for kvh in range(ATTN_KV_HEADS)] for j in range(n_sub)]
    units = [(j, h) for j in range(n_sub) for h in range(ATTN_HEADS)]
    kv_of = lambda h: h // group
    qt = [[(q_ref[0, j * blk:(j + 1) * blk, t * LANES:(t + 1) * LANES] * scale).astype(BF16)
           for t in range(ATTN_HEADS // 2)] for j in range(n_sub)]
    s = {(j, h): _nt(qt[j][h // 2], k_half[j][kv_of(h)][h % 2]) + bias[j][0, h] for j, h in units}
    m = {u: jnp.maximum(jnp.max(s[u], axis=-1, keepdims=True), sink_ref[u[1]]) for u in units}
    p = {u: jnp.exp(s[u] - m[u]) for u in units}
    den = {u: jnp.sum(p[u], axis=-1, keepdims=True) + jnp.exp(sink_ref[u[1]] - m[u]) for u in units}
    p = {u: (p[u] * (1.0 / den[u])).astype(BF16) for u in units}
    for j in range(n_sub):
        for tile in range(ATTN_HEADS // 2):
            kvh = kv_of(2 * tile)
            o = _dot(p[j, 2 * tile], v_half[j][kvh][0]) + _dot(p[j, 2 * tile + 1], v_half[j][kvh][1])
            o_ref[0, j * blk:(j + 1) * blk, tile * LANES:(tile + 1) * LANES] = o.astype(o_ref.dtype)


def _swa_prompt(proj3, sinks, bias, col_q, col_k, col_v):
    bsz, seq, _ = proj3.shape
    blk = WINDOW
    aw = ATTN_HEADS * HEAD_DIM
    kw = ATTN_KV_HEADS * HEAD_DIM
    per = SWA_BLOCKS_PER_STEP
    prev = lambda b, n: (b, jnp.maximum(per * n - 1, 0))
    bias_spec = lambda pick: pl.BlockSpec((1, ATTN_HEADS, blk, 2 * blk), lambda b, n: (pick(n), 0, 0, 0))
    return pl.pallas_call(
        _swa_prompt_kernel,
        grid=(bsz, seq // (per * blk)),
        in_specs=[pl.BlockSpec(memory_space=pltpu.SMEM),
                  pl.BlockSpec((1, per * blk, aw), lambda b, n: (b, n, col_q)),
                  pl.BlockSpec((1, blk, kw), lambda b, n: prev(b, n) + (col_k,)),
                  pl.BlockSpec((1, per * blk, kw), lambda b, n: (b, n, col_k)),
                  pl.BlockSpec((1, blk, kw), lambda b, n: prev(b, n) + (col_v,)),
                  pl.BlockSpec((1, per * blk, kw), lambda b, n: (b, n, col_v)),
                  bias_spec(lambda n: jnp.minimum(n, 1)),
                  bias_spec(lambda n: 1)],
        out_specs=pl.BlockSpec((1, per * blk, aw), lambda b, n: (b, n, 0)),
        out_shape=jax.ShapeDtypeStruct((bsz, seq, aw), BF16),
        compiler_params=_cparams("parallel", "arbitrary"),
        name="swa_prompt",
    )(sinks, proj3, proj3, proj3, proj3, proj3, bias, bias)


def _swa_sample_kernel(sink_ref, q_ref, kn_ref, vn_ref, kc_ref, vc_ref, bias_c_ref, bias_n_ref, o_ref):
    scale = HEAD_DIM ** -0.5
    group = ATTN_HEADS // ATTN_KV_HEADS
    heads = range(ATTN_HEADS)
    kv = [slice((h // group) * HEAD_DIM, (h // group + 1) * HEAD_DIM) for h in heads]
    dot_qk = lambda a, b: jnp.einsum('bqd,bkd->bqk', a, b, preferred_element_type=F32)
    dot_pv = lambda a, b: jnp.einsum('bqk,bkd->bqd', a, b, preferred_element_type=F32)
    qh = [q_ref[:, :, h * HEAD_DIM:(h + 1) * HEAD_DIM].astype(BF16) for h in heads]
    sc = [dot_qk(qh[h], kc_ref[:, :, kv[h]].astype(BF16)) * scale + bias_c_ref[h][None] for h in heads]
    sn = [dot_qk(qh[h], kn_ref[:, :, kv[h]].astype(BF16)) * scale + bias_n_ref[h][None] for h in heads]
    m = [jnp.maximum(jnp.maximum(jnp.max(sc[h], axis=-1, keepdims=True),
                                 jnp.max(sn[h], axis=-1, keepdims=True)), sink_ref[h]) for h in heads]
    pc = [jnp.exp(sc[h] - m[h]) for h in heads]
    pn = [jnp.exp(sn[h] - m[h]) for h in heads]
    den = [jnp.sum(pc[h], axis=-1, keepdims=True) + jnp.sum(pn[h], axis=-1, keepdims=True)
           + jnp.exp(sink_ref[h] - m[h]) for h in heads]
    for h in heads:
        o = (dot_pv(pc[h].astype(BF16), vc_ref[:, :, kv[h]].astype(BF16))
             + dot_pv(pn[h].astype(BF16), vn_ref[:, :, kv[h]].astype(BF16))) / den[h]
        o_ref[:, :, h * HEAD_DIM:(h + 1) * HEAD_DIM] = o.astype(o_ref.dtype)


def _swa_sample(proj3, k_cache, v_cache, sinks, bias_c, bias_n, col_q, col_k, col_v, bb):
    bsz, tp, _ = proj3.shape
    wb = k_cache.shape[1]
    aw = ATTN_HEADS * HEAD_DIM
    kw = ATTN_KV_HEADS * HEAD_DIM
    return pl.pallas_call(
        _swa_sample_kernel,
        grid=(bsz // bb,),
        in_specs=[pl.BlockSpec(memory_space=pltpu.SMEM),
                  pl.BlockSpec((bb, tp, aw), lambda b: (b, 0, col_q)),
                  pl.BlockSpec((bb, tp, kw), lambda b: (b, 0, col_k)),
                  pl.BlockSpec((bb, tp, kw), lambda b: (b, 0, col_v)),
                  pl.BlockSpec((bb, wb, kw), lambda b: (b, 0, 0)),
                  pl.BlockSpec((bb, wb, kw), lambda b: (b, 0, 0)),
                  pl.BlockSpec((ATTN_HEADS, tp, wb), lambda b: (0, 0, 0)),
                  pl.BlockSpec((ATTN_HEADS, tp, tp), lambda b: (0, 0, 0))],
        out_specs=pl.BlockSpec((bb, tp, aw), lambda b: (b, 0, 0)),
        out_shape=jax.ShapeDtypeStruct((bsz, tp, aw), BF16),
        compiler_params=_cparams("parallel"),
        name="swa_sample",
    )(sinks, proj3, proj3, proj3, k_cache, v_cache, bias_c, bias_n)


def _split3(x):
    hi = x.astype(BF16)
    r = x - hi.astype(F32)
    mid = r.astype(BF16)
    lo = (r - mid.astype(F32)).astype(BF16)
    return hi, mid, lo


def _hgrn_kernel(*refs, t_valid, has_state):
    if has_state:
        q_ref, f_ref, i_ref, g_ref, lb_ref, nw_ref, s0_ref, o_ref, s_ref, st_scr = refs
    else:
        q_ref, f_ref, i_ref, g_ref, lb_ref, nw_ref, o_ref, s_ref, st_scr = refs
    bb, chunk, width = q_ref.shape
    dk = width // HG_HEADS
    c = pl.program_id(1)

    @pl.when(c == 0)
    def _init():
        if has_state:
            st_scr[...] = s0_ref[...]
        else:
            st_scr[...] = jnp.zeros_like(st_scr)

    row = lax.broadcasted_iota(jnp.int32, (chunk, chunk), 0)
    col = lax.broadcasted_iota(jnp.int32, (chunk, chunk), 1)
    causal = row >= col
    tri = jnp.where(causal, 1.0, 0.0).astype(BF16)
    row_w = lax.broadcasted_iota(jnp.int32, (chunk, width), 0)
    valid = row_w < t_valid
    ones_rows = jnp.where(lax.broadcasted_iota(jnp.int32, (chunk, dk), 0) < 3, 1.0, 0.0).astype(BF16)
    mid_row = chunk // 2

    def body(b, carry):
        qx = q_ref[b]
        q = qx * _sigmoid(qx)
        lb = lb_ref[...]
        f = lb + (1.0 - lb) * _sigmoid(f_ref[b])
        k = 1.0 - f
        lg = jnp.log(f)
        if t_valid < chunk:
            k = jnp.where(valid, k, 0.0)
            lg = jnp.where(valid, lg, 0.0)
        v = i_ref[b].astype(BF16)
        cum = sum(_dot(tri, part) for part in _split3(lg))
        cum_mid = cum[mid_row:mid_row + 1, :]
        cum_last = cum[chunk - 1:chunk, :]
        qt = (q * jnp.exp(cum - cum_mid)).astype(BF16)
        kt = (k * jnp.exp(cum_mid - cum)).astype(BF16)
        qe = (q * jnp.exp(cum)).astype(BF16)
        kd = (k * jnp.exp(cum_last - cum)).astype(BF16)
        dec_rows = jnp.zeros((chunk, width), F32)
        for j, part in enumerate(_split3(jnp.exp(cum_last))):
            dec_rows = jnp.where(row_w == j, part.astype(F32), dec_rows)
        dec_rows = dec_rows.astype(BF16)
        heads = range(HG_HEADS)
        sl = [slice(h * dk, (h + 1) * dk) for h in heads]
        st = [st_scr[b, h] for h in heads]
        a = [_nt(qt[:, sl[h]], kt[:, sl[h]]) for h in heads]
        inter = [_dot(qe[:, sl[h]], st[h].astype(BF16)) for h in heads]
        upd = [_tn(kd[:, sl[h]], v[:, sl[h]]) for h in heads]
        decay = [_tn(dec_rows[:, sl[h]], ones_rows) for h in heads]
        for h in heads:
            st_scr[b, h] = st[h] * decay[h] + upd[h]
        a = [jnp.where(causal, a[h], 0.0).astype(BF16) for h in heads]
        o = [_dot(a[h], v[:, sl[h]]) + inter[h] for h in heads]
        outs = [o[h] * lax.rsqrt(jnp.mean(o[h] * o[h], axis=-1, keepdims=True) + EPS) for h in heads]
        gx = g_ref[b]
        o = jnp.concatenate(outs, axis=1) * nw_ref[...] * (gx * _sigmoid(gx))
        o_ref[b] = o.astype(o_ref.dtype)
        return carry

    lax.fori_loop(0, bb, body, 0, unroll=8)

    @pl.when(c == pl.num_programs(1) - 1)
    def _final():
        s_ref[...] = st_scr[...]


def _hgrn(proj3, lb, norm_w, s0, cols, chunk, t_valid, bb):
    bsz, t, _ = proj3.shape
    width = lb.shape[0]
    dk = width // HG_HEADS
    has_state = s0 is not None
    spec = lambda cb: pl.BlockSpec((bb, chunk, width), lambda b, c: (b, c, cb))
    vec = pl.BlockSpec((1, width), lambda b, c: (0, 0))
    st_spec = pl.BlockSpec((bb, HG_HEADS, dk, dk), lambda b, c: (b, 0, 0, 0))
    in_specs = [spec(cols[0]), spec(cols[1]), spec(cols[2]), spec(cols[3]), vec, vec]
    args = [proj3, proj3, proj3, proj3, lb.reshape(1, width), norm_w.reshape(1, width)]
    if has_state:
        in_specs.append(st_spec)
        args.append(s0)
    return pl.pallas_call(
        functools.partial(_hgrn_kernel, t_valid=t_valid, has_state=has_state),
        grid=(bsz // bb, t // chunk),
        in_specs=in_specs,
        out_specs=[pl.BlockSpec((bb, chunk, width), lambda b, c: (b, c, 0)), st_spec],
        out_shape=[jax.ShapeDtypeStruct((bsz, t, width), BF16),
                   jax.ShapeDtypeStruct((bsz, HG_HEADS, dk, dk), F32)],
        scratch_shapes=[pltpu.VMEM((bb, HG_HEADS, dk, dk), F32)],
        compiler_params=_cparams("parallel", "arbitrary"),
        name="hgrn_state" if has_state else "hgrn_prompt",
    )(*args)


def _out_proj_scores_kernel(x_ref, a_ref, oh_ref, wo_ref, nw_ref, wqt_ref, keys_ref, h_ref, hnt_ref, sc_ref):
    aw = a_ref.shape[1]
    mix = _dot(a_ref[...], wo_ref[:aw, :]) + _dot(oh_ref[...], wo_ref[aw:, :])
    h = x_ref[...] + mix
    h_ref[...] = h
    hn = h * lax.rsqrt(jnp.mean(h * h, axis=-1, keepdims=True) + EPS) * nw_ref[...]
    hnt = hn.T.astype(hnt_ref.dtype)
    hnt_ref[...] = hnt
    qt = _dot(wqt_ref[...], hnt)
    half = keys_ref.shape[2]
    for hc in range(keys_ref.shape[0]):
        s = _dot(keys_ref[hc], qt[hc * half:(hc + 1) * half, :].astype(BF16))
        for lt in range(sc_ref.shape[0]):
            sc_ref[lt, hc] = s[:, lt * LANES:(lt + 1) * LANES]


def _out_proj_scores(x2d, attn, oh, wo_bf16, norm_w, wqt_bf16, keys_bf16, tm):
    t, d = x2d.shape
    aw, hw = attn.shape[1], oh.shape[1]
    nhc, nk, half = keys_bf16.shape
    return pl.pallas_call(
        _out_proj_scores_kernel,
        grid=(t // tm,),
        in_specs=[pl.BlockSpec((tm, d), lambda i: (i, 0)),
                  pl.BlockSpec((tm, aw), lambda i: (i, 0)),
                  pl.BlockSpec((tm, hw), lambda i: (i, 0)),
                  pl.BlockSpec((aw + hw, d), lambda i: (0, 0)),
                  pl.BlockSpec((1, d), lambda i: (0, 0)),
                  pl.BlockSpec((nhc * half, d), lambda i: (0, 0)),
                  pl.BlockSpec((nhc, nk, half), lambda i: (0, 0, 0))],
        out_specs=[pl.BlockSpec((tm, d), lambda i: (i, 0)),
                   pl.BlockSpec((d, tm), lambda i: (0, i)),
                   pl.BlockSpec((tm // LANES, nhc, nk, LANES), lambda i: (i, 0, 0, 0))],
        out_shape=[jax.ShapeDtypeStruct((t, d), F32),
                   jax.ShapeDtypeStruct((d, t), BF16),
                   jax.ShapeDtypeStruct((t // LANES, nhc, nk, LANES), F32)],
        compiler_params=_cparams("parallel"),
        name="out_proj_scores",
    )(x2d, attn, oh, wo_bf16, norm_w.reshape(1, d), wqt_bf16, keys_bf16)


def _sort16_pairs():
    def merge(lo, hi, r):
        step = r * 2
        if step < hi - lo:
            yield from merge(lo, hi, step)
            yield from merge(lo + r, hi, step)
            yield from [(i, i + r) for i in range(lo + r, hi - r, step)]
        else:
            yield (lo, lo + r)

    def sort(lo, hi):
        if hi - lo >= 1:
            mid = lo + (hi - lo) // 2
            yield from sort(lo, mid)
            yield from sort(mid + 1, hi)
            yield from merge(lo, hi, 1)

    return tuple(sort(0, PEER_TOPK - 1))


_SORT16 = _sort16_pairs()


def _bitonic_to_sorted(z):
    z = list(z)
    d = PEER_TOPK // 2
    while d >= 1:
        for i in range(PEER_TOPK):
            if i & d == 0:
                hi, lo = jnp.maximum(z[i], z[i + d]), jnp.minimum(z[i], z[i + d])
                z[i], z[i + d] = hi, lo
        d //= 2
    return z


def _merge_bitonic(top, other):
    z = list(top)
    m = len(other)
    for r in range(PEER_TOPK - m, PEER_TOPK):
        z[r] = jnp.maximum(top[r], other[PEER_TOPK - 1 - r])
    return z


def _top16_desc(x):
    n = x.shape[0] // SUBLANES
    xs = [x[g * SUBLANES:(g + 1) * SUBLANES, :] for g in range(n)]
    for i, j in _SORT16:
        xs[i], xs[j] = jnp.maximum(xs[i], xs[j]), jnp.minimum(xs[i], xs[j])
    shift = SUBLANES // 2
    while shift >= 1:
        ys = [pltpu.roll(v, shift, axis=0) for v in xs]
        xs = _bitonic_to_sorted(_merge_bitonic(xs, ys))
        shift //= 2
    return xs


def _peer_select(sc_ref, thr_scr, pw_scr, q_scr, lt):
    assert PEER_HEADS == SUBLANES
    sub = lax.broadcasted_iota(jnp.int32, (SUBLANES, LANES), 0)
    a = b = None
    for h in range(PEER_HEADS):
        a_h = _top16_desc(sc_ref[lt, 2 * h])
        b_h = _top16_desc(sc_ref[lt, 2 * h + 1])
        a = a_h if h == 0 else [jnp.where(sub == h, new, old) for new, old in zip(a_h, a)]
        b = b_h if h == 0 else [jnp.where(sub == h, new, old) for new, old in zip(b_h, b)]
    lists = [[a[r] + b[c] for c in range(PEER_TOPK // (r + 1))] for r in range(SUBLANES)]
    lists.append([a[r] + b[0] for r in range(SUBLANES, PEER_TOPK)])
    top = lists[0]
    for other in lists[1:-1]:
        top = _bitonic_to_sorted(_merge_bitonic(top, other))
    z = _merge_bitonic(top, lists[-1])
    tau = functools.reduce(jnp.minimum, z)
    best = a[0] + b[0]
    zsum = jnp.zeros_like(tau)
    inf = jnp.full_like(tau, jnp.inf)
    thr_rank = []
    for r, cand in enumerate(lists):
        hits = [v >= tau for v in cand]
        for v, hit in zip(cand, hits):
            zsum = zsum + jnp.where(hit, jnp.exp(v - best), 0.0)
        if r < SUBLANES:
            t = inf
            for c, hit in enumerate(hits):
                t = jnp.where(hit, b[c], t)
            thr_rank.append(t)
        else:
            thr_rank.extend(jnp.where(hit, b[0], inf) for hit in hits)
    inv = 0.5 / zsum
    for h in range(PEER_HEADS):
        own = lambda v: jnp.broadcast_to(v[h:h + 1, :], (SUBLANES, LANES))
        a_h = [own(v) for v in a]
        thr_h = [own(v) for v in thr_rank]
        inv_h, b0_h = own(inv), own(b[0])
        s0 = sc_ref[lt, 2 * h]
        s1 = sc_ref[lt, 2 * h + 1]
        for g in range(s0.shape[0] // SUBLANES):
            rows = slice(g * SUBLANES, (g + 1) * SUBLANES)
            x0 = s0[rows, :]
            thr = jnp.full_like(x0, jnp.inf)
            for r in range(PEER_TOPK):
                thr = jnp.where(x0 == a_h[r], thr_h[r], thr)
            thr_scr[lt, h, rows, :] = thr
            pw_scr[lt, h, rows, :] = jnp.exp(x0 - a_h[0]) * inv_h
            q_scr[lt, h, rows, :] = jnp.exp(s1[rows, :] - b0_h)


MXU_TILE = 256
MXU_COUNT = 2
ACC_ROWS = 512
ACC_PIECE = 32
ACC_PRE = 0
ACC_OUT = ACC_ROWS // 4


def _peer_dense_kernel(sc_ref, hnt_ref, u_ref, vt_ref, res_ref, nw_ref, y_ref, thr_scr, pw_scr, q_scr,
                       h0_scr, h1_scr, g0_scr, g1_scr, yt_ref):
    s = pl.program_id(1)
    n_e = pl.num_programs(1) - 2
    d, tb = hnt_ref.shape
    n_lt, _, nk, _ = sc_ref.shape
    eb = u_ref.shape[0]
    n_i = eb // nk
    assert tb == MXU_COUNT * MXU_TILE and eb % ACC_ROWS == 0 and d % ACC_ROWS == 0

    @pl.when(s == 0)
    def _select():
        def body(lt, carry):
            _peer_select(sc_ref, thr_scr, pw_scr, q_scr, lt)
            return carry
        lax.fori_loop(0, n_lt, body, 0)
        yt_ref[...] = jnp.zeros_like(yt_ref)
        g0_scr[...] = jnp.zeros_like(g0_scr)
        g1_scr[...] = jnp.zeros_like(g1_scr)

    i0 = pl.multiple_of(jnp.clip(s - 1, 0, n_e - 1) * n_i, SUBLANES)

    def gate_steps(lt, ii, h_r, g_w):
        lanes = slice(lt * LANES, (lt + 1) * LANES)
        rows = slice(ii * nk, (ii + 1) * nk)
        tiled = (nk // SUBLANES, SUBLANES, LANES)
        state = {"w": jnp.zeros(tiled, F32)}

        def head(h):
            thr = thr_scr[lt, h, pl.ds(i0 + ii, SUBLANES, stride=0), :]
            pw = pw_scr[lt, h, pl.ds(i0 + ii, SUBLANES, stride=0), :]
            s1 = sc_ref[lt, 2 * h + 1].reshape(tiled)
            state["w"] = state["w"] + jnp.where(s1 >= thr[None], q_scr[lt, h].reshape(tiled), 0.0) * pw[None]
            if h == PEER_HEADS - 1:
                x = h_r[rows, lanes]
                act = x + x * lax.erf(x * (2.0 ** -0.5))
                g_w[rows, lanes] = (state["w"].reshape(nk, LANES) * act).astype(g_w.dtype)

        return [functools.partial(head, h) for h in range(PEER_HEADS)]

    def mxu_steps(kind, c, k, reg, h_w, g_r):
        lhs_ref, rhs_ref, acc, n_k = ((u_ref, hnt_ref, ACC_PRE, d // MXU_TILE) if kind == "pre"
                                      else (vt_ref, g_r, ACC_OUT, eb // MXU_TILE))
        kc = slice(k * MXU_TILE, (k + 1) * MXU_TILE)

        def push():
            for q in range(MXU_COUNT):
                pltpu.matmul_push_rhs(rhs_ref[kc, q * MXU_TILE:(q + 1) * MXU_TILE], staging_register=reg,
                                      mxu_index=q)

        def piece(p):
            r0 = c * ACC_ROWS + p * ACC_PIECE
            lhs = lhs_ref[r0:r0 + ACC_PIECE, kc]
            for q in range(MXU_COUNT):
                pltpu.matmul_acc_lhs(acc + p * ACC_PIECE // 4, lhs, q, load_staged_rhs=reg if p == 0 else None)

        def pop(p):
            rows = slice(c * ACC_ROWS + p * ACC_PIECE, c * ACC_ROWS + (p + 1) * ACC_PIECE)
            for q in range(MXU_COUNT):
                cols = slice(q * MXU_TILE, (q + 1) * MXU_TILE)
                res = pltpu.matmul_pop(acc + p * ACC_PIECE // 4, (ACC_PIECE, MXU_TILE), F32, q)
                if kind == "pre":
                    h_w[rows, cols] = res
                else:
                    yt_ref[rows, cols] += res

        n_p = ACC_ROWS // ACC_PIECE
        pops = [functools.partial(pop, p) for p in range(n_p)] if k == n_k - 1 else []
        return push, [functools.partial(piece, p) for p in range(n_p)], pops

    def stage(h_w, h_r, g_w, g_r, pre, gating, out):
        pre_groups = [("pre", c, k) for c in range(eb // ACC_ROWS) for k in range(d // MXU_TILE)] if pre else []
        out_groups = [("out", c, k) for c in range(d // ACC_ROWS) for k in range(eb // MXU_TILE)] if out else []
        if pre and out:
            order = [g for pair in zip(pre_groups, out_groups) for g in pair]
        else:
            order = pre_groups + out_groups
        steps = [mxu_steps(kind, c, k, gi % 2, h_w, g_r) for gi, (kind, c, k) in enumerate(order)]
        mxu = [steps[0][0]]
        lagged = []
        for gi, (_, pieces, pops) in enumerate(steps):
            half = len(pieces) // 2
            for p, piece in enumerate(pieces):
                if p == half and gi + 1 < len(steps):
                    mxu.append(steps[gi + 1][0])
                mxu.append(piece)
                if lagged:
                    mxu.append(lagged.pop(0))
            same_acc_next = gi + 1 < len(steps) and order[gi + 1][0] == order[gi][0]
            if same_acc_next or gi + 1 == len(steps):
                mxu.extend(pops)
            else:
                lagged = list(pops)
        vpu = [t for lt in range(n_lt) for ii in range(n_i) for t in gate_steps(lt, ii, h_r, g_w)] if gating else []
        im = iv = 0
        while im < len(mxu) or iv < len(vpu):
            if iv >= len(vpu) or (im < len(mxu) and im * len(vpu) <= iv * len(mxu)):
                mxu[im]()
                im += 1
            else:
                vpu[iv]()
                iv += 1

    last = n_e + 1

    @pl.when(s == 0)
    def _first():
        stage(h0_scr, None, None, None, True, False, False)

    @pl.when((s > 0) & (s < last) & (s % 2 == 0))
    def _even():
        stage(h0_scr, h1_scr, g1_scr, g0_scr, True, True, True)

    @pl.when((s > 0) & (s < last) & (s % 2 == 1))
    def _odd():
        stage(h1_scr, h0_scr, g0_scr, g1_scr, True, True, True)

    def finish(g_r):
        stage(None, None, None, g_r, False, False, True)
        y = res_ref[...] + yt_ref[...].T
        y_ref[...] = y * lax.rsqrt(jnp.mean(y * y, axis=-1, keepdims=True) + EPS) * nw_ref[...]

    @pl.when((s == last) & (s % 2 == 0))
    def _last_even():
        finish(g0_scr)

    @pl.when((s == last) & (s % 2 == 1))
    def _last_odd():
        finish(g1_scr)


def _peer_dense(sc, hnt, u_bf16, vt_bf16, resid, norm_w, tb, eb):
    _, nhc, nk, _ = sc.shape
    d, t = hnt.shape
    n_exp = u_bf16.shape[0]
    n_lt = tb // LANES
    assert eb == SUBLANES * nk and n_exp == nk * nk and t % tb == 0 and tb % LANES == 0
    sel = pltpu.VMEM((n_lt, PEER_HEADS, nk, LANES), F32)
    pre = pltpu.VMEM((eb, tb + LANES), F32)
    gated = pltpu.VMEM((eb, tb), BF16)
    n_e = n_exp // eb
    return pl.pallas_call(
        _peer_dense_kernel,
        grid=(t // tb, n_e + 2),
        in_specs=[pl.BlockSpec((n_lt, nhc, nk, LANES), lambda i, s: (i, 0, 0, 0)),
                  pl.BlockSpec((d, tb), lambda i, s: (0, i)),
                  pl.BlockSpec((eb, d), lambda i, s: (jnp.minimum(s, n_e - 1), 0)),
                  pl.BlockSpec((d, eb), lambda i, s: (0, jnp.clip(s - 2, 0, n_e - 1))),
                  pl.BlockSpec((tb, d), lambda i, s: (i, 0)),
                  pl.BlockSpec((1, d), lambda i, s: (0, 0))],
        out_specs=pl.BlockSpec((tb, d), lambda i, s: (i, 0)),
        out_shape=jax.ShapeDtypeStruct((t, d), F32),
        scratch_shapes=[sel, sel, sel, pre, pre, gated, gated, pltpu.VMEM((d, tb), F32)],
        compiler_params=_cparams("parallel", "arbitrary"),
        name="peer_dense",
    )(sc, hnt, u_bf16, vt_bf16, resid, norm_w.reshape(1, d))


TOKEN_BLOCK = 512
SWA_BLOCKS_PER_STEP = 8
EXPERT_BLOCK = 1024
SAMPLE_T_PAD = 16


def _ffn(x2d, attn, oh, w, tb):
    h, hnt, sc = _out_proj_scores(x2d, attn, oh, w['wo'], w['norm_ffn'], w['wqt'], w['keys'], tb)
    return _peer_dense(sc, hnt, w['u'], w['vt'], h, w['norm_final'], tb, EXPERT_BLOCK)


def kernel(x_prompt, x_sample, cache_k_win, cache_v_win, state_hgrn, norm_mix_w, w_in, attn_sinks,
           rel_bias_table, hg_lb, hg_norm_w, w_o, norm_ffn_w, peer_w_q, peer_sub_keys, peer_u, peer_v,
           norm_final_w):
    bsz, seq, d = x_prompt.shape
    dbsz, dseq, _ = x_sample.shape
    aw = ATTN_HEADS * HEAD_DIM
    kw = ATTN_KV_HEADS * HEAD_DIM
    hw = hg_norm_w.shape[1]
    wb = cache_k_win.shape[2]

    wi = w_in[0]
    w_in_r = jnp.concatenate([wi[:, :aw], wi[:, aw + 2 * kw:], wi[:, aw:aw + 2 * kw]], axis=1).astype(BF16)
    col_k = (aw + 4 * hw) // kw
    col_v = col_k + 1
    hg_cols = (1, 2, 3, 4)
    lb = jax.nn.softmax(hg_lb.astype(F32), axis=0)[0]
    nhc = PEER_HEADS * 2
    w = {
        'wo': w_o[0].astype(BF16),
        'norm_ffn': norm_ffn_w[0],
        'wqt': peer_w_q[0].astype(BF16).T,
        'keys': peer_sub_keys[0].reshape(nhc, peer_sub_keys.shape[3], peer_sub_keys.shape[4]).astype(BF16),
        'u': peer_u[0].astype(BF16),
        'vt': peer_v[0].astype(BF16).T,
        'norm_final': norm_final_w,
    }
    sinks = attn_sinks[0].astype(F32)

    proj_p = _in_proj(x_prompt.reshape(bsz * seq, d), norm_mix_w[0], w_in_r, TOKEN_BLOCK)
    proj_p3 = proj_p.reshape(bsz, seq, -1)
    blk = WINDOW
    dist_p = (jnp.arange(blk)[:, None] + blk) - jnp.arange(2 * blk)[None, :]
    in_win = (dist_p >= 0) & (dist_p <= WINDOW)
    has_prev = jnp.arange(2 * blk)[None, :] >= blk
    bias_p = jnp.stack([_masked_bias(rel_bias_table, blk, 2 * blk, blk, in_win & has_prev),
                        _masked_bias(rel_bias_table, blk, 2 * blk, blk, in_win)])
    attn_p = _swa_prompt(proj_p3, sinks, bias_p, 0, col_k, col_v)
    oh_p, st_p = _hgrn(proj_p3, lb, hg_norm_w[0], None, hg_cols, HG_CHUNK, HG_CHUNK, bsz)
    y_p = _ffn(x_prompt.reshape(bsz * seq, d), attn_p.reshape(bsz * seq, aw),
               oh_p.reshape(bsz * seq, hw), w, TOKEN_BLOCK)
    k_off = aw + 4 * hw
    wp = min(WINDOW, seq)
    k_win_p = proj_p3[:, seq - wp:, k_off:k_off + kw].reshape(1, bsz, wp, ATTN_KV_HEADS, HEAD_DIM)
    v_win_p = proj_p3[:, seq - wp:, k_off + kw:k_off + 2 * kw].reshape(1, bsz, wp, ATTN_KV_HEADS, HEAD_DIM)

    tp = SAMPLE_T_PAD
    xs_pad = jnp.pad(x_sample, ((0, 0), (0, tp - dseq), (0, 0)))
    proj_s3 = _in_proj(xs_pad.reshape(dbsz * tp, d), norm_mix_w[0], w_in_r, TOKEN_BLOCK).reshape(dbsz, tp, -1)
    k_new = proj_s3[:, :dseq, k_off:k_off + kw]
    v_new = proj_s3[:, :dseq, k_off + kw:k_off + 2 * kw]
    k_cache = cache_k_win[0].reshape(dbsz, wb, kw)
    v_cache = cache_v_win[0].reshape(dbsz, wb, kw)
    kk = jnp.concatenate([k_cache, k_new], axis=1)
    vv = jnp.concatenate([v_cache, v_new], axis=1)
    q_pos = jnp.arange(tp)[:, None]
    k_pos = jnp.arange(wb + tp)[None, :]
    dist_s = wb + q_pos - k_pos
    mask_s = (dist_s >= 0) & (dist_s <= WINDOW) & (q_pos < dseq) & (k_pos < wb + dseq)
    bias_s = _masked_bias(rel_bias_table, tp, wb + tp, wb, mask_s)
    attn_s = _swa_sample(proj_s3, k_cache, v_cache, sinks, bias_s[:, :, :wb], bias_s[:, :, wb:],
                         0, col_k, col_v, 16)
    oh_s, st_s = _hgrn(proj_s3, lb, hg_norm_w[0], state_hgrn[0], hg_cols, tp, dseq, 8)
    y_s = _ffn(x_sample.reshape(dbsz * dseq, d), attn_s[:, :dseq].reshape(dbsz * dseq, aw),
               oh_s[:, :dseq].reshape(dbsz * dseq, hw), w, TOKEN_BLOCK)
    k_win_s = kk[:, dseq:].reshape(1, dbsz, wb, ATTN_KV_HEADS, HEAD_DIM)
    v_win_s = vv[:, dseq:].reshape(1, dbsz, wb, ATTN_KV_HEADS, HEAD_DIM)

    return (y_p.reshape(bsz, seq, d), y_s.reshape(dbsz, dseq, d), k_win_p, v_win_p, st_p[None],
            k_win_s, v_win_s, st_s[None])
```
